```python
import math
import jax, jax.numpy as jnp
from jax import lax
import numpy as np

D_MODEL = 2048
BATCH = 2
SEQ = 4096
DEPTH = 1
DEC_BATCH = 8
DEC_SEQ = 1
PAST_LEN = 16384
PAGE_SIZE = 128

ATTN_V_DIM = 128
ATTN_HEADS = (D_MODEL // 2) // ATTN_V_DIM
ATTN_QK_DIM = ATTN_V_DIM // 2
ATTN_WIDTH = ATTN_HEADS * ATTN_V_DIM
QK_WIDTH = ATTN_HEADS * 2 * ATTN_QK_DIM
SSM_WIDTH = D_MODEL - ATTN_WIDTH
SSM_CH = 16
SSM_GROUPS = SSM_WIDTH // SSM_CH
SSM_STATE = 64
MIX_WIDTH = ATTN_WIDTH + SSM_WIDTH
IN_WIDTH = 2 * QK_WIDTH + ATTN_WIDTH + SSM_WIDTH
DT_MIN = 1e-3
DT_MAX = 1e-1
PEER_HEADS = 8
PEER_TOPK = 16
PEER_NKEYS = 128
PEER_EXPERTS = PEER_NKEYS * PEER_NKEYS
PEER_QDIM = 256
PEER_HALF = PEER_QDIM // 2
PEER_BLOCK = 128
Q_BLOCK = 128
ROPE_THETA = 10000.0
EPS = 1e-6
NEG_INF = -1e30

kernel_name = 'hybrid_diffattn_s5_peer_decode_step'


def rmsnorm(x, g):
    xf = x.astype(jnp.float32)
    y = xf * lax.rsqrt(jnp.mean(xf * xf, axis=-1, keepdims=True) + EPS) * g.astype(jnp.float32)
    return y.astype(x.dtype)


def rope(x, pos):
    half = ATTN_QK_DIM // 2
    inv = 1.0 / (ROPE_THETA ** (jnp.arange(half, dtype=jnp.float32) * 2.0 / ATTN_QK_DIM))
    ang = pos.astype(jnp.float32)[:, None] * inv[None, :]
    c = jnp.cos(ang)[:, None, None, :]
    s = jnp.sin(ang)[:, None, None, :]
    xf = x.astype(jnp.float32)
    x1, x2 = xf[..., :half], xf[..., half:]
    return jnp.concatenate([x1 * c - x2 * s, x2 * c + x1 * s], axis=-1).astype(x.dtype)


def diff_attend(q, k, v, q_pos, k_pos, lam, g_sub, lam_init):
    b, tq = q.shape[0], q.shape[1]
    qb = min(Q_BLOCK, tq)
    nq = -(-tq // qb)
    pad = nq * qb - tq
    qp = jnp.pad(q, ((0, 0), (0, pad), (0, 0), (0, 0), (0, 0)))
    pp = jnp.pad(q_pos, (0, pad), mode='edge')
    q_blocks = qp.reshape(b, nq, qb, ATTN_HEADS, 2, ATTN_QK_DIM).swapaxes(0, 1)
    p_blocks = pp.reshape(nq, qb)
    kf = k.astype(jnp.float32)
    vf = v.astype(jnp.float32)
    scale = ATTN_QK_DIM ** -0.5

    def one_block(args):
        qi, pi = args
        s = jnp.einsum('bqhcd,bkhcd->bchqk', qi.astype(jnp.float32), kf) * scale
        mask = pi[:, None] >= k_pos[None, :]
        s = jnp.where(mask, s, NEG_INF)
        p = jax.nn.softmax(s, axis=-1)
        w = p[:, 0] - lam * p[:, 1]
        return jnp.einsum('bhqk,bkhd->bqhd', w, vf)

    o = lax.map(one_block, (q_blocks, p_blocks))
    o = o.swapaxes(0, 1).reshape(b, nq * qb, ATTN_HEADS, ATTN_V_DIM)[:, :tq]
    o = o * lax.rsqrt(jnp.mean(o * o, axis=-1, keepdims=True) + EPS)
    o = o * g_sub.astype(jnp.float32) * (1.0 - lam_init)
    return o.reshape(b, tq, ATTN_WIDTH)


def s5_scan(u, h0_re, h0_im, a_re, a_im, log_dt, b_re, b_im, c_re, c_im, d):
    uf = u.astype(jnp.float32)
    are = a_re.astype(jnp.float32)
    aim = a_im.astype(jnp.float32)
    dt = jnp.exp(log_dt.astype(jnp.float32))[:, None]
    mag = jnp.exp(are * dt)
    lr = mag * jnp.cos(aim * dt)
    li = mag * jnp.sin(aim * dt)
    den = are * are + aim * aim
    zr = lr - 1.0
    fr = (zr * are + li * aim) / den
    fi = (li * are - zr * aim) / den
    br = b_re.astype(jnp.float32)
    bi = b_im.astype(jnp.float32)
    bbar_re = fr[..., None] * br - fi[..., None] * bi
    bbar_im = fr[..., None] * bi + fi[..., None] * br
    bu_re = jnp.einsum('btgc,gpc->btgp', uf, bbar_re)
    bu_im = jnp.einsum('btgc,gpc->btgp', uf, bbar_im)
    al_re = jnp.broadcast_to(lr, bu_re.shape)
    al_im = jnp.broadcast_to(li, bu_im.shape)

    def combine(e1, e2):
        a1r, a1i, b1r, b1i = e1
        a2r, a2i, b2r, b2i = e2
        return (a1r * a2r - a1i * a2i,
                a1r * a2i + a1i * a2r,
                a2r * b1r - a2i * b1i + b2r,
                a2r * b1i + a2i * b1r + b2i)

    cum_r, cum_i, hr, hi = lax.associative_scan(combine, (al_re, al_im, bu_re, bu_im), axis=1)
    h0r = h0_re.astype(jnp.float32)[:, None]
    h0i = h0_im.astype(jnp.float32)[:, None]
    hr = hr + cum_r * h0r - cum_i * h0i
    hi = hi + cum_r * h0i + cum_i * h0r
    y = (jnp.einsum('btgp,gcp->btgc', hr, c_re.astype(jnp.float32))
         - jnp.einsum('btgp,gcp->btgc', hi, c_im.astype(jnp.float32))
         + d.astype(jnp.float32) * uf)
    b, t = u.shape[0], u.shape[1]
    return y.reshape(b, t, SSM_WIDTH), hr[:, -1], hi[:, -1]


def peer(h, w_q, sub_keys, u_tab, v_tab):
    n, dm = h.shape
    nb = -(-n // PEER_BLOCK)
    pad = nb * PEER_BLOCK - n
    hb_all = jnp.pad(h, ((0, pad), (0, 0))).reshape(nb, PEER_BLOCK, dm)
    keys = sub_keys.astype(jnp.float32)

    def block(hb):
        q = (hb @ w_q).astype(jnp.float32).reshape(PEER_BLOCK, PEER_HEADS, 2, PEER_HALF)
        s = jnp.einsum('thcd,hckd->thck', q, keys)
        s1, i1 = lax.top_k(s[:, :, 0], PEER_TOPK)
        s2, i2 = lax.top_k(s[:, :, 1], PEER_TOPK)
        ncand = PEER_TOPK * PEER_TOPK
        cand_s = (s1[..., :, None] + s2[..., None, :]).reshape(PEER_BLOCK, PEER_HEADS, ncand)
        cand_i = (i1[..., :, None] * PEER_NKEYS + i2[..., None, :]).reshape(PEER_BLOCK, PEER_HEADS, ncand)
        top_s, top_p = lax.top_k(cand_s, PEER_TOPK)
        idx = jnp.take_along_axis(cand_i, top_p, axis=-1)
        gate = jax.nn.softmax(top_s, axis=-1)
        u = u_tab[idx]
        act = jax.nn.gelu(jnp.einsum('thkd,td->thk', u, hb).astype(jnp.float32))
        v = v_tab[idx].astype(jnp.float32)
        return jnp.einsum('thk,thkd->td', gate * act, v).astype(h.dtype)

    return lax.map(block, hb_all).reshape(nb * PEER_BLOCK, dm)[:n]


def decoder_layer(x, pos, past_k, past_v, k_pos, h0_re, h0_im, lp, lam_init):
    b, t, _ = x.shape
    h = rmsnorm(x, lp['g_mix'])
    z = h @ lp['w_in']
    q, k, v, u = jnp.split(z, [QK_WIDTH, 2 * QK_WIDTH, 2 * QK_WIDTH + ATTN_WIDTH], axis=-1)
    q = rope(q.reshape(b, t, ATTN_HEADS, 2, ATTN_QK_DIM), pos)
    k = rope(k.reshape(b, t, ATTN_HEADS, 2, ATTN_QK_DIM), pos)
    v = v.reshape(b, t, ATTN_HEADS, ATTN_V_DIM)
    if past_k is None:
        k_all, v_all = k, v
    else:
        k_all = jnp.concatenate([past_k.astype(k.dtype), k], axis=1)
        v_all = jnp.concatenate([past_v.astype(v.dtype), v], axis=1)
    lam = (jnp.exp(jnp.sum(lp['lq1'].astype(jnp.float32) * lp['lk1'].astype(jnp.float32)))
           - jnp.exp(jnp.sum(lp['lq2'].astype(jnp.float32) * lp['lk2'].astype(jnp.float32)))
           + lam_init)
    attn = diff_attend(q, k_all, v_all, pos, k_pos, lam, lp['g_sub'], lam_init)
    y, hr, hi = s5_scan(u.reshape(b, t, SSM_GROUPS, SSM_CH), h0_re, h0_im,
                        lp['a_re'], lp['a_im'], lp['log_dt'], lp['b_re'], lp['b_im'],
                        lp['c_re'], lp['c_im'], lp['d'])
    y = jax.nn.gelu(y)
    y = y * jax.nn.sigmoid(y @ lp['w_glu'].astype(jnp.float32) + lp['b_glu'].astype(jnp.float32))
    mix = jnp.concatenate([attn.astype(x.dtype), y.astype(x.dtype)], axis=-1)
    x = x + mix @ lp['w_out']
    h2 = rmsnorm(x, lp['g_ffn'])
    x = x + peer(h2.reshape(b * t, D_MODEL), lp['pq'], lp['pk'], lp['pu'], lp['pv']).reshape(b, t, D_MODEL)
    k_rows = k.reshape(b, t, ATTN_HEADS, 2 * ATTN_QK_DIM)
    return x, k_rows, v, hr, hi


def setup_inputs(seed: int = 0) -> dict:
    key = jax.random.key(seed)
    ks = jax.random.split(key, 32)
    f32 = jnp.float32
    n_pages = PAST_LEN // PAGE_SIZE
    n_used = DEC_BATCH * n_pages
    n_pool = n_used + (n_used + 3) // 4
    nrm = lambda k, shape, s: jax.random.normal(k, shape, f32) * s
    page_table = jax.random.permutation(ks[0], n_pool)[:n_used].reshape(DEC_BATCH, n_pages).astype(jnp.int32)
    a_im_base = math.pi * jnp.arange(SSM_STATE, dtype=f32)
    return {
        'x_prompt': nrm(ks[1], (BATCH, SEQ, D_MODEL), 1.0),
        'x_sample': nrm(ks[2], (DEC_BATCH, DEC_SEQ, D_MODEL), 1.0),
        'cache_k': nrm(ks[3], (DEPTH, n_pool, PAGE_SIZE, ATTN_HEADS, 2 * ATTN_QK_DIM), 1.0),
        'cache_v': nrm(ks[4], (DEPTH, n_pool, PAGE_SIZE, ATTN_HEADS, ATTN_V_DIM), 1.0),
        'state_ssm_re': nrm(ks[5], (DEPTH, DEC_BATCH, SSM_GROUPS, SSM_STATE), 0.5),
        'state_ssm_im': nrm(ks[6], (DEPTH, DEC_BATCH, SSM_GROUPS, SSM_STATE), 0.5),
        'page_table': page_table,
        'g_mix': 1.0 + nrm(ks[7], (DEPTH, D_MODEL), 0.02),
        'w_in': nrm(ks[8], (DEPTH, D_MODEL, IN_WIDTH), D_MODEL ** -0.5),
        'lambda_q1': nrm(ks[9], (DEPTH, ATTN_QK_DIM), 0.1),
        'lambda_k1': nrm(ks[10], (DEPTH, ATTN_QK_DIM), 0.1),
        'lambda_q2': nrm(ks[11], (DEPTH, ATTN_QK_DIM), 0.1),
        'lambda_k2': nrm(ks[12], (DEPTH, ATTN_QK_DIM), 0.1),
        'g_sub': 1.0 + nrm(ks[13], (DEPTH, ATTN_V_DIM), 0.02),
        'ssm_a_re': -0.5 + nrm(ks[14], (DEPTH, SSM_GROUPS, SSM_STATE), 0.01),
        'ssm_a_im': a_im_base + nrm(ks[15], (DEPTH, SSM_GROUPS, SSM_STATE), 0.01),
        'ssm_log_dt': jax.random.uniform(ks[16], (DEPTH, SSM_GROUPS), f32, math.log(DT_MIN), math.log(DT_MAX)),
        'ssm_b_re': nrm(ks[17], (DEPTH, SSM_GROUPS, SSM_STATE, SSM_CH), (2 * SSM_CH) ** -0.5),
        'ssm_b_im': nrm(ks[18], (DEPTH, SSM_GROUPS, SSM_STATE, SSM_CH), (2 * SSM_CH) ** -0.5),
        'ssm_c_re': nrm(ks[19], (DEPTH, SSM_GROUPS, SSM_CH, SSM_STATE), (2 * SSM_STATE) ** -0.5),
        'ssm_c_im': nrm(ks[20], (DEPTH, SSM_GROUPS, SSM_CH, SSM_STATE), (2 * SSM_STATE) ** -0.5),
        'ssm_d': nrm(ks[21], (DEPTH, SSM_GROUPS, SSM_CH), 1.0),
        'w_glu': nrm(ks[22], (DEPTH, SSM_WIDTH, SSM_WIDTH), SSM_WIDTH ** -0.5),
        'b_glu': nrm(ks[23], (DEPTH, SSM_WIDTH), 0.01),
        'w_out': nrm(ks[24], (DEPTH, MIX_WIDTH, D_MODEL), MIX_WIDTH ** -0.5),
        'g_ffn': 1.0 + nrm(ks[25], (DEPTH, D_MODEL), 0.02),
        'peer_w_q': nrm(ks[26], (DEPTH, D_MODEL, PEER_HEADS * PEER_QDIM), D_MODEL ** -0.5),
        'peer_sub_keys': nrm(ks[27], (DEPTH, PEER_HEADS, 2, PEER_NKEYS, PEER_HALF), PEER_HALF ** -0.5),
        'peer_u': nrm(ks[28], (DEPTH, PEER_EXPERTS, D_MODEL), D_MODEL ** -0.5),
        'peer_v': nrm(ks[29], (DEPTH, PEER_EXPERTS, D_MODEL), PEER_HEADS ** -0.5),
        'g_final': 1.0 + nrm(ks[30], (D_MODEL,), 0.02),
    }


def reference(x_prompt, x_sample, cache_k, cache_v, state_ssm_re, state_ssm_im, page_table,
              g_mix, w_in, lambda_q1, lambda_k1, lambda_q2, lambda_k2, g_sub,
              ssm_a_re, ssm_a_im, ssm_log_dt, ssm_b_re, ssm_b_im, ssm_c_re, ssm_c_im, ssm_d,
              w_glu, b_glu, w_out, g_ffn, peer_w_q, peer_sub_keys, peer_u, peer_v, g_final):
    bp, tp = x_prompt.shape[0], x_prompt.shape[1]
    bs, ts = x_sample.shape[0], x_sample.shape[1]
    n_pages = page_table.shape[1]
    past_len = n_pages * PAGE_SIZE
    pos_p = jnp.arange(tp)
    pos_s = past_len + jnp.arange(ts)
    kpos_s = jnp.concatenate([jnp.arange(past_len), pos_s])
    h0_zero = jnp.zeros((bp, SSM_GROUPS, SSM_STATE), jnp.float32)
    xp, xs = x_prompt, x_sample
    kp_l, vp_l, rp_l, ip_l, ks_l, vs_l, rs_l, is_l = [], [], [], [], [], [], [], []
    for l in range(DEPTH):
        lam_init = 0.8 - 0.6 * math.exp(-0.3 * l)
        lp = {'g_mix': g_mix[l], 'w_in': w_in[l], 'lq1': lambda_q1[l], 'lk1': lambda_k1[l],
              'lq2': lambda_q2[l], 'lk2': lambda_k2[l], 'g_sub': g_sub[l],
              'a_re': ssm_a_re[l], 'a_im': ssm_a_im[l], 'log_dt': ssm_log_dt[l],
              'b_re': ssm_b_re[l], 'b_im': ssm_b_im[l], 'c_re': ssm_c_re[l], 'c_im': ssm_c_im[l],
              'd': ssm_d[l], 'w_glu': w_glu[l], 'b_glu': b_glu[l], 'w_out': w_out[l],
              'g_ffn': g_ffn[l], 'pq': peer_w_q[l], 'pk': peer_sub_keys[l],
              'pu': peer_u[l], 'pv': peer_v[l]}
        xp, kp, vp, rp, ip = decoder_layer(xp, pos_p, None, None, pos_p, h0_zero, h0_zero, lp, lam_init)
        pk = cache_k[l][page_table].reshape(bs, past_len, ATTN_HEADS, 2, ATTN_QK_DIM)
        pv = cache_v[l][page_table].reshape(bs, past_len, ATTN_HEADS, ATTN_V_DIM)
        xs, kn, vn, rn, inn = decoder_layer(xs, pos_s, pk, pv, kpos_s,
                                            state_ssm_re[l], state_ssm_im[l], lp, lam_init)
        kp_l.append(kp); vp_l.append(vp); rp_l.append(rp); ip_l.append(ip)
        ks_l.append(kn); vs_l.append(vn); rs_l.append(rn); is_l.append(inn)
    y_prompt = rmsnorm(xp, g_final)
    y_sample = rmsnorm(xs, g_final)
    return (y_prompt, y_sample,
            jnp.stack(kp_l), jnp.stack(vp_l), jnp.stack(rp_l), jnp.stack(ip_l),
            jnp.stack(ks_l), jnp.stack(vs_l), jnp.stack(rs_l), jnp.stack(is_l))
```

```python
import functools
import math

import jax
import jax.numpy as jnp
from jax import lax
from jax.experimental import pallas as pl
from jax.experimental.pallas import tpu as pltpu

F32 = jnp.float32
BF16 = jnp.bfloat16

D_MODEL = 2048
PAGE = 128
HEADS = 8
QK = 64
VD = 128
AW = HEADS * VD
SSM_W = D_MODEL - AW
SSM_CH = 16
SSM_G = SSM_W // SSM_CH
SSM_P = 64
GB = 16
NB = SSM_G // GB
SL = GB * SSM_P
UL = GB * SSM_CH
P_HEADS = 8
P_TOPK = 16
P_NK = 128
P_HALF = 128
P_E = P_NK * P_NK
EPS = 1e-6
NEG = -1e30
ROPE_THETA = 10000.0
LAM_INIT = 0.8 - 0.6 * math.exp(-0.3 * 0)
VMEM_LIMIT = 56 * 1024 * 1024

STAIR = [(i, j) for i in range(P_TOPK) for j in range(P_TOPK) if (i + 1) * (j + 1) <= P_TOPK]
STAIR_ROWS = -(-len(STAIR) // 8) * 8


def _cp(*sem):
    return pltpu.CompilerParams(dimension_semantics=sem, vmem_limit_bytes=VMEM_LIMIT)


def _inproj_body(x_ref, g_ref, w_ref, cos_ref, sin_ref,
                 q_ref, k_ref, kb_ref, v_ref, vb_ref, u_ref, h_scr):
    j = pl.program_id(1)

    @pl.when(j == 0)
    def _():
        x = x_ref[...]
        ms = jnp.mean(x * x, axis=-1, keepdims=True)
        h_scr[...] = (x * lax.rsqrt(ms + EPS) * g_ref[...]).astype(BF16)

    z = jnp.dot(h_scr[...], w_ref[...], preferred_element_type=F32)

    def rope(zc):
        lane = lax.broadcasted_iota(jnp.int32, zc.shape, 1)
        first = (lane & (QK - 1)) < (QK // 2)
        partner = jnp.where(first, pltpu.roll(zc, 128 - QK // 2, 1), pltpu.roll(zc, QK // 2, 1))
        return zc * cos_ref[...] + partner * sin_ref[...]

    @pl.when(j == 0)
    def _():
        for c in range(AW // 128):
            sl = slice(c * 128, (c + 1) * 128)
            q_ref[:, sl] = (rope(z[:, sl]) * (QK ** -0.5)).astype(BF16)

    @pl.when(j == 1)
    def _():
        for c in range(AW // 128):
            sl = slice(c * 128, (c + 1) * 128)
            kc = rope(z[:, sl])
            k_ref[:, sl] = kc
            kb_ref[:, sl] = kc.astype(BF16)

    @pl.when(j == 2)
    def _():
        v_ref[...] = z
        vb_ref[...] = z.astype(BF16)

    @pl.when(j == 3)
    def _():
        u_ref[...] = z


def _inproj(x, g, w_bf, cos, sin, tm):
    m = x.shape[0]
    row = lambda i, j: (i, 0)
    outs = [jax.ShapeDtypeStruct((m, AW), BF16), jax.ShapeDtypeStruct((m, AW), F32),
            jax.ShapeDtypeStruct((m, AW), BF16), jax.ShapeDtypeStruct((m, AW), F32),
            jax.ShapeDtypeStruct((m, AW), BF16), jax.ShapeDtypeStruct((m, SSM_W), F32)]
    return pl.pallas_call(
        _inproj_body,
        grid=(m // tm, 4),
        in_specs=[pl.BlockSpec((tm, D_MODEL), row),
                  pl.BlockSpec((1, D_MODEL), lambda i, j: (0, 0)),
                  pl.BlockSpec((D_MODEL, AW), lambda i, j: (0, j)),
                  pl.BlockSpec((tm, 128), row),
                  pl.BlockSpec((tm, 128), row)],
        out_specs=[pl.BlockSpec((tm, AW), row) for _ in outs],
        out_shape=outs,
        scratch_shapes=[pltpu.VMEM((tm, D_MODEL), BF16)],
        compiler_params=_cp("parallel", "arbitrary"),
        name="inproj",
    )(x, g, w_bf, cos, sin)


def _lam(lam4_ref):
    l4 = lam4_ref[...]
    a = jnp.exp(jnp.sum(l4[0:1, :] * l4[1:2, :], axis=-1, keepdims=True))
    b = jnp.exp(jnp.sum(l4[2:3, :] * l4[3:4, :], axis=-1, keepdims=True))
    return a - b + LAM_INIT


def _subnorm(o, gsub):
    o = o * lax.rsqrt(jnp.mean(o * o, axis=-1, keepdims=True) + EPS)
    return o * gsub * (1.0 - LAM_INIT)


def _attn_body(lam4_ref, q_ref, k_ref, v_ref, gsub_ref, o_ref, m_scr, l_scr, acc_scr, *, tq):
    qi = pl.program_id(2)
    q = q_ref[...]
    lane = lax.broadcasted_iota(jnp.int32, q.shape, 1)
    zero = jnp.zeros_like(q)
    q2 = jnp.concatenate([jnp.where(lane < QK, q, zero), jnp.where(lane >= QK, q, zero)], axis=0)

    m_scr[...] = jnp.full(m_scr.shape, NEG, F32)
    l_scr[...] = jnp.zeros(l_scr.shape, F32)
    acc_scr[...] = jnp.zeros(acc_scr.shape, F32)

    def step(kv, masked):
        rows = pl.ds(pl.multiple_of(kv * tq, tq), tq)
        k = k_ref[rows, :]
        v = v_ref[rows, :]
        s = lax.dot_general(q2, k, (((1,), (1,)), ((), ())), preferred_element_type=F32)
        if masked:
            r = lax.broadcasted_iota(jnp.int32, s.shape, 0) & (tq - 1)
            c = lax.broadcasted_iota(jnp.int32, s.shape, 1)
            s = jnp.where(r >= c, s, NEG)
        m_prev = m_scr[...]
        m_new = jnp.maximum(m_prev, jnp.max(s, axis=-1, keepdims=True))
        alpha = jnp.exp(m_prev - m_new)
        p = jnp.exp(s - m_new)
        l_scr[...] = alpha * l_scr[...] + jnp.sum(p, axis=-1, keepdims=True)
        acc_scr[...] = alpha * acc_scr[...] + jnp.dot(p.astype(BF16), v, preferred_element_type=F32)
        m_scr[...] = m_new

    def body(kv, carry):
        step(kv, False)
        return carry

    lax.fori_loop(0, qi, body, 0)
    step(qi, True)

    o = acc_scr[...] / l_scr[...]
    o = o[:tq] - _lam(lam4_ref) * o[tq:]
    o_ref[...] = _subnorm(o, gsub_ref[...])


def _attn_prompt(lam4, qb, kb, vb, gsub, nb, t, tq):
    qb = qb.reshape(nb, t, AW)
    kb = kb.reshape(nb, t, AW)
    vb = vb.reshape(nb, t, AW)
    out = pl.pallas_call(
        functools.partial(_attn_body, tq=tq),
        grid=(nb, HEADS, t // tq),
        in_specs=[pl.BlockSpec((4, QK), lambda b, h, i: (0, 0)),
                  pl.BlockSpec((None, tq, VD), lambda b, h, i: (b, i, h)),
                  pl.BlockSpec((None, t, VD), lambda b, h, i: (b, 0, h)),
                  pl.BlockSpec((None, t, VD), lambda b, h, i: (b, 0, h)),
                  pl.BlockSpec((1, VD), lambda b, h, i: (0, 0))],
        out_specs=pl.BlockSpec((None, tq, VD), lambda b, h, i: (b, i, h)),
        out_shape=jax.ShapeDtypeStruct((nb, t, AW), F32),
        scratch_shapes=[pltpu.VMEM((2 * tq, 1), F32), pltpu.VMEM((2 * tq, 1), F32),
                        pltpu.VMEM((2 * tq, VD), F32)],
        compiler_params=_cp("parallel", "parallel", "arbitrary"),
        name="attn_prompt",
    )(lam4, qb, kb, vb, gsub)
    return out.reshape(nb * t, AW)


def _sattn_body(pt_ref, lam4_ref, q16_ref, kc_ref, vc_ref, kn_ref, vn_ref, gsub_ref, o_ref,
                m_scr, l_scr, acc_scr, *, n_pages):
    j = pl.program_id(1)
    q16 = q16_ref[...]

    @pl.when(j == 0)
    def _():
        m_scr[...] = jnp.full(m_scr.shape, NEG, F32)
        l_scr[...] = jnp.zeros(l_scr.shape, F32)
        acc_scr[...] = jnp.zeros(acc_scr.shape, F32)

    @pl.when(j < n_pages)
    def _():
        kp = kc_ref[...].astype(BF16)
        vp = vc_ref[...].astype(BF16)
        s = lax.dot_general(q16, kp, (((1,), (1,)), ((), ())), preferred_element_type=F32)
        m_prev = m_scr[...]
        m_new = jnp.maximum(m_prev, jnp.max(s, axis=-1, keepdims=True))
        alpha = jnp.exp(m_prev - m_new)
        p = jnp.exp(s - m_new)
        l_scr[...] = alpha * l_scr[...] + jnp.sum(p, axis=-1, keepdims=True)
        acc_scr[...] = alpha * acc_scr[...] + jnp.dot(p.astype(BF16), vp, preferred_element_type=F32)
        m_scr[...] = m_new

    @pl.when(j == n_pages)
    def _():
        kn = kn_ref[...].astype(BF16).astype(F32)
        vn = vn_ref[...].astype(BF16).astype(F32)
        s = jnp.sum(q16.astype(F32) * kn, axis=-1, keepdims=True)
        m_prev = m_scr[...]
        m_new = jnp.maximum(m_prev, s)
        alpha = jnp.exp(m_prev - m_new)
        p = jnp.exp(s - m_new)
        l = alpha * l_scr[...] + p
        acc = (alpha * acc_scr[...] + p.astype(BF16).astype(F32) * vn) / l
        lam = _lam(lam4_ref)
        gsub = gsub_ref[...]
        for h in range(HEADS):
            sl = slice(h * VD, (h + 1) * VD)
            o = acc[2 * h:2 * h + 1, sl] - lam * acc[2 * h + 1:2 * h + 2, sl]
            o_ref[h:h + 1, :] = _subnorm(o, gsub)


def _attn_sample(page_table, lam4, q16, ck, cv, kn, vn, gsub):
    bs, n_pages = page_table.shape
    pt = page_table.reshape(-1).astype(jnp.int32)

    def cache_idx(b, j, pt_ref):
        return (pt_ref[b * n_pages + jnp.minimum(j, n_pages - 1)], 0, 0)

    grid_spec = pltpu.PrefetchScalarGridSpec(
        num_scalar_prefetch=1,
        grid=(bs, n_pages + 1),
        in_specs=[pl.BlockSpec((4, QK), lambda b, j, p: (0, 0)),
                  pl.BlockSpec((None, 16, AW), lambda b, j, p: (b, 0, 0)),
                  pl.BlockSpec((None, PAGE, AW), cache_idx),
                  pl.BlockSpec((None, PAGE, AW), cache_idx),
                  pl.BlockSpec((None, 1, AW), lambda b, j, p: (b, 0, 0)),
                  pl.BlockSpec((None, 1, AW), lambda b, j, p: (b, 0, 0)),
                  pl.BlockSpec((1, VD), lambda b, j, p: (0, 0))],
        out_specs=pl.BlockSpec((None, HEADS, VD), lambda b, j, p: (b, 0, 0)),
        scratch_shapes=[pltpu.VMEM((16, 1), F32), pltpu.VMEM((16, 1), F32),
                        pltpu.VMEM((16, AW), F32)],
    )
    out = pl.pallas_call(
        functools.partial(_sattn_body, n_pages=n_pages),
        grid_spec=grid_spec,
        out_shape=jax.ShapeDtypeStruct((bs, HEADS, VD), F32),
        compiler_params=_cp("parallel", "arbitrary"),
        name="attn_sample",
    )(pt, lam4, q16, ck, cv, kn, vn, gsub)
    return out.reshape(bs, AW)


def _cmul(ar, ai, br, bi):
    return ar * br - ai * bi, ar * bi + ai * br


def _s5_prep_body(are_ref, aim_ref, ldt_ref, arer_ref, aimr_ref, ldtr_ref, bre_ref, bim_ref,
                  pow_ref, bbr_ref, bbi_ref):
    def disc(are, aim, ldt):
        dt = jnp.exp(ldt)
        mag = jnp.exp(are * dt)
        lr = mag * jnp.cos(aim * dt)
        li = mag * jnp.sin(aim * dt)
        return lr, li

    lr, li = disc(are_ref[...], aim_ref[...], ldt_ref[...])
    pr, pi = lr, li
    for n in range(8):
        pow_ref[2 * n] = pr
        pow_ref[2 * n + 1] = pi
        pr, pi = _cmul(pr, pi, lr, li)

    are, aim = arer_ref[...], aimr_ref[...]
    lrr, lir = disc(are, aim, ldtr_ref[...])
    den = are * are + aim * aim
    zr = lrr - 1.0
    fr = (zr * are + lir * aim) / den
    fi = (lir * are - zr * aim) / den
    br, bi = bre_ref[...], bim_ref[...]
    bbr_ref[...] = fr * br - fi * bi
    bbi_ref[...] = fr * bi + fi * br


def _s5_prep(a_re, a_im, log_dt, b_re, b_im):
    rep = lambda a: jnp.repeat(a, SSM_CH, axis=1)
    ldt = jnp.broadcast_to(log_dt[:, None], (SSM_G, SSM_P))
    pw, bbr, bbi = pl.pallas_call(
        _s5_prep_body,
        out_shape=[jax.ShapeDtypeStruct((16, SSM_G, SSM_P), F32),
                   jax.ShapeDtypeStruct((SSM_G, SSM_P * SSM_CH), F32),
                   jax.ShapeDtypeStruct((SSM_G, SSM_P * SSM_CH), F32)],
        name="s5_prep",
    )(a_re, a_im, ldt, rep(a_re), rep(a_im), rep(ldt),
      b_re.reshape(SSM_G, -1), b_im.reshape(SSM_G, -1))
    return pw, bbr.reshape(SSM_G, SSM_P, SSM_CH), bbi.reshape(SSM_G, SSM_P, SSM_CH)


def _s5_tables(pw, bbr, bbi, c_re, c_im, d):
    eye = jnp.eye(GB, dtype=F32)
    def blk_in(bb):
        x = bb.reshape(NB, GB, SSM_P, SSM_CH)
        return jnp.einsum('ngpc,gh->ngchp', x, eye).reshape(NB, UL, SL)
    bblk = jnp.concatenate([blk_in(bbr), blk_in(bbi)], axis=2).astype(BF16)
    def blk_out(cc):
        x = cc.reshape(NB, GB, SSM_CH, SSM_P)
        return jnp.einsum('ngcp,gh->ngphc', x, eye).reshape(NB, SL, UL)
    cblk = jnp.concatenate([blk_out(c_re), blk_out(-c_im)], axis=1).astype(BF16)
    lanes = lambda a: a.reshape(NB, 1, SL)
    t_idx = jnp.arange(8)[None, :, None]
    tabs = []
    for n in (1, 2, 4):
        keep = (t_idx >= n).astype(F32)
        tabs += [lanes(pw[2 * (n - 1)]) * keep, lanes(pw[2 * (n - 1) + 1]) * keep]
    pr = jnp.stack([pw[2 * n] for n in range(8)], axis=0).reshape(8, NB, SL).transpose(1, 0, 2)
    pi = jnp.stack([pw[2 * n + 1] for n in range(8)], axis=0).reshape(8, NB, SL).transpose(1, 0, 2)
    tabs += [pr, pi]
    tab = jnp.stack(tabs, axis=1)
    lam1 = jnp.stack([lanes(pw[0]), lanes(pw[1])], axis=1).reshape(NB, 2, SL)
    dvec = d.reshape(NB, 1, UL)
    return bblk, cblk, tab, lam1, dvec


def _s5_body(u_ref, bblk_ref, cblk_ref, d_ref, tab_ref, h0r_ref, h0i_ref,
             y_ref, hr_ref, hi_ref, st_scr, cr_scr, ci_scr, *, chunk):
    k = pl.program_id(2)

    @pl.when(k == 0)
    def _():
        cr_scr[...] = h0r_ref[...]
        ci_scr[...] = h0i_ref[...]

    u = u_ref[...]
    st_scr[...] = jnp.dot(u.astype(BF16), bblk_ref[...], preferred_element_type=F32)

    def tile(r, carry):
        cr, ci = carry
        rows = pl.ds(pl.multiple_of(r * 8, 8), 8)
        xr = st_scr[rows, 0:SL]
        xi = st_scr[rows, SL:2 * SL]
        for lvl, sh in enumerate((1, 2, 4)):
            ar, ai = tab_ref[2 * lvl], tab_ref[2 * lvl + 1]
            sr, si = pltpu.roll(xr, sh, 0), pltpu.roll(xi, sh, 0)
            xr, xi = xr + (ar * sr - ai * si), xi + (ar * si + ai * sr)
        pr, pi = tab_ref[6], tab_ref[7]
        hr = xr + (pr * cr - pi * ci)
        hi = xi + (pr * ci + pi * cr)
        st_scr[rows, 0:SL] = hr
        st_scr[rows, SL:2 * SL] = hi
        return hr[7:8, :], hi[7:8, :]

    cr, ci = lax.fori_loop(0, chunk // 8, tile, (cr_scr[...], ci_scr[...]))
    cr_scr[...] = cr
    ci_scr[...] = ci
    y_ref[...] = (jnp.dot(st_scr[...].astype(BF16), cblk_ref[...], preferred_element_type=F32)
                  + d_ref[...] * u)

    @pl.when(k == pl.num_programs(2) - 1)
    def _():
        hr_ref[...] = cr
        hi_ref[...] = ci


def _s5_prompt(u, bblk, cblk, dvec, tab, h0r, h0i, nb, t, chunk):
    nk = t // chunk
    st = jax.ShapeDtypeStruct((nb, 1, SSM_G * SSM_P), F32)
    y, hr, hi = pl.pallas_call(
        functools.partial(_s5_body, chunk=chunk),
        grid=(nb, NB, nk),
        in_specs=[pl.BlockSpec((chunk, UL), lambda b, n, k: (b * nk + k, n)),
                  pl.BlockSpec((None, UL, 2 * SL), lambda b, n, k: (n, 0, 0)),
                  pl.BlockSpec((None, 2 * SL, UL), lambda b, n, k: (n, 0, 0)),
                  pl.BlockSpec((None, 1, UL), lambda b, n, k: (n, 0, 0)),
                  pl.BlockSpec((None, 8, 8, SL), lambda b, n, k: (n, 0, 0, 0)),
                  pl.BlockSpec((None, 1, SL), lambda b, n, k: (b, 0, n)),
                  pl.BlockSpec((None, 1, SL), lambda b, n, k: (b, 0, n))],
        out_specs=[pl.BlockSpec((chunk, UL), lambda b, n, k: (b * nk + k, n)),
                   pl.BlockSpec((None, 1, SL), lambda b, n, k: (b, 0, n)),
                   pl.BlockSpec((None, 1, SL), lambda b, n, k: (b, 0, n))],
        out_shape=[jax.ShapeDtypeStruct((nb * t, SSM_W), F32), st, st],
        scratch_shapes=[pltpu.VMEM((chunk, 2 * SL), F32), pltpu.VMEM((1, SL), F32),
                        pltpu.VMEM((1, SL), F32)],
        compiler_params=_cp("parallel", "parallel", "arbitrary"),
        name="s5_prompt",
    )(u, bblk, cblk, dvec, tab, h0r.reshape(nb, 1, -1), h0i.reshape(nb, 1, -1))
    return y, hr.reshape(nb, SSM_G, SSM_P), hi.reshape(nb, SSM_G, SSM_P)


def _s5_step_body(u_ref, bblk_ref, cblk_ref, d_ref, lam_ref, h0r_ref, h0i_ref, y_ref, hr_ref, hi_ref):
    u = u_ref[...]
    bu = jnp.dot(u.astype(BF16), bblk_ref[...], preferred_element_type=F32)
    lr, li = lam_ref[0:1, :], lam_ref[1:2, :]
    h0r, h0i = h0r_ref[...], h0i_ref[...]
    hr = bu[:, 0:SL] + (lr * h0r - li * h0i)
    hi = bu[:, SL:2 * SL] + (lr * h0i + li * h0r)
    hr_ref[...] = hr
    hi_ref[...] = hi
    hcat = jnp.concatenate([hr, hi], axis=1).astype(BF16)
    y_ref[...] = jnp.dot(hcat, cblk_ref[...], preferred_element_type=F32) + d_ref[...] * u


def _s5_sample(u, bblk, cblk, dvec, lam1, h0r, h0i):
    rows = u.shape[0]
    st = jax.ShapeDtypeStruct((rows, SSM_G * SSM_P), F32)
    return pl.pallas_call(
        _s5_step_body,
        grid=(NB,),
        in_specs=[pl.BlockSpec((rows, UL), lambda n: (0, n)),
                  pl.BlockSpec((None, UL, 2 * SL), lambda n: (n, 0, 0)),
                  pl.BlockSpec((None, 2 * SL, UL), lambda n: (n, 0, 0)),
                  pl.BlockSpec((None, 1, UL), lambda n: (n, 0, 0)),
                  pl.BlockSpec((None, 2, SL), lambda n: (n, 0, 0)),
                  pl.BlockSpec((rows, SL), lambda n: (0, n)),
                  pl.BlockSpec((rows, SL), lambda n: (0, n))],
        out_specs=[pl.BlockSpec((rows, UL), lambda n: (0, n)),
                   pl.BlockSpec((rows, SL), lambda n: (0, n)),
                   pl.BlockSpec((rows, SL), lambda n: (0, n))],
        out_shape=[jax.ShapeDtypeStruct((rows, SSM_W), F32), st, st],
        compiler_params=_cp("parallel"),
        name="s5_sample",
    )(u, bblk, cblk, dvec, lam1, h0r, h0i)


def _outproj_body(x_ref, a_ref, y_ref, wg_ref, bg_ref, wo_ref, g_ref, x2_ref, h2_ref):
    y = jax.nn.gelu(y_ref[...])
    gate = jnp.dot(y.astype(BF16), wg_ref[...], preferred_element_type=F32) + bg_ref[...]
    y = y * jax.nn.sigmoid(gate)
    mix = (jnp.dot(a_ref[...].astype(BF16), wo_ref[0:AW, :], preferred_element_type=F32)
           + jnp.dot(y.astype(BF16), wo_ref[AW:, :], preferred_element_type=F32))
    x2 = x_ref[...] + mix
    x2_ref[...] = x2
    ms = jnp.mean(x2 * x2, axis=-1, keepdims=True)
    h2_ref[...] = (x2 * lax.rsqrt(ms + EPS) * g_ref[...]).astype(BF16)


def _outproj(x, attn, y, wg_bf, bg, wo_bf, g, tm):
    m = x.shape[0]
    row = lambda i: (i, 0)
    fix = lambda i: (0, 0)
    return pl.pallas_call(
        _outproj_body,
        grid=(m // tm,),
        in_specs=[pl.BlockSpec((tm, D_MODEL), row), pl.BlockSpec((tm, AW), row),
                  pl.BlockSpec((tm, SSM_W), row), pl.BlockSpec((SSM_W, SSM_W), fix),
                  pl.BlockSpec((1, SSM_W), fix), pl.BlockSpec((D_MODEL, D_MODEL), fix),
                  pl.BlockSpec((1, D_MODEL), fix)],
        out_specs=[pl.BlockSpec((tm, D_MODEL), row), pl.BlockSpec((tm, D_MODEL), row)],
        out_shape=[jax.ShapeDtypeStruct((m, D_MODEL), F32), jax.ShapeDtypeStruct((m, D_MODEL), BF16)],
        compiler_params=_cp("parallel"),
        name="outproj",
    )(x, attn, y, wg_bf, bg, wo_bf, g)


def _topk_body(h_ref, wq_ref, keys_ref, s1_ref, w1_ref, s2_ref, e2_ref, tau_ref, q_scr, cand_scr, *, tb):
    q_scr[...] = jnp.dot(wq_ref[...], h_ref[...], preferred_element_type=F32).astype(BF16)
    ninf = -jnp.inf

    def top16(s):
        vals = []
        w = s
        for _ in range(P_TOPK):
            mx = jnp.max(w, axis=0, keepdims=True)
            vals.append(mx)
            w = jnp.where(w == mx, ninf, w)
        return vals

    hrow = lax.broadcasted_iota(jnp.int32, (P_HEADS, 128), 0)

    def head(h, taus):
        taus = list(taus)
        for lc in range(tb // 128):
            ls = slice(lc * 128, (lc + 1) * 128)
            sc = []
            for c in range(2):
                hc = 2 * h + c
                qrows = pl.ds(pl.multiple_of(hc * P_HALF, P_HALF), P_HALF)
                sc.append(jnp.dot(keys_ref[hc], q_scr[qrows, ls],
                                  preferred_element_type=F32))
            v1 = top16(sc[0])
            v2 = top16(sc[1])
            cand_scr[...] = jnp.full(cand_scr.shape, ninf, F32)
            for n, (i, j) in enumerate(STAIR):
                cand_scr[n:n + 1, :] = v1[i] + v2[j]
            w = cand_scr[...]
            top = v1[0] + v2[0]
            z = jnp.zeros_like(top)
            mx = top
            for _ in range(P_TOPK):
                mx = jnp.max(w, axis=0, keepdims=True)
                z = z + jnp.exp(mx - top)
                w = jnp.where(w == mx, ninf, w)
            rows = pl.ds(pl.multiple_of(h * P_NK, P_NK), P_NK)
            taus[lc] = jnp.where(hrow == h, mx, taus[lc])
            s1_ref[rows, ls] = sc[0]
            s2_ref[rows, ls] = sc[1]
            w1_ref[rows, ls] = jnp.exp(sc[0] - v1[0]) / z
            e2_ref[rows, ls] = jnp.exp(sc[1] - v2[0])
        return tuple(taus)

    taus = lax.fori_loop(0, P_HEADS, head,
                         tuple(jnp.zeros((P_HEADS, 128), F32) for _ in range(tb // 128)))
    for lc in range(tb // 128):
        tau_ref[:, lc * 128:(lc + 1) * 128] = taus[lc]


def _peer_topk(h2t, wqt_bf, keys_bf, tb):
    m = h2t.shape[1]
    big = jax.ShapeDtypeStruct((P_HEADS * P_NK, m), F32)
    col = lambda i: (0, i)
    return pl.pallas_call(
        functools.partial(_topk_body, tb=tb),
        grid=(m // tb,),
        in_specs=[pl.BlockSpec((D_MODEL, tb), col),
                  pl.BlockSpec((P_HEADS * 2 * P_HALF, D_MODEL), lambda i: (0, 0)),
                  pl.BlockSpec((P_HEADS * 2, P_NK, P_HALF), lambda i: (0, 0, 0))],
        out_specs=[pl.BlockSpec((P_HEADS * P_NK, tb), col)] * 4 + [pl.BlockSpec((P_HEADS, tb), col)],
        out_shape=[big, big, big, big, jax.ShapeDtypeStruct((P_HEADS, m), F32)],
        scratch_shapes=[pltpu.VMEM((P_HEADS * 2 * P_HALF, tb), BF16), pltpu.VMEM((STAIR_ROWS, 128), F32)],
        compiler_params=_cp("parallel"),
        name="peer_topk",
    )(h2t, wqt_bf, keys_bf)


def _peer_body(h_ref, u_ref, vt_ref, s1_ref, w1_ref, s2_ref, e2_ref, tau_ref, o_ref,
               a_scr, wm_scr, *, tb, na):
    step = pl.program_id(1)

    @pl.when(step == 0)
    def _():
        o_ref[...] = jnp.zeros(o_ref.shape, F32)

    a_scr[...] = jnp.dot(u_ref[...], h_ref[...], preferred_element_type=F32)
    for al in range(na):
        for lc in range(tb // 128):
            ls = slice(lc * 128, (lc + 1) * 128)
            g = jnp.zeros((P_NK, 128), F32)
            for h in range(P_HEADS):
                s1 = s1_ref[al, h:h + 1, ls]
                w1 = w1_ref[al, h:h + 1, ls]
                rows = slice(h * P_NK, (h + 1) * P_NK)
                sel = (s1 + s2_ref[rows, ls]) >= tau_ref[h:h + 1, ls]
                g = g + jnp.where(sel, e2_ref[rows, ls] * w1, 0.0)
            rows = slice(al * P_NK, (al + 1) * P_NK)
            act = jax.nn.gelu(a_scr[rows, ls])
            wm_scr[rows, ls] = (act * g).astype(BF16)
    o_ref[...] += jnp.dot(vt_ref[...], wm_scr[...], preferred_element_type=F32)


def _peer_dense(h2t, u_bf, vt_bf, s1, w1, s2, e2, tau, tb, na):
    m = h2t.shape[1]
    col = lambda i, k: (0, i)
    return pl.pallas_call(
        functools.partial(_peer_body, tb=tb, na=na),
        grid=(m // tb, P_NK // na),
        in_specs=[pl.BlockSpec((D_MODEL, tb), col),
                  pl.BlockSpec((na * P_NK, D_MODEL), lambda i, k: (k, 0)),
                  pl.BlockSpec((D_MODEL, na * P_NK), lambda i, k: (0, k)),
                  pl.BlockSpec((na, P_HEADS, tb), lambda i, k: (k, 0, i)),
                  pl.BlockSpec((na, P_HEADS, tb), lambda i, k: (k, 0, i)),
                  pl.BlockSpec((P_HEADS * P_NK, tb), col),
                  pl.BlockSpec((P_HEADS * P_NK, tb), col),
                  pl.BlockSpec((P_HEADS, tb), col)],
        out_specs=pl.BlockSpec((D_MODEL, tb), col),
        out_shape=jax.ShapeDtypeStruct((D_MODEL, m), F32),
        scratch_shapes=[pltpu.VMEM((na * P_NK, tb), F32), pltpu.VMEM((na * P_NK, tb), BF16)],
        compiler_params=_cp("parallel", "arbitrary"),
        name="peer_dense",
    )(h2t, u_bf, vt_bf, s1, w1, s2, e2, tau)


def _final_body(x_ref, p_ref, g_ref, y_ref):
    x = x_ref[...] + p_ref[...]
    ms = jnp.mean(x * x, axis=-1, keepdims=True)
    y_ref[...] = x * lax.rsqrt(ms + EPS) * g_ref[...]


def _final(x2, p, g, tm):
    m = x2.shape[0]
    row = lambda i: (i, 0)
    return pl.pallas_call(
        _final_body,
        grid=(m // tm,),
        in_specs=[pl.BlockSpec((tm, D_MODEL), row), pl.BlockSpec((tm, D_MODEL), row),
                  pl.BlockSpec((1, D_MODEL), lambda i: (0, 0))],
        out_specs=pl.BlockSpec((tm, D_MODEL), row),
        out_shape=jax.ShapeDtypeStruct((m, D_MODEL), F32),
        compiler_params=_cp("parallel"),
        name="final_norm",
    )(x2, p, g)


def _rope_tables(pos):
    half = QK // 2
    inv = 1.0 / (ROPE_THETA ** (jnp.arange(half, dtype=F32) * 2.0 / QK))
    ang = pos.astype(F32)[:, None] * inv[None, :]
    c, s = jnp.cos(ang), jnp.sin(ang)
    cos = jnp.concatenate([c, c, c, c], axis=1)
    sin = jnp.concatenate([-s, s, -s, s], axis=1)
    return cos, sin


def _peer(h2_bf, wqt_bf, keys_bf, u_bf, vt_bf, tb, na):
    h2t = h2_bf.T
    s1, w1, s2, e2, tau = _peer_topk(h2t, wqt_bf, keys_bf, min(tb, 256))
    amajor = lambda a: a.reshape(P_HEADS, P_NK, -1).transpose(1, 0, 2)
    return _peer_dense(h2t, u_bf, vt_bf, amajor(s1), amajor(w1), s2, e2, tau, tb, na).T


def kernel(x_prompt, x_sample, cache_k, cache_v, state_ssm_re, state_ssm_im, page_table, g_mix, w_in, lambda_q1, lambda_k1, lambda_q2, lambda_k2, g_sub, ssm_a_re, ssm_a_im, ssm_log_dt, ssm_b_re, ssm_b_im, ssm_c_re, ssm_c_im, ssm_d, w_glu, b_glu, w_out, g_ffn, peer_w_q, peer_sub_keys, peer_u, peer_v, g_final):
    bp, tp, _ = x_prompt.shape
    bs, ts, _ = x_sample.shape
    assert ts == 1 and g_mix.shape[0] == 1
    n_pages = page_table.shape[1]
    past_len = n_pages * PAGE
    mp = bp * tp
    srows = 16

    w_in_bf = w_in[0].astype(BF16)
    wg_bf = w_glu[0].astype(BF16)
    wo_bf = w_out[0].astype(BF16)
    wqt_bf = peer_w_q[0].T.astype(BF16)
    keys_bf = peer_sub_keys[0].reshape(P_HEADS * 2, P_NK, P_HALF).astype(BF16)
    u_bf = peer_u[0].astype(BF16)
    vt_bf = peer_v[0].T.astype(BF16)
    gmix = g_mix[0].reshape(1, -1)
    gffn = g_ffn[0].reshape(1, -1)
    gfin = g_final.reshape(1, -1)
    gsub = g_sub[0].reshape(1, -1)
    bg = b_glu[0].reshape(1, -1)
    lam4 = jnp.stack([lambda_q1[0], lambda_k1[0], lambda_q2[0], lambda_k2[0]], axis=0)
    pw, bbr, bbi = _s5_prep(ssm_a_re[0], ssm_a_im[0], ssm_log_dt[0], ssm_b_re[0], ssm_b_im[0])
    bblk, cblk, tab, lam1, dvec = _s5_tables(pw, bbr, bbi, ssm_c_re[0], ssm_c_im[0], ssm_d[0])

    xp = x_prompt.reshape(mp, D_MODEL)
    cos_p, sin_p = _rope_tables(jnp.tile(jnp.arange(tp), bp))
    qb, kp, kpb, vp, vpb, up = _inproj(xp, gmix, w_in_bf, cos_p, sin_p, 512)
    attn_p = _attn_prompt(lam4, qb, kpb, vpb, gsub, bp, tp, 256)
    zeros = jnp.zeros((bp, SSM_G * SSM_P), F32)
    yp, rp, ip = _s5_prompt(up, bblk, cblk, dvec, tab, zeros, zeros, bp, tp, 512)
    x2p, h2p = _outproj(xp, attn_p, yp, wg_bf, bg, wo_bf, gffn, 256)
    peer_p = _peer(h2p, wqt_bf, keys_bf, u_bf, vt_bf, 512, 4)
    y_prompt = _final(x2p, peer_p, gfin, 512).reshape(bp, tp, D_MODEL)

    xs = jnp.pad(x_sample.reshape(bs, D_MODEL), ((0, srows - bs), (0, 0)))
    cos_s, sin_s = _rope_tables(jnp.full((srows,), past_len))
    qs, ks, _, vs, _, us = _inproj(xs, gmix, w_in_bf, cos_s, sin_s, srows)
    seg = (jnp.arange(AW)[None, :] // QK == jnp.arange(2 * HEADS)[:, None])
    q16 = jnp.where(seg[None], qs[:bs, None, :], jnp.zeros((), BF16))
    ck = cache_k[0].reshape(-1, PAGE, AW)
    cv = cache_v[0].reshape(-1, PAGE, AW)
    attn_s = _attn_sample(page_table, lam4, q16, ck, cv, ks[:bs, None, :], vs[:bs, None, :], gsub)
    pad_s = lambda a: jnp.pad(a.reshape(bs, -1), ((0, srows - bs), (0, 0)))
    ys, rs, is_ = _s5_sample(us, bblk, cblk, dvec, lam1, pad_s(state_ssm_re[0]), pad_s(state_ssm_im[0]))
    x2s, h2s = _outproj(xs, pad_s(attn_s), ys, wg_bf, bg, wo_bf, gffn, srows)
    h2s_rep = jnp.tile(h2s[:bs], (128 // bs, 1))
    peer_s = _peer(h2s_rep, wqt_bf, keys_bf, u_bf, vt_bf, 128, 4)[:srows]
    y_sample = _final(x2s, peer_s, gfin, srows)[:bs].reshape(bs, ts, D_MODEL)

    return (y_prompt, y_sample,
            kp.reshape(1, bp, tp, HEADS, 2 * QK), vp.reshape(1, bp, tp, HEADS, VD),
            rp[None], ip[None],
            ks[:bs].reshape(1, bs, ts, HEADS, 2 * QK), vs[:bs].reshape(1, bs, ts, HEADS, VD),
            rs[:bs].reshape(1, bs, SSM_G, SSM_P), is_[:bs].reshape(1, bs, SSM_G, SSM_P))
```

```python
import functools
import math

import jax
import jax.numpy as jnp
from jax import lax
from jax.experimental import pallas as pl
from jax.experimental.pallas import tpu as pltpu

F32 = jnp.float32
BF16 = jnp.bfloat16

D_MODEL = 2048
PAGE = 128
HEADS = 8
QK = 64
VD = 128
AW = HEADS * VD
SSM_W = D_MODEL - AW
SSM_CH = 16
SSM_G = SSM_W // SSM_CH
SSM_P = 64
GB = 16
NB = SSM_G // GB
SL = GB * SSM_P
UL = GB * SSM_CH
P_HEADS = 8
P_TOPK = 16
P_NK = 128
P_HALF = 128
P_E = P_NK * P_NK
EPS = 1e-6
NEG = -1e30
ROPE_THETA = 10000.0
LAM_INIT = 0.8 - 0.6 * math.exp(-0.3 * 0)
VMEM_LIMIT = 56 * 1024 * 1024

STAIR = [(i, j) for i in range(P_TOPK) for j in range(P_TOPK) if (i + 1) * (j + 1) <= P_TOPK]
STAIR_ROWS = -(-len(STAIR) // 8) * 8


def _cp(*sem):
    return pltpu.CompilerParams(dimension_semantics=sem, vmem_limit_bytes=VMEM_LIMIT)


def _inproj_body(x_ref, g_ref, w_ref, cos_ref, sin_ref,
                 q_ref, k_ref, kb_ref, v_ref, vb_ref, u_ref, h_scr):
    j = pl.program_id(1)

    @pl.when(j == 0)
    def _():
        x = x_ref[...]
        ms = jnp.mean(x * x, axis=-1, keepdims=True)
        h_scr[...] = (x * lax.rsqrt(ms + EPS) * g_ref[...]).astype(BF16)

    z = jnp.dot(h_scr[...], w_ref[...], preferred_element_type=F32)

    def rope(zc):
        lane = lax.broadcasted_iota(jnp.int32, zc.shape, 1)
        first = (lane & (QK - 1)) < (QK // 2)
        partner = jnp.where(first, pltpu.roll(zc, 128 - QK // 2, 1), pltpu.roll(zc, QK // 2, 1))
        return zc * cos_ref[...] + partner * sin_ref[...]

    @pl.when(j == 0)
    def _():
        for c in range(AW // 128):
            sl = slice(c * 128, (c + 1) * 128)
            q_ref[:, sl] = (rope(z[:, sl]) * (QK ** -0.5)).astype(BF16)

    @pl.when(j == 1)
    def _():
        for c in range(AW // 128):
            sl = slice(c * 128, (c + 1) * 128)
            kc = rope(z[:, sl])
            k_ref[:, sl] = kc
            kb_ref[:, sl] = kc.astype(BF16)

    @pl.when(j == 2)
    def _():
        v_ref[...] = z
        vb_ref[...] = z.astype(BF16)

    @pl.when(j == 3)
    def _():
        u_ref[...] = z


def _inproj(x, g, w_bf, cos, sin, tm):
    m = x.shape[0]
    row = lambda i, j: (i, 0)
    outs = [jax.ShapeDtypeStruct((m, AW), BF16), jax.ShapeDtypeStruct((m, AW), F32),
            jax.ShapeDtypeStruct((m, AW), BF16), jax.ShapeDtypeStruct((m, AW), F32),
            jax.ShapeDtypeStruct((m, AW), BF16), jax.ShapeDtypeStruct((m, SSM_W), F32)]
    return pl.pallas_call(
        _inproj_body,
        grid=(m // tm, 4),
        in_specs=[pl.BlockSpec((tm, D_MODEL), row),
                  pl.BlockSpec((1, D_MODEL), lambda i, j: (0, 0)),
                  pl.BlockSpec((D_MODEL, AW), lambda i, j: (0, j)),
                  pl.BlockSpec((tm, 128), row),
                  pl.BlockSpec((tm, 128), row)],
        out_specs=[pl.BlockSpec((tm, AW), row) for _ in outs],
        out_shape=outs,
        scratch_shapes=[pltpu.VMEM((tm, D_MODEL), BF16)],
        compiler_params=_cp("parallel", "arbitrary"),
        name="inproj",
    )(x, g, w_bf, cos, sin)


def _lam(lam4_ref):
    l4 = lam4_ref[...]
    a = jnp.exp(jnp.sum(l4[0:1, :] * l4[1:2, :], axis=-1, keepdims=True))
    b = jnp.exp(jnp.sum(l4[2:3, :] * l4[3:4, :], axis=-1, keepdims=True))
    return a - b + LAM_INIT


def _subnorm(o, gsub):
    o = o * lax.rsqrt(jnp.mean(o * o, axis=-1, keepdims=True) + EPS)
    return o * gsub * (1.0 - LAM_INIT)


def _attn_body(lam4_ref, qt_ref, k_ref, vt_ref, gsub_ref, o_ref, acc_scr, *, tq):
    qi = pl.program_id(2)
    qt = qt_ref[...]
    row = lax.broadcasted_iota(jnp.int32, qt.shape, 0)
    zero = jnp.zeros_like(qt)
    qc = (jnp.where(row < QK, qt, zero), jnp.where(row >= QK, qt, zero))
    acc_scr[...] = jnp.zeros(acc_scr.shape, F32)

    def step(kv, carry, masked):
        ms, ls = carry
        kb = k_ref[pl.ds(pl.multiple_of(kv * tq, tq), tq), :]
        vtb = vt_ref[kv]
        new_m, new_l = [], []
        for c in range(2):
            st = jnp.dot(kb, qc[c], preferred_element_type=F32)
            if masked:
                r = lax.broadcasted_iota(jnp.int32, st.shape, 0)
                q_ = lax.broadcasted_iota(jnp.int32, st.shape, 1)
                st = jnp.where(r <= q_, st, NEG)
            m_new = jnp.maximum(ms[c], jnp.max(st, axis=0, keepdims=True))
            alpha = jnp.exp(ms[c] - m_new)
            p = jnp.exp(st - m_new)
            new_l.append(alpha * ls[c] + jnp.sum(p, axis=0, keepdims=True))
            acc_scr[c] = alpha * acc_scr[c] + jnp.dot(vtb, p.astype(BF16), preferred_element_type=F32)
            new_m.append(m_new)
        return tuple(new_m), tuple(new_l)

    m0 = jnp.full((1, tq), NEG, F32)
    l0 = jnp.zeros((1, tq), F32)
    carry = lax.fori_loop(0, qi, lambda kv, c: step(kv, c, False), ((m0, m0), (l0, l0)))
    _, ls = step(qi, carry, True)

    o = acc_scr[0] / ls[0] - _lam(lam4_ref) * (acc_scr[1] / ls[1])
    o = o * lax.rsqrt(jnp.mean(o * o, axis=0, keepdims=True) + EPS)
    o_ref[...] = (o * gsub_ref[...] * (1.0 - LAM_INIT)).T


def _attn_prompt(lam4, qb, kb, vb, gsub, nb, t, tq):
    qt = qb.reshape(nb, t, AW).transpose(0, 2, 1)
    kb = kb.reshape(nb, t, AW)
    vt = vb.reshape(nb, t // tq, tq, AW).transpose(0, 1, 3, 2)
    gcol = jnp.broadcast_to(gsub.reshape(VD, 1), (VD, tq))
    out = pl.pallas_call(
        functools.partial(_attn_body, tq=tq),
        grid=(nb, HEADS, t // tq),
        in_specs=[pl.BlockSpec((4, QK), lambda b, h, i: (0, 0)),
                  pl.BlockSpec((None, VD, tq), lambda b, h, i: (b, h, i)),
                  pl.BlockSpec((None, t, VD), lambda b, h, i: (b, 0, h)),
                  pl.BlockSpec((None, t // tq, VD, tq), lambda b, h, i: (b, 0, h, 0)),
                  pl.BlockSpec((VD, tq), lambda b, h, i: (0, 0))],
        out_specs=pl.BlockSpec((None, tq, VD), lambda b, h, i: (b, i, h)),
        out_shape=jax.ShapeDtypeStruct((nb, t, AW), F32),
        scratch_shapes=[pltpu.VMEM((2, VD, tq), F32)],
        compiler_params=_cp("parallel", "parallel", "arbitrary"),
        name="attn_prompt",
    )(lam4, qt, kb, vt, gcol)
    return out.reshape(nb * t, AW)


def _sattn_body(pt_ref, lam4_ref, q_ref, *refs, gp):
    k_refs, v_refs = refs[:gp], refs[gp:2 * gp]
    kn_ref, vn_ref, gsub_ref, o_ref, m_scr, l_scr, acc_scr = refs[2 * gp:]
    j = pl.program_id(1)
    q = q_ref[...]
    first = lax.broadcasted_iota(jnp.int32, (HEADS, VD), 1) < QK

    @pl.when(j == 0)
    def _():
        m_scr[...] = jnp.full(m_scr.shape, NEG, F32)
        l_scr[...] = jnp.zeros(l_scr.shape, F32)
        acc_scr[...] = jnp.zeros(acc_scr.shape, F32)

    def update(kp, vp):
        prod = kp * q[None]
        for c in range(2):
            half = first if c == 0 else jnp.logical_not(first)
            s = jnp.sum(jnp.where(half[None], prod, 0.0), axis=-1, keepdims=True)
            m_prev = m_scr[c]
            m_new = jnp.maximum(m_prev, jnp.max(s, axis=0))
            alpha = jnp.exp(m_prev - m_new)
            p = jnp.exp(s - m_new[None])
            l_scr[c] = alpha * l_scr[c] + jnp.sum(p, axis=0)
            acc_scr[c] = alpha * acc_scr[c] + jnp.sum(p * vp, axis=0)
            m_scr[c] = m_new

    for g in range(gp):
        update(k_refs[g][...], v_refs[g][...])

    @pl.when(j == pl.num_programs(1) - 1)
    def _():
        update(kn_ref[...], vn_ref[...])
        o = acc_scr[0] / l_scr[0] - _lam(lam4_ref) * (acc_scr[1] / l_scr[1])
        o_ref[...] = _subnorm(o, gsub_ref[...])


def _attn_sample(page_table, lam4, q, cache_k, cache_v, kn, vn, gsub, gp):
    bs, n_pages = page_table.shape
    pt = page_table.reshape(-1).astype(jnp.int32)

    def page(g):
        return pl.BlockSpec((None, None, PAGE, HEADS, VD),
                            lambda b, j, pt_ref: (0, pt_ref[b * n_pages + j * gp + g], 0, 0, 0))

    per_b = lambda b, j, p: (b, 0, 0, 0)
    grid_spec = pltpu.PrefetchScalarGridSpec(
        num_scalar_prefetch=1,
        grid=(bs, n_pages // gp),
        in_specs=[pl.BlockSpec((4, QK), lambda b, j, p: (0, 0)),
                  pl.BlockSpec((None, HEADS, VD), lambda b, j, p: (b, 0, 0))]
                 + [page(g) for g in range(gp)] + [page(g) for g in range(gp)]
                 + [pl.BlockSpec((None, 1, HEADS, VD), per_b),
                    pl.BlockSpec((None, 1, HEADS, VD), per_b),
                    pl.BlockSpec((1, VD), lambda b, j, p: (0, 0))],
        out_specs=pl.BlockSpec((None, HEADS, VD), lambda b, j, p: (b, 0, 0)),
        scratch_shapes=[pltpu.VMEM((2, HEADS, VD), F32), pltpu.VMEM((2, HEADS, VD), F32),
                        pltpu.VMEM((2, HEADS, VD), F32)],
    )
    out = pl.pallas_call(
        functools.partial(_sattn_body, gp=gp),
        grid_spec=grid_spec,
        out_shape=jax.ShapeDtypeStruct((bs, HEADS, VD), F32),
        compiler_params=_cp("parallel", "arbitrary"),
        name="attn_sample",
    )(pt, lam4, q, *([cache_k] * gp), *([cache_v] * gp), kn, vn, gsub)
    return out.reshape(bs, AW)


def _cmul(ar, ai, br, bi):
    return ar * br - ai * bi, ar * bi + ai * br


def _s5_prep_body(are_ref, aim_ref, ldt_ref, arer_ref, aimr_ref, ldtr_ref, bre_ref, bim_ref,
                  pow_ref, bbr_ref, bbi_ref):
    def disc(are, aim, ldt):
        dt = jnp.exp(ldt)
        mag = jnp.exp(are * dt)
        lr = mag * jnp.cos(aim * dt)
        li = mag * jnp.sin(aim * dt)
        return lr, li

    lr, li = disc(are_ref[...], aim_ref[...], ldt_ref[...])
    pr, pi = lr, li
    for n in range(8):
        pow_ref[2 * n] = pr
        pow_ref[2 * n + 1] = pi
        pr, pi = _cmul(pr, pi, lr, li)

    are, aim = arer_ref[...], aimr_ref[...]
    lrr, lir = disc(are, aim, ldtr_ref[...])
    den = are * are + aim * aim
    zr = lrr - 1.0
    fr = (zr * are + lir * aim) / den
    fi = (lir * are - zr * aim) / den
    br, bi = bre_ref[...], bim_ref[...]
    bbr_ref[...] = fr * br - fi * bi
    bbi_ref[...] = fr * bi + fi * br


def _s5_prep(a_re, a_im, log_dt, b_re, b_im):
    rep = lambda a: jnp.repeat(a, SSM_CH, axis=1)
    ldt = jnp.broadcast_to(log_dt[:, None], (SSM_G, SSM_P))
    pw, bbr, bbi = pl.pallas_call(
        _s5_prep_body,
        out_shape=[jax.ShapeDtypeStruct((16, SSM_G, SSM_P), F32),
                   jax.ShapeDtypeStruct((SSM_G, SSM_P * SSM_CH), F32),
                   jax.ShapeDtypeStruct((SSM_G, SSM_P * SSM_CH), F32)],
        name="s5_prep",
    )(a_re, a_im, ldt, rep(a_re), rep(a_im), rep(ldt),
      b_re.reshape(SSM_G, -1), b_im.reshape(SSM_G, -1))
    return pw, bbr.reshape(SSM_G, SSM_P, SSM_CH), bbi.reshape(SSM_G, SSM_P, SSM_CH)


def _s5_tables(pw, bbr, bbi, c_re, c_im, d):
    eye = jnp.eye(GB, dtype=F32)
    def blk_in(bb):
        x = bb.reshape(NB, GB, SSM_P, SSM_CH)
        return jnp.einsum('ngpc,gh->ngchp', x, eye).reshape(NB, UL, SL)
    bblk = jnp.concatenate([blk_in(bbr), blk_in(bbi)], axis=2).astype(BF16)
    def blk_out(cc):
        x = cc.reshape(NB, GB, SSM_CH, SSM_P)
        return jnp.einsum('ngcp,gh->ngphc', x, eye).reshape(NB, SL, UL)
    cblk = jnp.concatenate([blk_out(c_re), blk_out(-c_im)], axis=1).astype(BF16)
    lanes = lambda a: a.reshape(NB, 1, SL)
    t_idx = jnp.arange(8)[None, :, None]
    tabs = []
    for n in (1, 2, 4):
        keep = (t_idx >= n).astype(F32)
        tabs += [lanes(pw[2 * (n - 1)]) * keep, lanes(pw[2 * (n - 1) + 1]) * keep]
    pr = jnp.stack([pw[2 * n] for n in range(8)], axis=0).reshape(8, NB, SL).transpose(1, 0, 2)
    pi = jnp.stack([pw[2 * n + 1] for n in range(8)], axis=0).reshape(8, NB, SL).transpose(1, 0, 2)
    tabs += [pr, pi]
    tab = jnp.stack(tabs, axis=1)
    lam1 = jnp.stack([lanes(pw[0]), lanes(pw[1])], axis=1).reshape(NB, 2, SL)
    dvec = d.reshape(NB, 1, UL)
    return bblk, cblk, tab, lam1, dvec


def _s5_body(u_ref, bblk_ref, cblk_ref, d_ref, tab_ref, h0r_ref, h0i_ref,
             y_ref, hr_ref, hi_ref, st_scr, cr_scr, ci_scr, *, chunk):
    k = pl.program_id(2)

    @pl.when(k == 0)
    def _():
        cr_scr[...] = h0r_ref[...]
        ci_scr[...] = h0i_ref[...]

    u = u_ref[...]
    st_scr[...] = jnp.dot(u.astype(BF16), bblk_ref[...], preferred_element_type=F32)

    def tile(r, carry):
        cr, ci = carry
        rows = pl.ds(pl.multiple_of(r * 8, 8), 8)
        xr = st_scr[rows, 0:SL]
        xi = st_scr[rows, SL:2 * SL]
        for lvl, sh in enumerate((1, 2, 4)):
            ar, ai = tab_ref[2 * lvl], tab_ref[2 * lvl + 1]
            sr, si = pltpu.roll(xr, sh, 0), pltpu.roll(xi, sh, 0)
            xr, xi = xr + (ar * sr - ai * si), xi + (ar * si + ai * sr)
        pr, pi = tab_ref[6], tab_ref[7]
        hr = xr + (pr * cr - pi * ci)
        hi = xi + (pr * ci + pi * cr)
        st_scr[rows, 0:SL] = hr
        st_scr[rows, SL:2 * SL] = hi
        return hr[7:8, :], hi[7:8, :]

    cr, ci = lax.fori_loop(0, chunk // 8, tile, (cr_scr[...], ci_scr[...]))
    cr_scr[...] = cr
    ci_scr[...] = ci
    y_ref[...] = (jnp.dot(st_scr[...].astype(BF16), cblk_ref[...], preferred_element_type=F32)
                  + d_ref[...] * u)

    @pl.when(k == pl.num_programs(2) - 1)
    def _():
        hr_ref[...] = cr
        hi_ref[...] = ci


def _s5_prompt(u, bblk, cblk, dvec, tab, h0r, h0i, nb, t, chunk):
    nk = t // chunk
    st = jax.ShapeDtypeStruct((nb, 1, SSM_G * SSM_P), F32)
    y, hr, hi = pl.pallas_call(
        functools.partial(_s5_body, chunk=chunk),
        grid=(nb, NB, nk),
        in_specs=[pl.BlockSpec((chunk, UL), lambda b, n, k: (b * nk + k, n)),
                  pl.BlockSpec((None, UL, 2 * SL), lambda b, n, k: (n, 0, 0)),
                  pl.BlockSpec((None, 2 * SL, UL), lambda b, n, k: (n, 0, 0)),
                  pl.BlockSpec((None, 1, UL), lambda b, n, k: (n, 0, 0)),
                  pl.BlockSpec((None, 8, 8, SL), lambda b, n, k: (n, 0, 0, 0)),
                  pl.BlockSpec((None, 1, SL), lambda b, n, k: (b, 0, n)),
                  pl.BlockSpec((None, 1, SL), lambda b, n, k: (b, 0, n))],
        out_specs=[pl.BlockSpec((chunk, UL), lambda b, n, k: (b * nk + k, n)),
                   pl.BlockSpec((None, 1, SL), lambda b, n, k: (b, 0, n)),
                   pl.BlockSpec((None, 1, SL), lambda b, n, k: (b, 0, n))],
        out_shape=[jax.ShapeDtypeStruct((nb * t, SSM_W), F32), st, st],
        scratch_shapes=[pltpu.VMEM((chunk, 2 * SL), F32), pltpu.VMEM((1, SL), F32),
                        pltpu.VMEM((1, SL), F32)],
        compiler_params=_cp("parallel", "parallel", "arbitrary"),
        name="s5_prompt",
    )(u, bblk, cblk, dvec, tab, h0r.reshape(nb, 1, -1), h0i.reshape(nb, 1, -1))
    return y, hr.reshape(nb, SSM_G, SSM_P), hi.reshape(nb, SSM_G, SSM_P)


def _s5_step_body(u_ref, bblk_ref, cblk_ref, d_ref, lam_ref, h0r_ref, h0i_ref, y_ref, hr_ref, hi_ref):
    u = u_ref[...]
    bu = jnp.dot(u.astype(BF16), bblk_ref[...], preferred_element_type=F32)
    lr, li = lam_ref[0:1, :], lam_ref[1:2, :]
    h0r, h0i = h0r_ref[...], h0i_ref[...]
    hr = bu[:, 0:SL] + (lr * h0r - li * h0i)
    hi = bu[:, SL:2 * SL] + (lr * h0i + li * h0r)
    hr_ref[...] = hr
    hi_ref[...] = hi
    hcat = jnp.concatenate([hr, hi], axis=1).astype(BF16)
    y_ref[...] = jnp.dot(hcat, cblk_ref[...], preferred_element_type=F32) + d_ref[...] * u


def _s5_sample(u, bblk, cblk, dvec, lam1, h0r, h0i):
    rows = u.shape[0]
    st = jax.ShapeDtypeStruct((rows, SSM_G * SSM_P), F32)
    return pl.pallas_call(
        _s5_step_body,
        grid=(NB,),
        in_specs=[pl.BlockSpec((rows, UL), lambda n: (0, n)),
                  pl.BlockSpec((None, UL, 2 * SL), lambda n: (n, 0, 0)),
                  pl.BlockSpec((None, 2 * SL, UL), lambda n: (n, 0, 0)),
                  pl.BlockSpec((None, 1, UL), lambda n: (n, 0, 0)),
                  pl.BlockSpec((None, 2, SL), lambda n: (n, 0, 0)),
                  pl.BlockSpec((rows, SL), lambda n: (0, n)),
                  pl.BlockSpec((rows, SL), lambda n: (0, n))],
        out_specs=[pl.BlockSpec((rows, UL), lambda n: (0, n)),
                   pl.BlockSpec((rows, SL), lambda n: (0, n)),
                   pl.BlockSpec((rows, SL), lambda n: (0, n))],
        out_shape=[jax.ShapeDtypeStruct((rows, SSM_W), F32), st, st],
        compiler_params=_cp("parallel"),
        name="s5_sample",
    )(u, bblk, cblk, dvec, lam1, h0r, h0i)


def _outproj_body(x_ref, a_ref, y_ref, wg_ref, bg_ref, wo_ref, g_ref, x2_ref, h2_ref):
    y = jax.nn.gelu(y_ref[...])
    gate = jnp.dot(y.astype(BF16), wg_ref[...], preferred_element_type=F32) + bg_ref[...]
    y = y * jax.nn.sigmoid(gate)
    mix = (jnp.dot(a_ref[...].astype(BF16), wo_ref[0:AW, :], preferred_element_type=F32)
           + jnp.dot(y.astype(BF16), wo_ref[AW:, :], preferred_element_type=F32))
    x2 = x_ref[...] + mix
    x2_ref[...] = x2
    ms = jnp.mean(x2 * x2, axis=-1, keepdims=True)
    h2_ref[...] = (x2 * lax.rsqrt(ms + EPS) * g_ref[...]).astype(BF16)


def _outproj(x, attn, y, wg_bf, bg, wo_bf, g, tm):
    m = x.shape[0]
    row = lambda i: (i, 0)
    fix = lambda i: (0, 0)
    return pl.pallas_call(
        _outproj_body,
        grid=(m // tm,),
        in_specs=[pl.BlockSpec((tm, D_MODEL), row), pl.BlockSpec((tm, AW), row),
                  pl.BlockSpec((tm, SSM_W), row), pl.BlockSpec((SSM_W, SSM_W), fix),
                  pl.BlockSpec((1, SSM_W), fix), pl.BlockSpec((D_MODEL, D_MODEL), fix),
                  pl.BlockSpec((1, D_MODEL), fix)],
        out_specs=[pl.BlockSpec((tm, D_MODEL), row), pl.BlockSpec((tm, D_MODEL), row)],
        out_shape=[jax.ShapeDtypeStruct((m, D_MODEL), F32), jax.ShapeDtypeStruct((m, D_MODEL), BF16)],
        compiler_params=_cp("parallel"),
        name="outproj",
    )(x, attn, y, wg_bf, bg, wo_bf, g)


def _topk_body(h_ref, wq_ref, keys_ref, s1_ref, w1_ref, s2_ref, e2_ref, tau_ref, q_scr, cand_scr, *, tb):
    q_scr[...] = jnp.dot(wq_ref[...], h_ref[...], preferred_element_type=F32).astype(BF16)
    ninf = -jnp.inf

    def top16(s):
        vals = []
        w = s
        for _ in range(P_TOPK):
            mx = jnp.max(w, axis=0, keepdims=True)
            vals.append(mx)
            w = jnp.where(w == mx, ninf, w)
        return vals

    hrow = lax.broadcasted_iota(jnp.int32, (P_HEADS, 128), 0)

    def head(h, taus):
        taus = list(taus)
        for lc in range(tb // 128):
            ls = slice(lc * 128, (lc + 1) * 128)
            sc = []
            for c in range(2):
                hc = 2 * h + c
                qrows = pl.ds(pl.multiple_of(hc * P_HALF, P_HALF), P_HALF)
                sc.append(jnp.dot(keys_ref[hc], q_scr[qrows, ls],
                                  preferred_element_type=F32))
            v1 = top16(sc[0])
            v2 = top16(sc[1])
            cand_scr[...] = jnp.full(cand_scr.shape, ninf, F32)
            for n, (i, j) in enumerate(STAIR):
                cand_scr[n:n + 1, :] = v1[i] + v2[j]
            w = cand_scr[...]
            top = v1[0] + v2[0]
            z = jnp.zeros_like(top)
            mx = top
            for _ in range(P_TOPK):
                mx = jnp.max(w, axis=0, keepdims=True)
                z = z + jnp.exp(mx - top)
                w = jnp.where(w == mx, ninf, w)
            rows = pl.ds(pl.multiple_of(h * P_NK, P_NK), P_NK)
            taus[lc] = jnp.where(hrow == h, mx, taus[lc])
            s1_ref[rows, ls] = sc[0]
            s2_ref[rows, ls] = sc[1]
            w1_ref[rows, ls] = jnp.exp(sc[0] - v1[0]) / z
            e2_ref[rows, ls] = jnp.exp(sc[1] - v2[0])
        return tuple(taus)

    taus = lax.fori_loop(0, P_HEADS, head,
                         tuple(jnp.zeros((P_HEADS, 128), F32) for _ in range(tb // 128)))
    for lc in range(tb // 128):
        tau_ref[:, lc * 128:(lc + 1) * 128] = taus[lc]


def _peer_topk(h2t, wqt_bf, keys_bf, tb):
    m = h2t.shape[1]
    big = jax.ShapeDtypeStruct((P_HEADS * P_NK, m), F32)
    col = lambda i: (0, i)
    return pl.pallas_call(
        functools.partial(_topk_body, tb=tb),
        grid=(m // tb,),
        in_specs=[pl.BlockSpec((D_MODEL, tb), col),
                  pl.BlockSpec((P_HEADS * 2 * P_HALF, D_MODEL), lambda i: (0, 0)),
                  pl.BlockSpec((P_HEADS * 2, P_NK, P_HALF), lambda i: (0, 0, 0))],
        out_specs=[pl.BlockSpec((P_HEADS * P_NK, tb), col)] * 4 + [pl.BlockSpec((P_HEADS, tb), col)],
        out_shape=[big, big, big, big, jax.ShapeDtypeStruct((P_HEADS, m), F32)],
        scratch_shapes=[pltpu.VMEM((P_HEADS * 2 * P_HALF, tb), BF16), pltpu.VMEM((STAIR_ROWS, 128), F32)],
        compiler_params=_cp("parallel"),
        name="peer_topk",
    )(h2t, wqt_bf, keys_bf)


def _peer_body(h_ref, u_ref, vt_ref, s1_ref, w1_ref, s2_ref, e2_ref, tau_ref, o_ref,
               a_scr, wm_scr, *, tb, na):
    step = pl.program_id(1)

    @pl.when(step == 0)
    def _():
        o_ref[...] = jnp.zeros(o_ref.shape, F32)

    a_scr[...] = jnp.dot(u_ref[...], h_ref[...], preferred_element_type=F32)
    for al in range(na):
        for lc in range(tb // 128):
            ls = slice(lc * 128, (lc + 1) * 128)
            g = jnp.zeros((P_NK, 128), F32)
            for h in range(P_HEADS):
                s1 = s1_ref[al, h:h + 1, ls]
                w1 = w1_ref[al, h:h + 1, ls]
                rows = slice(h * P_NK, (h + 1) * P_NK)
                sel = (s1 + s2_ref[rows, ls]) >= tau_ref[h:h + 1, ls]
                g = g + jnp.where(sel, e2_ref[rows, ls] * w1, 0.0)
            rows = slice(al * P_NK, (al + 1) * P_NK)
            act = jax.nn.gelu(a_scr[rows, ls])
            wm_scr[rows, ls] = (act * g).astype(BF16)
    o_ref[...] += jnp.dot(vt_ref[...], wm_scr[...], preferred_element_type=F32)


def _peer_dense(h2t, u_bf, vt_bf, s1, w1, s2, e2, tau, tb, na):
    m = h2t.shape[1]
    col = lambda i, k: (0, i)
    return pl.pallas_call(
        functools.partial(_peer_body, tb=tb, na=na),
        grid=(m // tb, P_NK // na),
        in_specs=[pl.BlockSpec((D_MODEL, tb), col),
                  pl.BlockSpec((na * P_NK, D_MODEL), lambda i, k: (k, 0)),
                  pl.BlockSpec((D_MODEL, na * P_NK), lambda i, k: (0, k)),
                  pl.BlockSpec((na, P_HEADS, tb), lambda i, k: (k, 0, i)),
                  pl.BlockSpec((na, P_HEADS, tb), lambda i, k: (k, 0, i)),
                  pl.BlockSpec((P_HEADS * P_NK, tb), col),
                  pl.BlockSpec((P_HEADS * P_NK, tb), col),
                  pl.BlockSpec((P_HEADS, tb), col)],
        out_specs=pl.BlockSpec((D_MODEL, tb), col),
        out_shape=jax.ShapeDtypeStruct((D_MODEL, m), F32),
        scratch_shapes=[pltpu.VMEM((na * P_NK, tb), F32), pltpu.VMEM((na * P_NK, tb), BF16)],
        compiler_params=_cp("parallel", "arbitrary"),
        name="peer_dense",
    )(h2t, u_bf, vt_bf, s1, w1, s2, e2, tau)


def _final_body(x_ref, p_ref, g_ref, y_ref):
    x = x_ref[...] + p_ref[...]
    ms = jnp.mean(x * x, axis=-1, keepdims=True)
    y_ref[...] = x * lax.rsqrt(ms + EPS) * g_ref[...]


def _final(x2, p, g, tm):
    m = x2.shape[0]
    row = lambda i: (i, 0)
    return pl.pallas_call(
        _final_body,
        grid=(m // tm,),
        in_specs=[pl.BlockSpec((tm, D_MODEL), row), pl.BlockSpec((tm, D_MODEL), row),
                  pl.BlockSpec((1, D_MODEL), lambda i: (0, 0))],
        out_specs=pl.BlockSpec((tm, D_MODEL), row),
        out_shape=jax.ShapeDtypeStruct((m, D_MODEL), F32),
        compiler_params=_cp("parallel"),
        name="final_norm",
    )(x2, p, g)


def _rope_tables(pos):
    half = QK // 2
    inv = 1.0 / (ROPE_THETA ** (jnp.arange(half, dtype=F32) * 2.0 / QK))
    ang = pos.astype(F32)[:, None] * inv[None, :]
    c, s = jnp.cos(ang), jnp.sin(ang)
    cos = jnp.concatenate([c, c, c, c], axis=1)
    sin = jnp.concatenate([-s, s, -s, s], axis=1)
    return cos, sin


def _peer(h2_bf, wqt_bf, keys_bf, u_bf, vt_bf, tb, na):
    h2t = h2_bf.T
    s1, w1, s2, e2, tau = _peer_topk(h2t, wqt_bf, keys_bf, min(tb, 256))
    amajor = lambda a: a.reshape(P_HEADS, P_NK, -1).transpose(1, 0, 2)
    return _peer_dense(h2t, u_bf, vt_bf, amajor(s1), amajor(w1), s2, e2, tau, tb, na).T


def kernel(x_prompt, x_sample, cache_k, cache_v, state_ssm_re, state_ssm_im, page_table, g_mix, w_in, lambda_q1, lambda_k1, lambda_q2, lambda_k2, g_sub, ssm_a_re, ssm_a_im, ssm_log_dt, ssm_b_re, ssm_b_im, ssm_c_re, ssm_c_im, ssm_d, w_glu, b_glu, w_out, g_ffn, peer_w_q, peer_sub_keys, peer_u, peer_v, g_final):
    bp, tp, _ = x_prompt.shape
    bs, ts, _ = x_sample.shape
    assert ts == 1 and g_mix.shape[0] == 1
    n_pages = page_table.shape[1]
    past_len = n_pages * PAGE
    mp = bp * tp
    srows = 16

    w_in_bf = w_in[0].astype(BF16)
    wg_bf = w_glu[0].astype(BF16)
    wo_bf = w_out[0].astype(BF16)
    wqt_bf = peer_w_q[0].T.astype(BF16)
    keys_bf = peer_sub_keys[0].reshape(P_HEADS * 2, P_NK, P_HALF).astype(BF16)
    u_bf = peer_u[0].astype(BF16)
    vt_bf = peer_v[0].T.astype(BF16)
    gmix = g_mix[0].reshape(1, -1)
    gffn = g_ffn[0].reshape(1, -1)
    gfin = g_final.reshape(1, -1)
    gsub = g_sub[0].reshape(1, -1)
    bg = b_glu[0].reshape(1, -1)
    lam4 = jnp.stack([lambda_q1[0], lambda_k1[0], lambda_q2[0], lambda_k2[0]], axis=0)
    pw, bbr, bbi = _s5_prep(ssm_a_re[0], ssm_a_im[0], ssm_log_dt[0], ssm_b_re[0], ssm_b_im[0])
    bblk, cblk, tab, lam1, dvec = _s5_tables(pw, bbr, bbi, ssm_c_re[0], ssm_c_im[0], ssm_d[0])

    xp = x_prompt.reshape(mp, D_MODEL)
    cos_p, sin_p = _rope_tables(jnp.tile(jnp.arange(tp), bp))
    qb, kp, kpb, vp, vpb, up = _inproj(xp, gmix, w_in_bf, cos_p, sin_p, 512)
    attn_p = _attn_prompt(lam4, qb, kpb, vpb, gsub, bp, tp, 256)
    zeros = jnp.zeros((bp, SSM_G * SSM_P), F32)
    yp, rp, ip = _s5_prompt(up, bblk, cblk, dvec, tab, zeros, zeros, bp, tp, 512)
    x2p, h2p = _outproj(xp, attn_p, yp, wg_bf, bg, wo_bf, gffn, 256)
    peer_p = _peer(h2p, wqt_bf, keys_bf, u_bf, vt_bf, 512, 4)
    y_prompt = _final(x2p, peer_p, gfin, 512).reshape(bp, tp, D_MODEL)

    xs = jnp.pad(x_sample.reshape(bs, D_MODEL), ((0, srows - bs), (0, 0)))
    cos_s, sin_s = _rope_tables(jnp.full((srows,), past_len))
    qs, ks, _, vs, _, us = _inproj(xs, gmix, w_in_bf, cos_s, sin_s, srows)
    heads = lambda a: a[:bs].astype(F32).reshape(bs, 1, HEADS, VD)
    attn_s = _attn_sample(page_table, lam4, heads(qs)[:, 0], cache_k, cache_v, heads(ks), heads(vs),
                          gsub, 4 if n_pages % 4 == 0 else 1)
    pad_s = lambda a: jnp.pad(a.reshape(bs, -1), ((0, srows - bs), (0, 0)))
    ys, rs, is_ = _s5_sample(us, bblk, cblk, dvec, lam1, pad_s(state_ssm_re[0]), pad_s(state_ssm_im[0]))
    x2s, h2s = _outproj(xs, pad_s(attn_s), ys, wg_bf, bg, wo_bf, gffn, srows)
    h2s_rep = jnp.tile(h2s[:bs], (128 // bs, 1))
    peer_s = _peer(h2s_rep, wqt_bf, keys_bf, u_bf, vt_bf, 128, 4)[:srows]
    y_sample = _final(x2s, peer_s, gfin, srows)[:bs].reshape(bs, ts, D_MODEL)

    return (y_prompt, y_sample,
            kp.reshape(1, bp, tp, HEADS, 2 * QK), vp.reshape(1, bp, tp, HEADS, VD),
            rp[None], ip[None],
            ks[:bs].reshape(1, bs, ts, HEADS, 2 * QK), vs[:bs].reshape(1, bs, ts, HEADS, VD),
            rs[:bs].reshape(1, bs, SSM_G, SSM_P), is_[:bs].reshape(1, bs, SSM_G, SSM_P))
```

```python
import functools
import math

import jax
import jax.numpy as jnp
from jax import lax
from jax.experimental import pallas as pl
from jax.experimental.pallas import tpu as pltpu

F32 = jnp.float32
BF16 = jnp.bfloat16

D_MODEL = 2048
PAGE = 128
HEADS = 8
QK = 64
VD = 128
AW = HEADS * VD
SSM_W = D_MODEL - AW
SSM_CH = 16
SSM_G = SSM_W // SSM_CH
SSM_P = 64
GB = 16
NB = SSM_G // GB
SL = GB * SSM_P
UL = GB * SSM_CH
P_HEADS = 8
P_TOPK = 16
P_NK = 128
P_HALF = 128
P_E = P_NK * P_NK
EPS = 1e-6
NEG = -1e30
ROPE_THETA = 10000.0
LAM_INIT = 0.8 - 0.6 * math.exp(-0.3 * 0)
VMEM_LIMIT = 56 * 1024 * 1024

STAIR = [(i, j) for i in range(P_TOPK) for j in range(P_TOPK) if (i + 1) * (j + 1) <= P_TOPK]
STAIR_ROWS = -(-len(STAIR) // 8) * 8


def _cp(*sem):
    return pltpu.CompilerParams(dimension_semantics=sem, vmem_limit_bytes=VMEM_LIMIT)


def _inproj_body(x_ref, g_ref, w_ref, cos_ref, sin_ref,
                 q_ref, k_ref, kb_ref, v_ref, vb_ref, u_ref, h_scr):
    j = pl.program_id(1)

    @pl.when(j == 0)
    def _():
        x = x_ref[...]
        ms = jnp.mean(x * x, axis=-1, keepdims=True)
        h_scr[...] = (x * lax.rsqrt(ms + EPS) * g_ref[...]).astype(BF16)

    z = jnp.dot(h_scr[...], w_ref[...], preferred_element_type=F32)

    def rope(zc):
        lane = lax.broadcasted_iota(jnp.int32, zc.shape, 1)
        first = (lane & (QK - 1)) < (QK // 2)
        partner = jnp.where(first, pltpu.roll(zc, 128 - QK // 2, 1), pltpu.roll(zc, QK // 2, 1))
        return zc * cos_ref[...] + partner * sin_ref[...]

    @pl.when(j == 0)
    def _():
        for c in range(AW // 128):
            sl = slice(c * 128, (c + 1) * 128)
            q_ref[:, sl] = (rope(z[:, sl]) * (QK ** -0.5)).astype(BF16)

    @pl.when(j == 1)
    def _():
        for c in range(AW // 128):
            sl = slice(c * 128, (c + 1) * 128)
            kc = rope(z[:, sl])
            k_ref[:, sl] = kc
            kb_ref[:, sl] = kc.astype(BF16)

    @pl.when(j == 2)
    def _():
        v_ref[...] = z
        vb_ref[...] = z.astype(BF16)

    @pl.when(j == 3)
    def _():
        u_ref[...] = z


def _inproj(x, g, w_bf, cos, sin, tm):
    m = x.shape[0]
    row = lambda i, j: (i, 0)
    outs = [jax.ShapeDtypeStruct((m, AW), BF16), jax.ShapeDtypeStruct((m, AW), F32),
            jax.ShapeDtypeStruct((m, AW), BF16), jax.ShapeDtypeStruct((m, AW), F32),
            jax.ShapeDtypeStruct((m, AW), BF16), jax.ShapeDtypeStruct((m, SSM_W), F32)]
    return pl.pallas_call(
        _inproj_body,
        grid=(m // tm, 4),
        in_specs=[pl.BlockSpec((tm, D_MODEL), row),
                  pl.BlockSpec((1, D_MODEL), lambda i, j: (0, 0)),
                  pl.BlockSpec((D_MODEL, AW), lambda i, j: (0, j)),
                  pl.BlockSpec((tm, 128), row),
                  pl.BlockSpec((tm, 128), row)],
        out_specs=[pl.BlockSpec((tm, AW), row) for _ in outs],
        out_shape=outs,
        scratch_shapes=[pltpu.VMEM((tm, D_MODEL), BF16)],
        compiler_params=_cp("parallel", "arbitrary"),
        name="inproj",
    )(x, g, w_bf, cos, sin)


def _lam(lam4_ref):
    l4 = lam4_ref[...]
    a = jnp.exp(jnp.sum(l4[0:1, :] * l4[1:2, :], axis=-1, keepdims=True))
    b = jnp.exp(jnp.sum(l4[2:3, :] * l4[3:4, :], axis=-1, keepdims=True))
    return a - b + LAM_INIT


def _subnorm(o, gsub):
    o = o * lax.rsqrt(jnp.mean(o * o, axis=-1, keepdims=True) + EPS)
    return o * gsub * (1.0 - LAM_INIT)


def _attn_body(lam4_ref, qt_ref, k_ref, vt_ref, gsub_ref, o_ref, acc_scr, *, tq, tk):
    qi = pl.program_id(2)
    unroll = tq // tk
    qt = qt_ref[...]
    row = lax.broadcasted_iota(jnp.int32, qt.shape, 0)
    zero = jnp.zeros_like(qt)
    qc = (jnp.where(row < QK, qt, zero), jnp.where(row >= QK, qt, zero))
    acc_scr[...] = jnp.zeros(acc_scr.shape, F32)
    rel = (lax.broadcasted_iota(jnp.int32, (tk, tq), 1) - lax.broadcasted_iota(jnp.int32, (tk, tq), 0))

    def group(gi, carry, masked):
        ms, ls = list(carry[0]), list(carry[1])
        blocks = [gi * unroll + u for u in range(unroll)]
        kbs = [k_ref[pl.ds(pl.multiple_of(b * tk, tk), tk), :] for b in blocks]
        sts = [[jnp.dot(kb, qc[c], preferred_element_type=F32) for c in range(2)] for kb in kbs]
        for u, b in enumerate(blocks):
            vtb = vt_ref[b]
            for c in range(2):
                st = sts[u][c]
                if masked:
                    st = jnp.where(rel >= u * tk, st, NEG)
                m_new = jnp.maximum(ms[c], jnp.max(st, axis=0, keepdims=True))
                alpha = jnp.exp(ms[c] - m_new)
                p = jnp.exp(st - m_new)
                ls[c] = alpha * ls[c] + jnp.sum(p, axis=0, keepdims=True)
                acc_scr[c] = alpha * acc_scr[c] + jnp.dot(vtb, p.astype(BF16), preferred_element_type=F32)
                ms[c] = m_new
        return tuple(ms), tuple(ls)

    m0 = jnp.full((1, tq), NEG, F32)
    l0 = jnp.zeros((1, tq), F32)
    carry = lax.fori_loop(0, qi, lambda g, c: group(g, c, False), ((m0, m0), (l0, l0)))
    _, ls = group(qi, carry, True)

    o = acc_scr[0] / ls[0] - _lam(lam4_ref) * (acc_scr[1] / ls[1])
    o = o * lax.rsqrt(jnp.mean(o * o, axis=0, keepdims=True) + EPS)
    o_ref[...] = (o * gsub_ref[...] * (1.0 - LAM_INIT)).T


def _attn_prompt(lam4, qb, kb, vb, gsub, nb, t, tq, tk):
    qt = qb.reshape(nb, t, AW).transpose(0, 2, 1)
    kb = kb.reshape(nb, t, AW)
    vt = vb.reshape(nb, t // tk, tk, AW).transpose(0, 1, 3, 2)
    gcol = jnp.broadcast_to(gsub.reshape(VD, 1), (VD, tq))
    out = pl.pallas_call(
        functools.partial(_attn_body, tq=tq, tk=tk),
        grid=(nb, HEADS, t // tq),
        in_specs=[pl.BlockSpec((4, QK), lambda b, h, i: (0, 0)),
                  pl.BlockSpec((None, VD, tq), lambda b, h, i: (b, h, i)),
                  pl.BlockSpec((None, t, VD), lambda b, h, i: (b, 0, h)),
                  pl.BlockSpec((None, t // tk, VD, tk), lambda b, h, i: (b, 0, h, 0)),
                  pl.BlockSpec((VD, tq), lambda b, h, i: (0, 0))],
        out_specs=pl.BlockSpec((None, tq, VD), lambda b, h, i: (b, i, h)),
        out_shape=jax.ShapeDtypeStruct((nb, t, AW), F32),
        scratch_shapes=[pltpu.VMEM((2, VD, tq), F32)],
        compiler_params=_cp("parallel", "parallel", "arbitrary"),
        name="attn_prompt",
    )(lam4, qt, kb, vt, gcol)
    return out.reshape(nb * t, AW)


def _sattn_body(pt_ref, lam4_ref, q_ref, *refs, gp):
    k_refs, v_refs = refs[:gp], refs[gp:2 * gp]
    kn_ref, vn_ref, gsub_ref, o_ref, m_scr, l_scr, acc_scr = refs[2 * gp:]
    j = pl.program_id(1)
    q = q_ref[...]
    first = lax.broadcasted_iota(jnp.int32, (HEADS, VD), 1) < QK

    @pl.when(j == 0)
    def _():
        m_scr[...] = jnp.full(m_scr.shape, NEG, F32)
        l_scr[...] = jnp.zeros(l_scr.shape, F32)
        acc_scr[...] = jnp.zeros(acc_scr.shape, F32)

    def update(kp, vp):
        prod = kp * q[None]
        for c in range(2):
            half = first if c == 0 else jnp.logical_not(first)
            s = jnp.sum(jnp.where(half[None], prod, 0.0), axis=-1, keepdims=True)
            m_prev = m_scr[c]
            m_new = jnp.maximum(m_prev, jnp.max(s, axis=0))
            alpha = jnp.exp(m_prev - m_new)
            p = jnp.exp(s - m_new[None])
            l_scr[c] = alpha * l_scr[c] + jnp.sum(p, axis=0)
            acc_scr[c] = alpha * acc_scr[c] + jnp.sum(p * vp, axis=0)
            m_scr[c] = m_new

    for g in range(gp):
        update(k_refs[g][...], v_refs[g][...])

    @pl.when(j == pl.num_programs(1) - 1)
    def _():
        update(kn_ref[...], vn_ref[...])
        o = acc_scr[0] / l_scr[0] - _lam(lam4_ref) * (acc_scr[1] / l_scr[1])
        o_ref[...] = _subnorm(o, gsub_ref[...])


def _attn_sample(page_table, lam4, q, cache_k, cache_v, kn, vn, gsub, gp):
    bs, n_pages = page_table.shape
    pt = page_table.reshape(-1).astype(jnp.int32)

    def page(g):
        return pl.BlockSpec((None, None, PAGE, HEADS, VD),
                            lambda b, j, pt_ref: (0, pt_ref[b * n_pages + j * gp + g], 0, 0, 0))

    per_b = lambda b, j, p: (b, 0, 0, 0)
    grid_spec = pltpu.PrefetchScalarGridSpec(
        num_scalar_prefetch=1,
        grid=(bs, n_pages // gp),
        in_specs=[pl.BlockSpec((4, QK), lambda b, j, p: (0, 0)),
                  pl.BlockSpec((None, HEADS, VD), lambda b, j, p: (b, 0, 0))]
                 + [page(g) for g in range(gp)] + [page(g) for g in range(gp)]
                 + [pl.BlockSpec((None, 1, HEADS, VD), per_b),
                    pl.BlockSpec((None, 1, HEADS, VD), per_b),
                    pl.BlockSpec((1, VD), lambda b, j, p: (0, 0))],
        out_specs=pl.BlockSpec((None, HEADS, VD), lambda b, j, p: (b, 0, 0)),
        scratch_shapes=[pltpu.VMEM((2, HEADS, VD), F32), pltpu.VMEM((2, HEADS, VD), F32),
                        pltpu.VMEM((2, HEADS, VD), F32)],
    )
    out = pl.pallas_call(
        functools.partial(_sattn_body, gp=gp),
        grid_spec=grid_spec,
        out_shape=jax.ShapeDtypeStruct((bs, HEADS, VD), F32),
        compiler_params=_cp("parallel", "arbitrary"),
        name="attn_sample",
    )(pt, lam4, q, *([cache_k] * gp), *([cache_v] * gp), kn, vn, gsub)
    return out.reshape(bs, AW)


def _cmul(ar, ai, br, bi):
    return ar * br - ai * bi, ar * bi + ai * br


def _s5_prep_body(are_ref, aim_ref, ldt_ref, arer_ref, aimr_ref, ldtr_ref, bre_ref, bim_ref,
                  pow_ref, bbr_ref, bbi_ref):
    def disc(are, aim, ldt):
        dt = jnp.exp(ldt)
        mag = jnp.exp(are * dt)
        lr = mag * jnp.cos(aim * dt)
        li = mag * jnp.sin(aim * dt)
        return lr, li

    lr, li = disc(are_ref[...], aim_ref[...], ldt_ref[...])
    pr, pi = lr, li
    for n in range(8):
        pow_ref[2 * n] = pr
        pow_ref[2 * n + 1] = pi
        pr, pi = _cmul(pr, pi, lr, li)

    are, aim = arer_ref[...], aimr_ref[...]
    lrr, lir = disc(are, aim, ldtr_ref[...])
    den = are * are + aim * aim
    zr = lrr - 1.0
    fr = (zr * are + lir * aim) / den
    fi = (lir * are - zr * aim) / den
    br, bi = bre_ref[...], bim_ref[...]
    bbr_ref[...] = fr * br - fi * bi
    bbi_ref[...] = fr * bi + fi * br


def _s5_prep(a_re, a_im, log_dt, b_re, b_im):
    rep = lambda a: jnp.repeat(a, SSM_CH, axis=1)
    ldt = jnp.broadcast_to(log_dt[:, None], (SSM_G, SSM_P))
    pw, bbr, bbi = pl.pallas_call(
        _s5_prep_body,
        out_shape=[jax.ShapeDtypeStruct((16, SSM_G, SSM_P), F32),
                   jax.ShapeDtypeStruct((SSM_G, SSM_P * SSM_CH), F32),
                   jax.ShapeDtypeStruct((SSM_G, SSM_P * SSM_CH), F32)],
        name="s5_prep",
    )(a_re, a_im, ldt, rep(a_re), rep(a_im), rep(ldt),
      b_re.reshape(SSM_G, -1), b_im.reshape(SSM_G, -1))
    return pw, bbr.reshape(SSM_G, SSM_P, SSM_CH), bbi.reshape(SSM_G, SSM_P, SSM_CH)


def _s5_tables(pw, bbr, bbi, c_re, c_im, d):
    eye = jnp.eye(GB, dtype=F32)
    def blk_in(bb):
        x = bb.reshape(NB, GB, SSM_P, SSM_CH)
        return jnp.einsum('ngpc,gh->ngchp', x, eye).reshape(NB, UL, SL)
    bblk = jnp.concatenate([blk_in(bbr), blk_in(bbi)], axis=2).astype(BF16)
    def blk_out(cc):
        x = cc.reshape(NB, GB, SSM_CH, SSM_P)
        return jnp.einsum('ngcp,gh->ngphc', x, eye).reshape(NB, SL, UL)
    cblk = jnp.concatenate([blk_out(c_re), blk_out(-c_im)], axis=1).astype(BF16)
    lanes = lambda a: a.reshape(NB, 1, SL)
    t_idx = jnp.arange(8)[None, :, None]
    tabs = []
    for n in (1, 2, 4):
        keep = (t_idx >= n).astype(F32)
        tabs += [lanes(pw[2 * (n - 1)]) * keep, lanes(pw[2 * (n - 1) + 1]) * keep]
    pr = jnp.stack([pw[2 * n] for n in range(8)], axis=0).reshape(8, NB, SL).transpose(1, 0, 2)
    pi = jnp.stack([pw[2 * n + 1] for n in range(8)], axis=0).reshape(8, NB, SL).transpose(1, 0, 2)
    tabs += [pr, pi]
    tab = jnp.stack(tabs, axis=1)
    lam1 = jnp.stack([lanes(pw[0]), lanes(pw[1])], axis=1).reshape(NB, 2, SL)
    dvec = d.reshape(NB, 1, UL)
    return bblk, cblk, tab, lam1, dvec


def _s5_body(u_ref, bblk_ref, cblk_ref, d_ref, tab_ref, h0r_ref, h0i_ref,
             y_ref, hr_ref, hi_ref, st_scr, cr_scr, ci_scr, *, chunk):
    k = pl.program_id(2)

    @pl.when(k == 0)
    def _():
        cr_scr[...] = h0r_ref[...]
        ci_scr[...] = h0i_ref[...]

    u = u_ref[...]
    st_scr[...] = jnp.dot(u.astype(BF16), bblk_ref[...], preferred_element_type=F32)

    def tile(r, carry):
        cr, ci = carry
        rows = pl.ds(pl.multiple_of(r * 8, 8), 8)
        xr = st_scr[rows, 0:SL]
        xi = st_scr[rows, SL:2 * SL]
        for lvl, sh in enumerate((1, 2, 4)):
            ar, ai = tab_ref[2 * lvl], tab_ref[2 * lvl + 1]
            sr, si = pltpu.roll(xr, sh, 0), pltpu.roll(xi, sh, 0)
            xr, xi = xr + (ar * sr - ai * si), xi + (ar * si + ai * sr)
        pr, pi = tab_ref[6], tab_ref[7]
        hr = xr + (pr * cr - pi * ci)
        hi = xi + (pr * ci + pi * cr)
        st_scr[rows, 0:SL] = hr
        st_scr[rows, SL:2 * SL] = hi
        return hr[7:8, :], hi[7:8, :]

    cr, ci = lax.fori_loop(0, chunk // 8, tile, (cr_scr[...], ci_scr[...]))
    cr_scr[...] = cr
    ci_scr[...] = ci
    y_ref[...] = (jnp.dot(st_scr[...].astype(BF16), cblk_ref[...], preferred_element_type=F32)
                  + d_ref[...] * u)

    @pl.when(k == pl.num_programs(2) - 1)
    def _():
        hr_ref[...] = cr
        hi_ref[...] = ci


def _s5_prompt(u, bblk, cblk, dvec, tab, h0r, h0i, nb, t, chunk):
    nk = t // chunk
    st = jax.ShapeDtypeStruct((nb, 1, SSM_G * SSM_P), F32)
    y, hr, hi = pl.pallas_call(
        functools.partial(_s5_body, chunk=chunk),
        grid=(nb, NB, nk),
        in_specs=[pl.BlockSpec((chunk, UL), lambda b, n, k: (b * nk + k, n)),
                  pl.BlockSpec((None, UL, 2 * SL), lambda b, n, k: (n, 0, 0)),
                  pl.BlockSpec((None, 2 * SL, UL), lambda b, n, k: (n, 0, 0)),
                  pl.BlockSpec((None, 1, UL), lambda b, n, k: (n, 0, 0)),
                  pl.BlockSpec((None, 8, 8, SL), lambda b, n, k: (n, 0, 0, 0)),
                  pl.BlockSpec((None, 1, SL), lambda b, n, k: (b, 0, n)),
                  pl.BlockSpec((None, 1, SL), lambda b, n, k: (b, 0, n))],
        out_specs=[pl.BlockSpec((chunk, UL), lambda b, n, k: (b * nk + k, n)),
                   pl.BlockSpec((None, 1, SL), lambda b, n, k: (b, 0, n)),
                   pl.BlockSpec((None, 1, SL), lambda b, n, k: (b, 0, n))],
        out_shape=[jax.ShapeDtypeStruct((nb * t, SSM_W), F32), st, st],
        scratch_shapes=[pltpu.VMEM((chunk, 2 * SL), F32), pltpu.VMEM((1, SL), F32),
                        pltpu.VMEM((1, SL), F32)],
        compiler_params=_cp("parallel", "parallel", "arbitrary"),
        name="s5_prompt",
    )(u, bblk, cblk, dvec, tab, h0r.reshape(nb, 1, -1), h0i.reshape(nb, 1, -1))
    return y, hr.reshape(nb, SSM_G, SSM_P), hi.reshape(nb, SSM_G, SSM_P)


def _s5_step_body(u_ref, bblk_ref, cblk_ref, d_ref, lam_ref, h0r_ref, h0i_ref, y_ref, hr_ref, hi_ref):
    u = u_ref[...]
    bu = jnp.dot(u.astype(BF16), bblk_ref[...], preferred_element_type=F32)
    lr, li = lam_ref[0:1, :], lam_ref[1:2, :]
    h0r, h0i = h0r_ref[...], h0i_ref[...]
    hr = bu[:, 0:SL] + (lr * h0r - li * h0i)
    hi = bu[:, SL:2 * SL] + (lr * h0i + li * h0r)
    hr_ref[...] = hr
    hi_ref[...] = hi
    hcat = jnp.concatenate([hr, hi], axis=1).astype(BF16)
    y_ref[...] = jnp.dot(hcat, cblk_ref[...], preferred_element_type=F32) + d_ref[...] * u


def _s5_sample(u, bblk, cblk, dvec, lam1, h0r, h0i):
    rows = u.shape[0]
    st = jax.ShapeDtypeStruct((rows, SSM_G * SSM_P), F32)
    return pl.pallas_call(
        _s5_step_body,
        grid=(NB,),
        in_specs=[pl.BlockSpec((rows, UL), lambda n: (0, n)),
                  pl.BlockSpec((None, UL, 2 * SL), lambda n: (n, 0, 0)),
                  pl.BlockSpec((None, 2 * SL, UL), lambda n: (n, 0, 0)),
                  pl.BlockSpec((None, 1, UL), lambda n: (n, 0, 0)),
                  pl.BlockSpec((None, 2, SL), lambda n: (n, 0, 0)),
                  pl.BlockSpec((rows, SL), lambda n: (0, n)),
                  pl.BlockSpec((rows, SL), lambda n: (0, n))],
        out_specs=[pl.BlockSpec((rows, UL), lambda n: (0, n)),
                   pl.BlockSpec((rows, SL), lambda n: (0, n)),
                   pl.BlockSpec((rows, SL), lambda n: (0, n))],
        out_shape=[jax.ShapeDtypeStruct((rows, SSM_W), F32), st, st],
        compiler_params=_cp("parallel"),
        name="s5_sample",
    )(u, bblk, cblk, dvec, lam1, h0r, h0i)


def _outproj_body(x_ref, a_ref, y_ref, wg_ref, bg_ref, wo_ref, g_ref, x2_ref, h2_ref):
    y = jax.nn.gelu(y_ref[...])
    gate = jnp.dot(y.astype(BF16), wg_ref[...], preferred_element_type=F32) + bg_ref[...]
    y = y * jax.nn.sigmoid(gate)
    mix = (jnp.dot(a_ref[...].astype(BF16), wo_ref[0:AW, :], preferred_element_type=F32)
           + jnp.dot(y.astype(BF16), wo_ref[AW:, :], preferred_element_type=F32))
    x2 = x_ref[...] + mix
    x2_ref[...] = x2
    ms = jnp.mean(x2 * x2, axis=-1, keepdims=True)
    h2_ref[...] = (x2 * lax.rsqrt(ms + EPS) * g_ref[...]).astype(BF16)


def _outproj(x, attn, y, wg_bf, bg, wo_bf, g, tm):
    m = x.shape[0]
    row = lambda i: (i, 0)
    fix = lambda i: (0, 0)
    return pl.pallas_call(
        _outproj_body,
        grid=(m // tm,),
        in_specs=[pl.BlockSpec((tm, D_MODEL), row), pl.BlockSpec((tm, AW), row),
                  pl.BlockSpec((tm, SSM_W), row), pl.BlockSpec((SSM_W, SSM_W), fix),
                  pl.BlockSpec((1, SSM_W), fix), pl.BlockSpec((D_MODEL, D_MODEL), fix),
                  pl.BlockSpec((1, D_MODEL), fix)],
        out_specs=[pl.BlockSpec((tm, D_MODEL), row), pl.BlockSpec((tm, D_MODEL), row)],
        out_shape=[jax.ShapeDtypeStruct((m, D_MODEL), F32), jax.ShapeDtypeStruct((m, D_MODEL), BF16)],
        compiler_params=_cp("parallel"),
        name="outproj",
    )(x, attn, y, wg_bf, bg, wo_bf, g)


def _topk_body(h_ref, wq_ref, keys_ref, s1_ref, w1_ref, s2_ref, e2_ref, tau_ref, q_scr, cand_scr, *, tb):
    q_scr[...] = jnp.dot(wq_ref[...], h_ref[...], preferred_element_type=F32).astype(BF16)
    ninf = -jnp.inf

    def top16(s):
        vals = []
        w = s
        for _ in range(P_TOPK):
            mx = jnp.max(w, axis=0, keepdims=True)
            vals.append(mx)
            w = jnp.where(w == mx, ninf, w)
        return vals

    hrow = lax.broadcasted_iota(jnp.int32, (P_HEADS, 128), 0)

    def head(h, taus):
        taus = list(taus)
        for lc in range(tb // 128):
            ls = slice(lc * 128, (lc + 1) * 128)
            sc = []
            for c in range(2):
                hc = 2 * h + c
                qrows = pl.ds(pl.multiple_of(hc * P_HALF, P_HALF), P_HALF)
                sc.append(jnp.dot(keys_ref[hc], q_scr[qrows, ls],
                                  preferred_element_type=F32))
            v1 = top16(sc[0])
            v2 = top16(sc[1])
            cand_scr[...] = jnp.full(cand_scr.shape, ninf, F32)
            for n, (i, j) in enumerate(STAIR):
                cand_scr[n:n + 1, :] = v1[i] + v2[j]
            w = cand_scr[...]
            top = v1[0] + v2[0]
            z = jnp.zeros_like(top)
            mx = top
            for _ in range(P_TOPK):
                mx = jnp.max(w, axis=0, keepdims=True)
                z = z + jnp.exp(mx - top)
                w = jnp.where(w == mx, ninf, w)
            rows = pl.ds(pl.multiple_of(h * P_NK, P_NK), P_NK)
            taus[lc] = jnp.where(hrow == h, mx, taus[lc])
            s1_ref[rows, ls] = sc[0]
            s2_ref[rows, ls] = sc[1]
            w1_ref[rows, ls] = jnp.exp(sc[0] - v1[0]) / z
            e2_ref[rows, ls] = jnp.exp(sc[1] - v2[0])
        return tuple(taus)

    taus = lax.fori_loop(0, P_HEADS, head,
                         tuple(jnp.zeros((P_HEADS, 128), F32) for _ in range(tb // 128)))
    for lc in range(tb // 128):
        tau_ref[:, lc * 128:(lc + 1) * 128] = taus[lc]


def _peer_topk(h2t, wqt_bf, keys_bf, tb):
    m = h2t.shape[1]
    big = jax.ShapeDtypeStruct((P_HEADS * P_NK, m), F32)
    col = lambda i: (0, i)
    return pl.pallas_call(
        functools.partial(_topk_body, tb=tb),
        grid=(m // tb,),
        in_specs=[pl.BlockSpec((D_MODEL, tb), col),
                  pl.BlockSpec((P_HEADS * 2 * P_HALF, D_MODEL), lambda i: (0, 0)),
                  pl.BlockSpec((P_HEADS * 2, P_NK, P_HALF), lambda i: (0, 0, 0))],
        out_specs=[pl.BlockSpec((P_HEADS * P_NK, tb), col)] * 4 + [pl.BlockSpec((P_HEADS, tb), col)],
        out_shape=[big, big, big, big, jax.ShapeDtypeStruct((P_HEADS, m), F32)],
        scratch_shapes=[pltpu.VMEM((P_HEADS * 2 * P_HALF, tb), BF16), pltpu.VMEM((STAIR_ROWS, 128), F32)],
        compiler_params=_cp("parallel"),
        name="peer_topk",
    )(h2t, wqt_bf, keys_bf)


def _peer_body(h_ref, u_ref, vt_ref, s1_ref, w1_ref, s2_ref, e2_ref, tau_ref, o_ref,
               a_scr, wm_scr, *, tb, na):
    step = pl.program_id(1)

    @pl.when(step == 0)
    def _():
        o_ref[...] = jnp.zeros(o_ref.shape, F32)

    rt = P_NK // 2

    def per_a(al, carry):
        arows = pl.ds(pl.multiple_of(al * P_NK, P_NK), P_NK)
        a_scr[...] = jnp.dot(u_ref[arows, :], h_ref[...], preferred_element_type=F32)
        for lc in range(tb // 128):
            ls = slice(lc * 128, (lc + 1) * 128)
            for rb in range(P_NK // rt):
                g = jnp.zeros((rt, 128), F32)
                for h in range(P_HEADS):
                    rows = slice(h * P_NK + rb * rt, h * P_NK + (rb + 1) * rt)
                    sel = (s1_ref[al, h:h + 1, ls] + s2_ref[rows, ls]) >= tau_ref[h:h + 1, ls]
                    g = g + jnp.where(sel, e2_ref[rows, ls] * w1_ref[al, h:h + 1, ls], 0.0)
                act = jax.nn.gelu(a_scr[rb * rt:(rb + 1) * rt, ls])
                rows = pl.ds(pl.multiple_of(al * P_NK + rb * rt, rt), rt)
                wm_scr[rows, ls] = (act * g).astype(BF16)
        return carry

    lax.fori_loop(0, na, per_a, 0)
    o_ref[...] += jnp.dot(vt_ref[...], wm_scr[...], preferred_element_type=F32)


def _peer_dense(h2t, u_bf, vt_bf, s1, w1, s2, e2, tau, tb, na):
    m = h2t.shape[1]
    col = lambda i, k: (0, i)
    return pl.pallas_call(
        functools.partial(_peer_body, tb=tb, na=na),
        grid=(m // tb, P_NK // na),
        in_specs=[pl.BlockSpec((D_MODEL, tb), col),
                  pl.BlockSpec((na * P_NK, D_MODEL), lambda i, k: (k, 0)),
                  pl.BlockSpec((D_MODEL, na * P_NK), lambda i, k: (0, k)),
                  pl.BlockSpec((na, P_HEADS, tb), lambda i, k: (k, 0, i)),
                  pl.BlockSpec((na, P_HEADS, tb), lambda i, k: (k, 0, i)),
                  pl.BlockSpec((P_HEADS * P_NK, tb), col),
                  pl.BlockSpec((P_HEADS * P_NK, tb), col),
                  pl.BlockSpec((P_HEADS, tb), col)],
        out_specs=pl.BlockSpec((D_MODEL, tb), col),
        out_shape=jax.ShapeDtypeStruct((D_MODEL, m), F32),
        scratch_shapes=[pltpu.VMEM((P_NK, tb), F32), pltpu.VMEM((na * P_NK, tb), BF16)],
        compiler_params=_cp("parallel", "arbitrary"),
        name="peer_dense",
    )(h2t, u_bf, vt_bf, s1, w1, s2, e2, tau)


def _final_body(x_ref, p_ref, g_ref, y_ref):
    x = x_ref[...] + p_ref[...]
    ms = jnp.mean(x * x, axis=-1, keepdims=True)
    y_ref[...] = x * lax.rsqrt(ms + EPS) * g_ref[...]


def _final(x2, p, g, tm):
    m = x2.shape[0]
    row = lambda i: (i, 0)
    return pl.pallas_call(
        _final_body,
        grid=(m // tm,),
        in_specs=[pl.BlockSpec((tm, D_MODEL), row), pl.BlockSpec((tm, D_MODEL), row),
                  pl.BlockSpec((1, D_MODEL), lambda i: (0, 0))],
        out_specs=pl.BlockSpec((tm, D_MODEL), row),
        out_shape=jax.ShapeDtypeStruct((m, D_MODEL), F32),
        compiler_params=_cp("parallel"),
        name="final_norm",
    )(x2, p, g)


def _rope_tables(pos):
    half = QK // 2
    inv = 1.0 / (ROPE_THETA ** (jnp.arange(half, dtype=F32) * 2.0 / QK))
    ang = pos.astype(F32)[:, None] * inv[None, :]
    c, s = jnp.cos(ang), jnp.sin(ang)
    cos = jnp.concatenate([c, c, c, c], axis=1)
    sin = jnp.concatenate([-s, s, -s, s], axis=1)
    return cos, sin


def _peer(h2_bf, wqt_bf, keys_bf, u_bf, vt_bf, tb, na):
    h2t = h2_bf.T
    s1, w1, s2, e2, tau = _peer_topk(h2t, wqt_bf, keys_bf, min(tb, 256))
    amajor = lambda a: a.reshape(P_HEADS, P_NK, -1).transpose(1, 0, 2)
    return _peer_dense(h2t, u_bf, vt_bf, amajor(s1), amajor(w1), s2, e2, tau, tb, na).T


def kernel(x_prompt, x_sample, cache_k, cache_v, state_ssm_re, state_ssm_im, page_table, g_mix, w_in, lambda_q1, lambda_k1, lambda_q2, lambda_k2, g_sub, ssm_a_re, ssm_a_im, ssm_log_dt, ssm_b_re, ssm_b_im, ssm_c_re, ssm_c_im, ssm_d, w_glu, b_glu, w_out, g_ffn, peer_w_q, peer_sub_keys, peer_u, peer_v, g_final):
    bp, tp, _ = x_prompt.shape
    bs, ts, _ = x_sample.shape
    assert ts == 1 and g_mix.shape[0] == 1
    n_pages = page_table.shape[1]
    past_len = n_pages * PAGE
    mp = bp * tp
    srows = 16

    w_in_bf = w_in[0].astype(BF16)
    wg_bf = w_glu[0].astype(BF16)
    wo_bf = w_out[0].astype(BF16)
    wqt_bf = peer_w_q[0].T.astype(BF16)
    keys_bf = peer_sub_keys[0].reshape(P_HEADS * 2, P_NK, P_HALF).astype(BF16)
    u_bf = peer_u[0].astype(BF16)
    vt_bf = peer_v[0].T.astype(BF16)
    gmix = g_mix[0].reshape(1, -1)
    gffn = g_ffn[0].reshape(1, -1)
    gfin = g_final.reshape(1, -1)
    gsub = g_sub[0].reshape(1, -1)
    bg = b_glu[0].reshape(1, -1)
    lam4 = jnp.stack([lambda_q1[0], lambda_k1[0], lambda_q2[0], lambda_k2[0]], axis=0)
    pw, bbr, bbi = _s5_prep(ssm_a_re[0], ssm_a_im[0], ssm_log_dt[0], ssm_b_re[0], ssm_b_im[0])
    bblk, cblk, tab, lam1, dvec = _s5_tables(pw, bbr, bbi, ssm_c_re[0], ssm_c_im[0], ssm_d[0])

    xp = x_prompt.reshape(mp, D_MODEL)
    cos_p, sin_p = _rope_tables(jnp.tile(jnp.arange(tp), bp))
    qb, kp, kpb, vp, vpb, up = _inproj(xp, gmix, w_in_bf, cos_p, sin_p, 512)
    attn_p = _attn_prompt(lam4, qb, kpb, vpb, gsub, bp, tp, 512, 128)
    zeros = jnp.zeros((bp, SSM_G * SSM_P), F32)
    yp, rp, ip = _s5_prompt(up, bblk, cblk, dvec, tab, zeros, zeros, bp, tp, 512)
    x2p, h2p = _outproj(xp, attn_p, yp, wg_bf, bg, wo_bf, gffn, 256)
    peer_p = _peer(h2p, wqt_bf, keys_bf, u_bf, vt_bf, 512, 8)
    y_prompt = _final(x2p, peer_p, gfin, 512).reshape(bp, tp, D_MODEL)

    xs = jnp.pad(x_sample.reshape(bs, D_MODEL), ((0, srows - bs), (0, 0)))
    cos_s, sin_s = _rope_tables(jnp.full((srows,), past_len))
    qs, ks, _, vs, _, us = _inproj(xs, gmix, w_in_bf, cos_s, sin_s, srows)
    heads = lambda a: a[:bs].astype(F32).reshape(bs, 1, HEADS, VD)
    attn_s = _attn_sample(page_table, lam4, heads(qs)[:, 0], cache_k, cache_v, heads(ks), heads(vs),
                          gsub, 4 if n_pages % 4 == 0 else 1)
    pad_s = lambda a: jnp.pad(a.reshape(bs, -1), ((0, srows - bs), (0, 0)))
    ys, rs, is_ = _s5_sample(us, bblk, cblk, dvec, lam1, pad_s(state_ssm_re[0]), pad_s(state_ssm_im[0]))
    x2s, h2s = _outproj(xs, pad_s(attn_s), ys, wg_bf, bg, wo_bf, gffn, srows)
    h2s_rep = jnp.tile(h2s[:bs], (128 // bs, 1))
    peer_s = _peer(h2s_rep, wqt_bf, keys_bf, u_bf, vt_bf, 128, 8)[:srows]
    y_sample = _final(x2s, peer_s, gfin, srows)[:bs].reshape(bs, ts, D_MODEL)

    return (y_prompt, y_sample,
            kp.reshape(1, bp, tp, HEADS, 2 * QK), vp.reshape(1, bp, tp, HEADS, VD),
            rp[None], ip[None],
            ks[:bs].reshape(1, bs, ts, HEADS, 2 * QK), vs[:bs].reshape(1, bs, ts, HEADS, VD),
            rs[:bs].reshape(1, bs, SSM_G, SSM_P), is_[:bs].reshape(1, bs, SSM_G, SSM_P))
```

```python
import functools
import math

import jax
import jax.numpy as jnp
from jax import lax
from jax.experimental import pallas as pl
from jax.experimental.pallas import tpu as pltpu

F32 = jnp.float32
BF16 = jnp.bfloat16

D_MODEL = 2048
PAGE = 128
HEADS = 8
QK = 64
VD = 128
AW = HEADS * VD
SSM_W = D_MODEL - AW
SSM_CH = 16
SSM_G = SSM_W // SSM_CH
SSM_P = 64
GB = 16
NB = SSM_G // GB
SL = GB * SSM_P
UL = GB * SSM_CH
P_HEADS = 8
P_TOPK = 16
P_NK = 128
P_HALF = 128
P_E = P_NK * P_NK
EPS = 1e-6
NEG = -1e30
ROPE_THETA = 10000.0
LAM_INIT = 0.8 - 0.6 * math.exp(-0.3 * 0)
VMEM_LIMIT = 56 * 1024 * 1024

STAIR = [(i, j) for i in range(P_TOPK) for j in range(P_TOPK) if (i + 1) * (j + 1) <= P_TOPK]
STAIR_ROWS = -(-len(STAIR) // 8) * 8


def _cp(*sem):
    return pltpu.CompilerParams(dimension_semantics=sem, vmem_limit_bytes=VMEM_LIMIT)


def _inproj_body(x_ref, g_ref, w_ref, cos_ref, sin_ref,
                 q_ref, k_ref, kb_ref, v_ref, vb_ref, u_ref, h_scr):
    j = pl.program_id(1)

    @pl.when(j == 0)
    def _():
        x = x_ref[...]
        ms = jnp.mean(x * x, axis=-1, keepdims=True)
        h_scr[...] = (x * lax.rsqrt(ms + EPS) * g_ref[...]).astype(BF16)

    z = jnp.dot(h_scr[...], w_ref[...], preferred_element_type=F32)

    def rope(zc):
        lane = lax.broadcasted_iota(jnp.int32, zc.shape, 1)
        first = (lane & (QK - 1)) < (QK // 2)
        partner = jnp.where(first, pltpu.roll(zc, 128 - QK // 2, 1), pltpu.roll(zc, QK // 2, 1))
        return zc * cos_ref[...] + partner * sin_ref[...]

    @pl.when(j == 0)
    def _():
        for c in range(AW // 128):
            sl = slice(c * 128, (c + 1) * 128)
            q_ref[:, sl] = (rope(z[:, sl]) * (QK ** -0.5)).astype(BF16)

    @pl.when(j == 1)
    def _():
        for c in range(AW // 128):
            sl = slice(c * 128, (c + 1) * 128)
            kc = rope(z[:, sl])
            k_ref[:, sl] = kc
            kb_ref[:, sl] = kc.astype(BF16)

    @pl.when(j == 2)
    def _():
        v_ref[...] = z
        vb_ref[...] = z.astype(BF16)

    @pl.when(j == 3)
    def _():
        u_ref[...] = z


def _inproj(x, g, w_bf, cos, sin, tm):
    m = x.shape[0]
    row = lambda i, j: (i, 0)
    outs = [jax.ShapeDtypeStruct((m, AW), BF16), jax.ShapeDtypeStruct((m, AW), F32),
            jax.ShapeDtypeStruct((m, AW), BF16), jax.ShapeDtypeStruct((m, AW), F32),
            jax.ShapeDtypeStruct((m, AW), BF16), jax.ShapeDtypeStruct((m, SSM_W), F32)]
    return pl.pallas_call(
        _inproj_body,
        grid=(m // tm, 4),
        in_specs=[pl.BlockSpec((tm, D_MODEL), row),
                  pl.BlockSpec((1, D_MODEL), lambda i, j: (0, 0)),
                  pl.BlockSpec((D_MODEL, AW), lambda i, j: (0, j)),
                  pl.BlockSpec((tm, 128), row),
                  pl.BlockSpec((tm, 128), row)],
        out_specs=[pl.BlockSpec((tm, AW), row) for _ in outs],
        out_shape=outs,
        scratch_shapes=[pltpu.VMEM((tm, D_MODEL), BF16)],
        compiler_params=_cp("parallel", "arbitrary"),
        name="inproj",
    )(x, g, w_bf, cos, sin)


def _lam(lam4_ref):
    l4 = lam4_ref[...]
    a = jnp.exp(jnp.sum(l4[0:1, :] * l4[1:2, :], axis=-1, keepdims=True))
    b = jnp.exp(jnp.sum(l4[2:3, :] * l4[3:4, :], axis=-1, keepdims=True))
    return a - b + LAM_INIT


def _subnorm(o, gsub):
    o = o * lax.rsqrt(jnp.mean(o * o, axis=-1, keepdims=True) + EPS)
    return o * gsub * (1.0 - LAM_INIT)


def _attn_body(lam4_ref, qt_ref, k_ref, vt_ref, gsub_ref, o_ref, acc_scr, *, tq, tk):
    qi = pl.program_id(2)
    unroll = tq // tk
    qt = qt_ref[...]
    row = lax.broadcasted_iota(jnp.int32, qt.shape, 0)
    zero = jnp.zeros_like(qt)
    qc = (jnp.where(row < QK, qt, zero), jnp.where(row >= QK, qt, zero))
    acc_scr[...] = jnp.zeros(acc_scr.shape, F32)
    rel = (lax.broadcasted_iota(jnp.int32, (tk, tq), 1) - lax.broadcasted_iota(jnp.int32, (tk, tq), 0))

    def group(gi, carry, masked):
        ms, ls = list(carry[0]), list(carry[1])
        blocks = [gi * unroll + u for u in range(unroll)]
        kbs = [k_ref[pl.ds(pl.multiple_of(b * tk, tk), tk), :] for b in blocks]
        sts = [[jnp.dot(kb, qc[c], preferred_element_type=F32) for c in range(2)] for kb in kbs]
        for u, b in enumerate(blocks):
            vtb = vt_ref[b]
            for c in range(2):
                st = sts[u][c]
                if masked:
                    st = jnp.where(rel >= u * tk, st, NEG)
                m_new = jnp.maximum(ms[c], jnp.max(st, axis=0, keepdims=True))
                alpha = jnp.exp(ms[c] - m_new)
                p = jnp.exp(st - m_new)
                ls[c] = alpha * ls[c] + jnp.sum(p, axis=0, keepdims=True)
                acc_scr[c] = alpha * acc_scr[c] + jnp.dot(vtb, p.astype(BF16), preferred_element_type=F32)
                ms[c] = m_new
        return tuple(ms), tuple(ls)

    m0 = jnp.full((1, tq), NEG, F32)
    l0 = jnp.zeros((1, tq), F32)
    carry = lax.fori_loop(0, qi, lambda g, c: group(g, c, False), ((m0, m0), (l0, l0)))
    _, ls = group(qi, carry, True)

    o = acc_scr[0] / ls[0] - _lam(lam4_ref) * (acc_scr[1] / ls[1])
    o = o * lax.rsqrt(jnp.mean(o * o, axis=0, keepdims=True) + EPS)
    o_ref[...] = (o * gsub_ref[...] * (1.0 - LAM_INIT)).T


def _attn_prompt(lam4, qb, kb, vb, gsub, nb, t, tq, tk):
    qt = qb.reshape(nb, t, AW).transpose(0, 2, 1)
    kb = kb.reshape(nb, t, AW)
    vt = vb.reshape(nb, t // tk, tk, AW).transpose(0, 1, 3, 2)
    gcol = jnp.broadcast_to(gsub.reshape(VD, 1), (VD, tq))
    out = pl.pallas_call(
        functools.partial(_attn_body, tq=tq, tk=tk),
        grid=(nb, HEADS, t // tq),
        in_specs=[pl.BlockSpec((4, QK), lambda b, h, i: (0, 0)),
                  pl.BlockSpec((None, VD, tq), lambda b, h, i: (b, h, i)),
                  pl.BlockSpec((None, t, VD), lambda b, h, i: (b, 0, h)),
                  pl.BlockSpec((None, t // tk, VD, tk), lambda b, h, i: (b, 0, h, 0)),
                  pl.BlockSpec((VD, tq), lambda b, h, i: (0, 0))],
        out_specs=pl.BlockSpec((None, tq, VD), lambda b, h, i: (b, i, h)),
        out_shape=jax.ShapeDtypeStruct((nb, t, AW), F32),
        scratch_shapes=[pltpu.VMEM((2, VD, tq), F32)],
        compiler_params=_cp("parallel", "parallel", "arbitrary"),
        name="attn_prompt",
    )(lam4, qt, kb, vt, gcol)
    return out.reshape(nb * t, AW)


def _sattn_body(pt_ref, lam4_ref, q_ref, *refs, gp):
    k_refs, v_refs = refs[:gp], refs[gp:2 * gp]
    kn_ref, vn_ref, gsub_ref, o_ref, m_scr, l_scr, acc_scr = refs[2 * gp:]
    j = pl.program_id(1)
    q = q_ref[...]
    first = lax.broadcasted_iota(jnp.int32, (HEADS, VD), 1) < QK

    @pl.when(j == 0)
    def _():
        m_scr[...] = jnp.full(m_scr.shape, NEG, F32)
        l_scr[...] = jnp.zeros(l_scr.shape, F32)
        acc_scr[...] = jnp.zeros(acc_scr.shape, F32)

    def update(kp, vp):
        prod = kp * q[None]
        for c in range(2):
            half = first if c == 0 else jnp.logical_not(first)
            s = jnp.sum(jnp.where(half[None], prod, 0.0), axis=-1, keepdims=True)
            m_prev = m_scr[c]
            m_new = jnp.maximum(m_prev, jnp.max(s, axis=0))
            alpha = jnp.exp(m_prev - m_new)
            p = jnp.exp(s - m_new[None])
            l_scr[c] = alpha * l_scr[c] + jnp.sum(p, axis=0)
            acc_scr[c] = alpha * acc_scr[c] + jnp.sum(p * vp, axis=0)
            m_scr[c] = m_new

    for g in range(gp):
        update(k_refs[g][...], v_refs[g][...])

    @pl.when(j == pl.num_programs(1) - 1)
    def _():
        update(kn_ref[...], vn_ref[...])
        o = acc_scr[0] / l_scr[0] - _lam(lam4_ref) * (acc_scr[1] / l_scr[1])
        o_ref[...] = _subnorm(o, gsub_ref[...])


def _attn_sample(page_table, lam4, q, cache_k, cache_v, kn, vn, gsub, gp):
    bs, n_pages = page_table.shape
    pt = page_table.reshape(-1).astype(jnp.int32)

    def page(g):
        return pl.BlockSpec((None, None, PAGE, HEADS, VD),
                            lambda b, j, pt_ref: (0, pt_ref[b * n_pages + j * gp + g], 0, 0, 0))

    per_b = lambda b, j, p: (b, 0, 0, 0)
    grid_spec = pltpu.PrefetchScalarGridSpec(
        num_scalar_prefetch=1,
        grid=(bs, n_pages // gp),
        in_specs=[pl.BlockSpec((4, QK), lambda b, j, p: (0, 0)),
                  pl.BlockSpec((None, HEADS, VD), lambda b, j, p: (b, 0, 0))]
                 + [page(g) for g in range(gp)] + [page(g) for g in range(gp)]
                 + [pl.BlockSpec((None, 1, HEADS, VD), per_b),
                    pl.BlockSpec((None, 1, HEADS, VD), per_b),
                    pl.BlockSpec((1, VD), lambda b, j, p: (0, 0))],
        out_specs=pl.BlockSpec((None, HEADS, VD), lambda b, j, p: (b, 0, 0)),
        scratch_shapes=[pltpu.VMEM((2, HEADS, VD), F32), pltpu.VMEM((2, HEADS, VD), F32),
                        pltpu.VMEM((2, HEADS, VD), F32)],
    )
    out = pl.pallas_call(
        functools.partial(_sattn_body, gp=gp),
        grid_spec=grid_spec,
        out_shape=jax.ShapeDtypeStruct((bs, HEADS, VD), F32),
        compiler_params=_cp("parallel", "arbitrary"),
        name="attn_sample",
    )(pt, lam4, q, *([cache_k] * gp), *([cache_v] * gp), kn, vn, gsub)
    return out.reshape(bs, AW)


def _cmul(ar, ai, br, bi):
    return ar * br - ai * bi, ar * bi + ai * br


def _s5_prep_body(are_ref, aim_ref, ldt_ref, arer_ref, aimr_ref, ldtr_ref, bre_ref, bim_ref,
                  pow_ref, bbr_ref, bbi_ref):
    def disc(are, aim, ldt):
        dt = jnp.exp(ldt)
        mag = jnp.exp(are * dt)
        lr = mag * jnp.cos(aim * dt)
        li = mag * jnp.sin(aim * dt)
        return lr, li

    lr, li = disc(are_ref[...], aim_ref[...], ldt_ref[...])
    pr, pi = lr, li
    for n in range(8):
        pow_ref[2 * n] = pr
        pow_ref[2 * n + 1] = pi
        pr, pi = _cmul(pr, pi, lr, li)

    are, aim = arer_ref[...], aimr_ref[...]
    lrr, lir = disc(are, aim, ldtr_ref[...])
    den = are * are + aim * aim
    zr = lrr - 1.0
    fr = (zr * are + lir * aim) / den
    fi = (lir * are - zr * aim) / den
    br, bi = bre_ref[...], bim_ref[...]
    bbr_ref[...] = fr * br - fi * bi
    bbi_ref[...] = fr * bi + fi * br


def _s5_prep(a_re, a_im, log_dt, b_re, b_im):
    rep = lambda a: jnp.repeat(a, SSM_CH, axis=1)
    ldt = jnp.broadcast_to(log_dt[:, None], (SSM_G, SSM_P))
    pw, bbr, bbi = pl.pallas_call(
        _s5_prep_body,
        out_shape=[jax.ShapeDtypeStruct((16, SSM_G, SSM_P), F32),
                   jax.ShapeDtypeStruct((SSM_G, SSM_P * SSM_CH), F32),
                   jax.ShapeDtypeStruct((SSM_G, SSM_P * SSM_CH), F32)],
        name="s5_prep",
    )(a_re, a_im, ldt, rep(a_re), rep(a_im), rep(ldt),
      b_re.reshape(SSM_G, -1), b_im.reshape(SSM_G, -1))
    return pw, bbr.reshape(SSM_G, SSM_P, SSM_CH), bbi.reshape(SSM_G, SSM_P, SSM_CH)


def _s5_tables(pw, bbr, bbi, c_re, c_im, d):
    eye = jnp.eye(GB, dtype=F32)
    def blk_in(bb):
        x = bb.reshape(NB, GB, SSM_P, SSM_CH)
        return jnp.einsum('ngpc,gh->ngchp', x, eye).reshape(NB, UL, SL)
    bblk = jnp.concatenate([blk_in(bbr), blk_in(bbi)], axis=2).astype(BF16)
    def blk_out(cc):
        x = cc.reshape(NB, GB, SSM_CH, SSM_P)
        return jnp.einsum('ngcp,gh->ngphc', x, eye).reshape(NB, SL, UL)
    cblk = jnp.concatenate([blk_out(c_re), blk_out(-c_im)], axis=1).astype(BF16)
    lanes = lambda a: a.reshape(NB, 1, SL)
    t_idx = jnp.arange(8)[None, :, None]
    tabs = []
    for n in (1, 2, 4):
        keep = (t_idx >= n).astype(F32)
        tabs += [lanes(pw[2 * (n - 1)]) * keep, lanes(pw[2 * (n - 1) + 1]) * keep]
    pr = jnp.stack([pw[2 * n] for n in range(8)], axis=0).reshape(8, NB, SL).transpose(1, 0, 2)
    pi = jnp.stack([pw[2 * n + 1] for n in range(8)], axis=0).reshape(8, NB, SL).transpose(1, 0, 2)
    tabs += [pr, pi]
    tab = jnp.stack(tabs, axis=1)
    lam1 = jnp.stack([lanes(pw[0]), lanes(pw[1])], axis=1).reshape(NB, 2, SL)
    dvec = d.reshape(NB, 1, UL)
    return bblk, cblk, tab, lam1, dvec


def _s5_body(u_ref, bblk_ref, cblk_ref, d_ref, tab_ref, h0r_ref, h0i_ref,
             y_ref, hr_ref, hi_ref, st_scr, cr_scr, ci_scr, *, chunk):
    k = pl.program_id(2)

    @pl.when(k == 0)
    def _():
        cr_scr[...] = h0r_ref[...]
        ci_scr[...] = h0i_ref[...]

    u = u_ref[...]
    st_scr[...] = jnp.dot(u.astype(BF16), bblk_ref[...], preferred_element_type=F32)

    def tile(r, carry):
        cr, ci = carry
        rows = pl.ds(pl.multiple_of(r * 8, 8), 8)
        xr = st_scr[rows, 0:SL]
        xi = st_scr[rows, SL:2 * SL]
        for lvl, sh in enumerate((1, 2, 4)):
            ar, ai = tab_ref[2 * lvl], tab_ref[2 * lvl + 1]
            sr, si = pltpu.roll(xr, sh, 0), pltpu.roll(xi, sh, 0)
            xr, xi = xr + (ar * sr - ai * si), xi + (ar * si + ai * sr)
        pr, pi = tab_ref[6], tab_ref[7]
        hr = xr + (pr * cr - pi * ci)
        hi = xi + (pr * ci + pi * cr)
        st_scr[rows, 0:SL] = hr
        st_scr[rows, SL:2 * SL] = hi
        return hr[7:8, :], hi[7:8, :]

    cr, ci = lax.fori_loop(0, chunk // 8, tile, (cr_scr[...], ci_scr[...]))
    cr_scr[...] = cr
    ci_scr[...] = ci
    y_ref[...] = (jnp.dot(st_scr[...].astype(BF16), cblk_ref[...], preferred_element_type=F32)
                  + d_ref[...] * u)

    @pl.when(k == pl.num_programs(2) - 1)
    def _():
        hr_ref[...] = cr
        hi_ref[...] = ci


def _s5_prompt(u, bblk, cblk, dvec, tab, h0r, h0i, nb, t, chunk):
    nk = t // chunk
    st = jax.ShapeDtypeStruct((nb, 1, SSM_G * SSM_P), F32)
    y, hr, hi = pl.pallas_call(
        functools.partial(_s5_body, chunk=chunk),
        grid=(nb, NB, nk),
        in_specs=[pl.BlockSpec((chunk, UL), lambda b, n, k: (b * nk + k, n)),
                  pl.BlockSpec((None, UL, 2 * SL), lambda b, n, k: (n, 0, 0)),
                  pl.BlockSpec((None, 2 * SL, UL), lambda b, n, k: (n, 0, 0)),
                  pl.BlockSpec((None, 1, UL), lambda b, n, k: (n, 0, 0)),
                  pl.BlockSpec((None, 8, 8, SL), lambda b, n, k: (n, 0, 0, 0)),
                  pl.BlockSpec((None, 1, SL), lambda b, n, k: (b, 0, n)),
                  pl.BlockSpec((None, 1, SL), lambda b, n, k: (b, 0, n))],
        out_specs=[pl.BlockSpec((chunk, UL), lambda b, n, k: (b * nk + k, n)),
                   pl.BlockSpec((None, 1, SL), lambda b, n, k: (b, 0, n)),
                   pl.BlockSpec((None, 1, SL), lambda b, n, k: (b, 0, n))],
        out_shape=[jax.ShapeDtypeStruct((nb * t, SSM_W), F32), st, st],
        scratch_shapes=[pltpu.VMEM((chunk, 2 * SL), F32), pltpu.VMEM((1, SL), F32),
                        pltpu.VMEM((1, SL), F32)],
        compiler_params=_cp("parallel", "parallel", "arbitrary"),
        name="s5_prompt",
    )(u, bblk, cblk, dvec, tab, h0r.reshape(nb, 1, -1), h0i.reshape(nb, 1, -1))
    return y, hr.reshape(nb, SSM_G, SSM_P), hi.reshape(nb, SSM_G, SSM_P)


def _s5_step_body(u_ref, bblk_ref, cblk_ref, d_ref, lam_ref, h0r_ref, h0i_ref, y_ref, hr_ref, hi_ref):
    u = u_ref[...]
    bu = jnp.dot(u.astype(BF16), bblk_ref[...], preferred_element_type=F32)
    lr, li = lam_ref[0:1, :], lam_ref[1:2, :]
    h0r, h0i = h0r_ref[...], h0i_ref[...]
    hr = bu[:, 0:SL] + (lr * h0r - li * h0i)
    hi = bu[:, SL:2 * SL] + (lr * h0i + li * h0r)
    hr_ref[...] = hr
    hi_ref[...] = hi
    hcat = jnp.concatenate([hr, hi], axis=1).astype(BF16)
    y_ref[...] = jnp.dot(hcat, cblk_ref[...], preferred_element_type=F32) + d_ref[...] * u


def _s5_sample(u, bblk, cblk, dvec, lam1, h0r, h0i):
    rows = u.shape[0]
    st = jax.ShapeDtypeStruct((rows, SSM_G * SSM_P), F32)
    return pl.pallas_call(
        _s5_step_body,
        grid=(NB,),
        in_specs=[pl.BlockSpec((rows, UL), lambda n: (0, n)),
                  pl.BlockSpec((None, UL, 2 * SL), lambda n: (n, 0, 0)),
                  pl.BlockSpec((None, 2 * SL, UL), lambda n: (n, 0, 0)),
                  pl.BlockSpec((None, 1, UL), lambda n: (n, 0, 0)),
                  pl.BlockSpec((None, 2, SL), lambda n: (n, 0, 0)),
                  pl.BlockSpec((rows, SL), lambda n: (0, n)),
                  pl.BlockSpec((rows, SL), lambda n: (0, n))],
        out_specs=[pl.BlockSpec((rows, UL), lambda n: (0, n)),
                   pl.BlockSpec((rows, SL), lambda n: (0, n)),
                   pl.BlockSpec((rows, SL), lambda n: (0, n))],
        out_shape=[jax.ShapeDtypeStruct((rows, SSM_W), F32), st, st],
        compiler_params=_cp("parallel"),
        name="s5_sample",
    )(u, bblk, cblk, dvec, lam1, h0r, h0i)


def _outproj_body(x_ref, a_ref, y_ref, wg_ref, bg_ref, wo_ref, g_ref, x2_ref, h2_ref):
    y = jax.nn.gelu(y_ref[...])
    gate = jnp.dot(y.astype(BF16), wg_ref[...], preferred_element_type=F32) + bg_ref[...]
    y = y * jax.nn.sigmoid(gate)
    mix = (jnp.dot(a_ref[...].astype(BF16), wo_ref[0:AW, :], preferred_element_type=F32)
           + jnp.dot(y.astype(BF16), wo_ref[AW:, :], preferred_element_type=F32))
    x2 = x_ref[...] + mix
    x2_ref[...] = x2
    ms = jnp.mean(x2 * x2, axis=-1, keepdims=True)
    h2_ref[...] = (x2 * lax.rsqrt(ms + EPS) * g_ref[...]).astype(BF16)


def _outproj(x, attn, y, wg_bf, bg, wo_bf, g, tm):
    m = x.shape[0]
    row = lambda i: (i, 0)
    fix = lambda i: (0, 0)
    return pl.pallas_call(
        _outproj_body,
        grid=(m // tm,),
        in_specs=[pl.BlockSpec((tm, D_MODEL), row), pl.BlockSpec((tm, AW), row),
                  pl.BlockSpec((tm, SSM_W), row), pl.BlockSpec((SSM_W, SSM_W), fix),
                  pl.BlockSpec((1, SSM_W), fix), pl.BlockSpec((D_MODEL, D_MODEL), fix),
                  pl.BlockSpec((1, D_MODEL), fix)],
        out_specs=[pl.BlockSpec((tm, D_MODEL), row), pl.BlockSpec((tm, D_MODEL), row)],
        out_shape=[jax.ShapeDtypeStruct((m, D_MODEL), F32), jax.ShapeDtypeStruct((m, D_MODEL), BF16)],
        compiler_params=_cp("parallel"),
        name="outproj",
    )(x, attn, y, wg_bf, bg, wo_bf, g)


def _topk_body(h_ref, wq_ref, keys_ref, cnt_ref, w1_ref, rank_ref, e2_ref, q_scr, cand_scr, *, tb):
    q_scr[...] = jnp.dot(wq_ref[...], h_ref[...], preferred_element_type=F32).astype(BF16)
    ninf = -jnp.inf

    def head(h, carry):
        for lc in range(tb // 128):
            ls = slice(lc * 128, (lc + 1) * 128)
            sc = []
            for c in range(2):
                hc = 2 * h + c
                qrows = pl.ds(pl.multiple_of(hc * P_HALF, P_HALF), P_HALF)
                sc.append(jnp.dot(keys_ref[hc], q_scr[qrows, ls],
                                  preferred_element_type=F32))
            v1, v2 = [], []
            w = sc[0]
            for _ in range(P_TOPK):
                mx = jnp.max(w, axis=0, keepdims=True)
                v1.append(mx)
                w = jnp.where(w == mx, ninf, w)
            w = sc[1]
            rank = jnp.full(w.shape, float(P_TOPK), F32)
            for r in range(P_TOPK):
                mx = jnp.max(w, axis=0, keepdims=True)
                v2.append(mx)
                hit = w == mx
                rank = jnp.where(hit, float(r), rank)
                w = jnp.where(hit, ninf, w)
            cand_scr[...] = jnp.full(cand_scr.shape, ninf, F32)
            for n, (i, j) in enumerate(STAIR):
                cand_scr[n:n + 1, :] = v1[i] + v2[j]
            w = cand_scr[...]
            top = v1[0] + v2[0]
            z = jnp.zeros_like(top)
            tau = top
            for _ in range(P_TOPK):
                tau = jnp.max(w, axis=0, keepdims=True)
                z = z + jnp.exp(tau - top)
                w = jnp.where(w == tau, ninf, w)
            cnt = jnp.zeros(sc[0].shape, F32)
            for r in range(P_TOPK):
                cnt = cnt + jnp.where(sc[0] + v2[r] >= tau, 1.0, 0.0)
            rows = pl.ds(pl.multiple_of(h * P_NK, P_NK), P_NK)
            cnt_ref[rows, ls] = cnt
            w1_ref[rows, ls] = jnp.exp(sc[0] - v1[0]) / z
            rank_ref[rows, ls] = rank.astype(BF16)
            e2_ref[rows, ls] = jnp.exp(sc[1] - v2[0]).astype(BF16)
        return carry

    lax.fori_loop(0, P_HEADS, head, 0)


def _peer_topk(h2t, wqt_bf, keys_bf, tb):
    m = h2t.shape[1]
    f32 = jax.ShapeDtypeStruct((P_HEADS * P_NK, m), F32)
    bf16 = jax.ShapeDtypeStruct((P_HEADS * P_NK, m), BF16)
    col = lambda i: (0, i)
    return pl.pallas_call(
        functools.partial(_topk_body, tb=tb),
        grid=(m // tb,),
        in_specs=[pl.BlockSpec((D_MODEL, tb), col),
                  pl.BlockSpec((P_HEADS * 2 * P_HALF, D_MODEL), lambda i: (0, 0)),
                  pl.BlockSpec((P_HEADS * 2, P_NK, P_HALF), lambda i: (0, 0, 0))],
        out_specs=[pl.BlockSpec((P_HEADS * P_NK, tb), col)] * 4,
        out_shape=[f32, f32, bf16, bf16],
        scratch_shapes=[pltpu.VMEM((P_HEADS * 2 * P_HALF, tb), BF16), pltpu.VMEM((STAIR_ROWS, 128), F32)],
        compiler_params=_cp("parallel"),
        name="peer_topk",
    )(h2t, wqt_bf, keys_bf)


def _peer_body(h_ref, u_ref, vt_ref, cnt_ref, w1_ref, rank_ref, e2_ref, o_ref,
               a_scr, wm_scr, *, tb, na):
    step = pl.program_id(1)

    @pl.when(step == 0)
    def _():
        o_ref[...] = jnp.zeros(o_ref.shape, F32)

    a_scr[...] = jnp.dot(u_ref[...], h_ref[...], preferred_element_type=F32)

    def row_tile(ref, al, h, ls):
        packed = jnp.broadcast_to(ref[al, h:h + 1, ls], (16, 128)).astype(BF16)
        return jnp.tile(packed, (P_NK // 16, 1))

    def per_a(al, carry):
        arows = pl.ds(pl.multiple_of(al * P_NK, P_NK), P_NK)
        for lc in range(tb // 128):
            ls = slice(lc * 128, (lc + 1) * 128)
            g = jnp.zeros((P_NK, 128), BF16)
            for h in range(P_HEADS):
                rows = slice(h * P_NK, (h + 1) * P_NK)
                sel = rank_ref[rows, ls] < row_tile(cnt_ref, al, h, ls)
                g = g + jnp.where(sel, e2_ref[rows, ls] * row_tile(w1_ref, al, h, ls), jnp.zeros((), BF16))
            act = jax.nn.gelu(a_scr[arows, ls])
            wm_scr[arows, ls] = (act * g.astype(F32)).astype(BF16)
        return carry

    lax.fori_loop(0, na, per_a, 0)
    o_ref[...] += jnp.dot(vt_ref[...], wm_scr[...], preferred_element_type=F32)


def _peer_dense(h2t, u_bf, vt_bf, cnt, w1, rank, e2, tb, na):
    m = h2t.shape[1]
    col = lambda i, k: (0, i)
    return pl.pallas_call(
        functools.partial(_peer_body, tb=tb, na=na),
        grid=(m // tb, P_NK // na),
        in_specs=[pl.BlockSpec((D_MODEL, tb), col),
                  pl.BlockSpec((na * P_NK, D_MODEL), lambda i, k: (k, 0)),
                  pl.BlockSpec((D_MODEL, na * P_NK), lambda i, k: (0, k)),
                  pl.BlockSpec((na, P_HEADS, tb), lambda i, k: (k, 0, i)),
                  pl.BlockSpec((na, P_HEADS, tb), lambda i, k: (k, 0, i)),
                  pl.BlockSpec((P_HEADS * P_NK, tb), col),
                  pl.BlockSpec((P_HEADS * P_NK, tb), col)],
        out_specs=pl.BlockSpec((D_MODEL, tb), col),
        out_shape=jax.ShapeDtypeStruct((D_MODEL, m), F32),
        scratch_shapes=[pltpu.VMEM((na * P_NK, tb), F32), pltpu.VMEM((na * P_NK, tb), BF16)],
        compiler_params=_cp("parallel", "arbitrary"),
        name="peer_dense",
    )(h2t, u_bf, vt_bf, cnt, w1, rank, e2)


def _final_body(x_ref, p_ref, g_ref, y_ref):
    x = x_ref[...] + p_ref[...]
    ms = jnp.mean(x * x, axis=-1, keepdims=True)
    y_ref[...] = x * lax.rsqrt(ms + EPS) * g_ref[...]


def _final(x2, p, g, tm):
    m = x2.shape[0]
    row = lambda i: (i, 0)
    return pl.pallas_call(
        _final_body,
        grid=(m // tm,),
        in_specs=[pl.BlockSpec((tm, D_MODEL), row), pl.BlockSpec((tm, D_MODEL), row),
                  pl.BlockSpec((1, D_MODEL), lambda i: (0, 0))],
        out_specs=pl.BlockSpec((tm, D_MODEL), row),
        out_shape=jax.ShapeDtypeStruct((m, D_MODEL), F32),
        compiler_params=_cp("parallel"),
        name="final_norm",
    )(x2, p, g)


def _rope_tables(pos):
    half = QK // 2
    inv = 1.0 / (ROPE_THETA ** (jnp.arange(half, dtype=F32) * 2.0 / QK))
    ang = pos.astype(F32)[:, None] * inv[None, :]
    c, s = jnp.cos(ang), jnp.sin(ang)
    cos = jnp.concatenate([c, c, c, c], axis=1)
    sin = jnp.concatenate([-s, s, -s, s], axis=1)
    return cos, sin


def _peer(h2_bf, wqt_bf, keys_bf, u_bf, vt_bf, tb, na):
    h2t = h2_bf.T
    cnt, w1, rank, e2 = _peer_topk(h2t, wqt_bf, keys_bf, min(tb, 256))
    amajor = lambda a: a.reshape(P_HEADS, P_NK, -1).transpose(1, 0, 2)
    return _peer_dense(h2t, u_bf, vt_bf, amajor(cnt), amajor(w1), rank, e2, tb, na).T


def kernel(x_prompt, x_sample, cache_k, cache_v, state_ssm_re, state_ssm_im, page_table, g_mix, w_in, lambda_q1, lambda_k1, lambda_q2, lambda_k2, g_sub, ssm_a_re, ssm_a_im, ssm_log_dt, ssm_b_re, ssm_b_im, ssm_c_re, ssm_c_im, ssm_d, w_glu, b_glu, w_out, g_ffn, peer_w_q, peer_sub_keys, peer_u, peer_v, g_final):
    bp, tp, _ = x_prompt.shape
    bs, ts, _ = x_sample.shape
    assert ts == 1 and g_mix.shape[0] == 1
    n_pages = page_table.shape[1]
    past_len = n_pages * PAGE
    mp = bp * tp
    srows = 16

    w_in_bf = w_in[0].astype(BF16)
    wg_bf = w_glu[0].astype(BF16)
    wo_bf = w_out[0].astype(BF16)
    wqt_bf = peer_w_q[0].T.astype(BF16)
    keys_bf = peer_sub_keys[0].reshape(P_HEADS * 2, P_NK, P_HALF).astype(BF16)
    u_bf = peer_u[0].astype(BF16)
    vt_bf = peer_v[0].T.astype(BF16)
    gmix = g_mix[0].reshape(1, -1)
    gffn = g_ffn[0].reshape(1, -1)
    gfin = g_final.reshape(1, -1)
    gsub = g_sub[0].reshape(1, -1)
    bg = b_glu[0].reshape(1, -1)
    lam4 = jnp.stack([lambda_q1[0], lambda_k1[0], lambda_q2[0], lambda_k2[0]], axis=0)
    pw, bbr, bbi = _s5_prep(ssm_a_re[0], ssm_a_im[0], ssm_log_dt[0], ssm_b_re[0], ssm_b_im[0])
    bblk, cblk, tab, lam1, dvec = _s5_tables(pw, bbr, bbi, ssm_c_re[0], ssm_c_im[0], ssm_d[0])

    xp = x_prompt.reshape(mp, D_MODEL)
    cos_p, sin_p = _rope_tables(jnp.tile(jnp.arange(tp), bp))
    qb, kp, kpb, vp, vpb, up = _inproj(xp, gmix, w_in_bf, cos_p, sin_p, 512)
    attn_p = _attn_prompt(lam4, qb, kpb, vpb, gsub, bp, tp, 512, 128)
    zeros = jnp.zeros((bp, SSM_G * SSM_P), F32)
    yp, rp, ip = _s5_prompt(up, bblk, cblk, dvec, tab, zeros, zeros, bp, tp, 512)
    x2p, h2p = _outproj(xp, attn_p, yp, wg_bf, bg, wo_bf, gffn, 256)
    peer_p = _peer(h2p, wqt_bf, keys_bf, u_bf, vt_bf, 512, 8)
    y_prompt = _final(x2p, peer_p, gfin, 512).reshape(bp, tp, D_MODEL)

    xs = jnp.pad(x_sample.reshape(bs, D_MODEL), ((0, srows - bs), (0, 0)))
    cos_s, sin_s = _rope_tables(jnp.full((srows,), past_len))
    qs, ks, _, vs, _, us = _inproj(xs, gmix, w_in_bf, cos_s, sin_s, srows)
    heads = lambda a: a[:bs].astype(F32).reshape(bs, 1, HEADS, VD)
    attn_s = _attn_sample(page_table, lam4, heads(qs)[:, 0], cache_k, cache_v, heads(ks), heads(vs),
                          gsub, 4 if n_pages % 4 == 0 else 1)
    pad_s = lambda a: jnp.pad(a.reshape(bs, -1), ((0, srows - bs), (0, 0)))
    ys, rs, is_ = _s5_sample(us, bblk, cblk, dvec, lam1, pad_s(state_ssm_re[0]), pad_s(state_ssm_im[0]))
    x2s, h2s = _outproj(xs, pad_s(attn_s), ys, wg_bf, bg, wo_bf, gffn, srows)
    h2s_rep = jnp.tile(h2s[:bs], (128 // bs, 1))
    peer_s = _peer(h2s_rep, wqt_bf, keys_bf, u_bf, vt_bf, 128, 8)[:srows]
    y_sample = _final(x2s, peer_s, gfin, srows)[:bs].reshape(bs, ts, D_MODEL)

    return (y_prompt, y_sample,
            kp.reshape(1, bp, tp, HEADS, 2 * QK), vp.reshape(1, bp, tp, HEADS, VD),
            rp[None], ip[None],
            ks[:bs].reshape(1, bs, ts, HEADS, 2 * QK), vs[:bs].reshape(1, bs, ts, HEADS, VD),
            rs[:bs].reshape(1, bs, SSM_G, SSM_P), is_[:bs].reshape(1, bs, SSM_G, SSM_P))
```

```python
import functools
import math

import jax
import jax.numpy as jnp
from jax import lax
from jax.experimental import pallas as pl
from jax.experimental.pallas import tpu as pltpu

F32 = jnp.float32
BF16 = jnp.bfloat16

D_MODEL = 2048
PAGE = 128
HEADS = 8
QK = 64
VD = 128
AW = HEADS * VD
SSM_W = D_MODEL - AW
SSM_CH = 16
SSM_G = SSM_W // SSM_CH
SSM_P = 64
GB = 16
NB = SSM_G // GB
SL = GB * SSM_P
UL = GB * SSM_CH
P_HEADS = 8
P_TOPK = 16
P_NK = 128
P_HALF = 128
P_E = P_NK * P_NK
EPS = 1e-6
NEG = -1e30
ROPE_THETA = 10000.0
LAM_INIT = 0.8 - 0.6 * math.exp(-0.3 * 0)
VMEM_LIMIT = 56 * 1024 * 1024

STAIR = [(i, j) for i in range(P_TOPK) for j in range(P_TOPK) if (i + 1) * (j + 1) <= P_TOPK]
STAIR_ROWS = -(-len(STAIR) // 8) * 8


def _cp(*sem):
    return pltpu.CompilerParams(dimension_semantics=sem, vmem_limit_bytes=VMEM_LIMIT)


def _inproj_body(x_ref, g_ref, w_ref, cos_ref, sin_ref,
                 q_ref, k_ref, kb_ref, v_ref, vb_ref, u_ref, h_scr):
    j = pl.program_id(1)

    @pl.when(j == 0)
    def _():
        x = x_ref[...]
        ms = jnp.mean(x * x, axis=-1, keepdims=True)
        h_scr[...] = (x * lax.rsqrt(ms + EPS) * g_ref[...]).astype(BF16)

    z = jnp.dot(h_scr[...], w_ref[...], preferred_element_type=F32)

    def rope(zc):
        lane = lax.broadcasted_iota(jnp.int32, zc.shape, 1)
        first = (lane & (QK - 1)) < (QK // 2)
        partner = jnp.where(first, pltpu.roll(zc, 128 - QK // 2, 1), pltpu.roll(zc, QK // 2, 1))
        return zc * cos_ref[...] + partner * sin_ref[...]

    @pl.when(j == 0)
    def _():
        for c in range(AW // 128):
            sl = slice(c * 128, (c + 1) * 128)
            q_ref[:, sl] = (rope(z[:, sl]) * (QK ** -0.5)).astype(BF16)

    @pl.when(j == 1)
    def _():
        for c in range(AW // 128):
            sl = slice(c * 128, (c + 1) * 128)
            kc = rope(z[:, sl])
            k_ref[:, sl] = kc
            kb_ref[:, sl] = kc.astype(BF16)

    @pl.when(j == 2)
    def _():
        v_ref[...] = z
        vb_ref[...] = z.astype(BF16)

    @pl.when(j == 3)
    def _():
        u_ref[...] = z


def _inproj(x, g, w_bf, cos, sin, tm):
    m = x.shape[0]
    row = lambda i, j: (i, 0)
    outs = [jax.ShapeDtypeStruct((m, AW), BF16), jax.ShapeDtypeStruct((m, AW), F32),
            jax.ShapeDtypeStruct((m, AW), BF16), jax.ShapeDtypeStruct((m, AW), F32),
            jax.ShapeDtypeStruct((m, AW), BF16), jax.ShapeDtypeStruct((m, SSM_W), F32)]
    return pl.pallas_call(
        _inproj_body,
        grid=(m // tm, 4),
        in_specs=[pl.BlockSpec((tm, D_MODEL), row),
                  pl.BlockSpec((1, D_MODEL), lambda i, j: (0, 0)),
                  pl.BlockSpec((D_MODEL, AW), lambda i, j: (0, j)),
                  pl.BlockSpec((tm, 128), row),
                  pl.BlockSpec((tm, 128), row)],
        out_specs=[pl.BlockSpec((tm, AW), row) for _ in outs],
        out_shape=outs,
        scratch_shapes=[pltpu.VMEM((tm, D_MODEL), BF16)],
        compiler_params=_cp("parallel", "arbitrary"),
        name="inproj",
    )(x, g, w_bf, cos, sin)


def _lam(lam4_ref):
    l4 = lam4_ref[...]
    a = jnp.exp(jnp.sum(l4[0:1, :] * l4[1:2, :], axis=-1, keepdims=True))
    b = jnp.exp(jnp.sum(l4[2:3, :] * l4[3:4, :], axis=-1, keepdims=True))
    return a - b + LAM_INIT


def _subnorm(o, gsub):
    o = o * lax.rsqrt(jnp.mean(o * o, axis=-1, keepdims=True) + EPS)
    return o * gsub * (1.0 - LAM_INIT)


def _attn_body(lam4_ref, qt_ref, k_ref, vt_ref, gsub_ref, o_ref, acc_scr, *, tq, tk):
    qi = pl.program_id(2)
    unroll = tq // tk
    qt = qt_ref[...]
    row = lax.broadcasted_iota(jnp.int32, qt.shape, 0)
    zero = jnp.zeros_like(qt)
    qc = (jnp.where(row < QK, qt, zero), jnp.where(row >= QK, qt, zero))
    acc_scr[...] = jnp.zeros(acc_scr.shape, F32)
    rel = (lax.broadcasted_iota(jnp.int32, (tk, tq), 1) - lax.broadcasted_iota(jnp.int32, (tk, tq), 0))

    def group(gi, carry, masked):
        ms, ls = list(carry[0]), list(carry[1])
        blocks = [gi * unroll + u for u in range(unroll)]
        kbs = [k_ref[pl.ds(pl.multiple_of(b * tk, tk), tk), :] for b in blocks]
        sts = [[jnp.dot(kb, qc[c], preferred_element_type=F32) for c in range(2)] for kb in kbs]
        for u, b in enumerate(blocks):
            vtb = vt_ref[b]
            for c in range(2):
                st = sts[u][c]
                if masked:
                    st = jnp.where(rel >= u * tk, st, NEG)
                m_new = jnp.maximum(ms[c], jnp.max(st, axis=0, keepdims=True))
                alpha = jnp.exp(ms[c] - m_new)
                p = jnp.exp(st - m_new)
                ls[c] = alpha * ls[c] + jnp.sum(p, axis=0, keepdims=True)
                acc_scr[c] = alpha * acc_scr[c] + jnp.dot(vtb, p.astype(BF16), preferred_element_type=F32)
                ms[c] = m_new
        return tuple(ms), tuple(ls)

    m0 = jnp.full((1, tq), NEG, F32)
    l0 = jnp.zeros((1, tq), F32)
    carry = lax.fori_loop(0, qi, lambda g, c: group(g, c, False), ((m0, m0), (l0, l0)))
    _, ls = group(qi, carry, True)

    o = acc_scr[0] / ls[0] - _lam(lam4_ref) * (acc_scr[1] / ls[1])
    o = o * lax.rsqrt(jnp.mean(o * o, axis=0, keepdims=True) + EPS)
    o_ref[...] = (o * gsub_ref[...] * (1.0 - LAM_INIT)).T


def _attn_prompt(lam4, qb, kb, vb, gsub, nb, t, tq, tk):
    qt = qb.reshape(nb, t, AW).transpose(0, 2, 1)
    kb = kb.reshape(nb, t, AW)
    vt = vb.reshape(nb, t // tk, tk, AW).transpose(0, 1, 3, 2)
    gcol = jnp.broadcast_to(gsub.reshape(VD, 1), (VD, tq))
    out = pl.pallas_call(
        functools.partial(_attn_body, tq=tq, tk=tk),
        grid=(nb, HEADS, t // tq),
        in_specs=[pl.BlockSpec((4, QK), lambda b, h, i: (0, 0)),
                  pl.BlockSpec((None, VD, tq), lambda b, h, i: (b, h, i)),
                  pl.BlockSpec((None, t, VD), lambda b, h, i: (b, 0, h)),
                  pl.BlockSpec((None, t // tk, VD, tk), lambda b, h, i: (b, 0, h, 0)),
                  pl.BlockSpec((VD, tq), lambda b, h, i: (0, 0))],
        out_specs=pl.BlockSpec((None, tq, VD), lambda b, h, i: (b, i, h)),
        out_shape=jax.ShapeDtypeStruct((nb, t, AW), F32),
        scratch_shapes=[pltpu.VMEM((2, VD, tq), F32)],
        compiler_params=_cp("parallel", "parallel", "arbitrary"),
        name="attn_prompt",
    )(lam4, qt, kb, vt, gcol)
    return out.reshape(nb * t, AW)


def _sattn_body(pt_ref, lam4_ref, q_ref, *refs, gp):
    k_refs, v_refs = refs[:gp], refs[gp:2 * gp]
    kn_ref, vn_ref, gsub_ref, o_ref, m_scr, l_scr, acc_scr = refs[2 * gp:]
    j = pl.program_id(1)
    q = q_ref[...]
    first = lax.broadcasted_iota(jnp.int32, (HEADS, VD), 1) < QK
    qc = (jnp.where(first, q, 0.0), jnp.where(first, 0.0, q))

    @pl.when(j == 0)
    def _():
        m_scr[...] = jnp.full(m_scr.shape, NEG, F32)
        l_scr[...] = jnp.zeros(l_scr.shape, F32)
        acc_scr[...] = jnp.zeros(acc_scr.shape, F32)

    def update(kp, vp):
        for c in range(2):
            s = jnp.sum(kp * qc[c][None], axis=-1, keepdims=True)
            m_prev = m_scr[c]
            m_new = jnp.maximum(m_prev, jnp.max(s, axis=0))
            alpha = jnp.exp(m_prev - m_new)
            p = jnp.exp(s - m_new[None])
            l_scr[c] = alpha * l_scr[c] + jnp.sum(p, axis=0)
            acc_scr[c] = alpha * acc_scr[c] + jnp.sum(p * vp, axis=0)
            m_scr[c] = m_new

    kc = 32
    for g in range(gp):
        for i in range(PAGE // kc):
            update(k_refs[g][i * kc:(i + 1) * kc], v_refs[g][i * kc:(i + 1) * kc])

    @pl.when(j == pl.num_programs(1) - 1)
    def _():
        update(kn_ref[...], vn_ref[...])
        o = acc_scr[0] / l_scr[0] - _lam(lam4_ref) * (acc_scr[1] / l_scr[1])
        o_ref[...] = _subnorm(o, gsub_ref[...])


def _attn_sample(page_table, lam4, q, cache_k, cache_v, kn, vn, gsub, gp):
    bs, n_pages = page_table.shape
    pt = page_table.reshape(-1).astype(jnp.int32)

    def page(g):
        return pl.BlockSpec((None, None, PAGE, HEADS, VD),
                            lambda b, j, pt_ref: (0, pt_ref[b * n_pages + j * gp + g], 0, 0, 0))

    per_b = lambda b, j, p: (b, 0, 0, 0)
    grid_spec = pltpu.PrefetchScalarGridSpec(
        num_scalar_prefetch=1,
        grid=(bs, n_pages // gp),
        in_specs=[pl.BlockSpec((4, QK), lambda b, j, p: (0, 0)),
                  pl.BlockSpec((None, HEADS, VD), lambda b, j, p: (b, 0, 0))]
                 + [page(g) for g in range(gp)] + [page(g) for g in range(gp)]
                 + [pl.BlockSpec((None, 1, HEADS, VD), per_b),
                    pl.BlockSpec((None, 1, HEADS, VD), per_b),
                    pl.BlockSpec((1, VD), lambda b, j, p: (0, 0))],
        out_specs=pl.BlockSpec((None, HEADS, VD), lambda b, j, p: (b, 0, 0)),
        scratch_shapes=[pltpu.VMEM((2, HEADS, VD), F32), pltpu.VMEM((2, HEADS, VD), F32),
                        pltpu.VMEM((2, HEADS, VD), F32)],
    )
    out = pl.pallas_call(
        functools.partial(_sattn_body, gp=gp),
        grid_spec=grid_spec,
        out_shape=jax.ShapeDtypeStruct((bs, HEADS, VD), F32),
        compiler_params=_cp("parallel", "arbitrary"),
        name="attn_sample",
    )(pt, lam4, q, *([cache_k] * gp), *([cache_v] * gp), kn, vn, gsub)
    return out.reshape(bs, AW)


def _cmul(ar, ai, br, bi):
    return ar * br - ai * bi, ar * bi + ai * br


def _s5_prep_body(are_ref, aim_ref, ldt_ref, arer_ref, aimr_ref, ldtr_ref, bre_ref, bim_ref,
                  pow_ref, bbr_ref, bbi_ref):
    def disc(are, aim, ldt):
        dt = jnp.exp(ldt)
        mag = jnp.exp(are * dt)
        lr = mag * jnp.cos(aim * dt)
        li = mag * jnp.sin(aim * dt)
        return lr, li

    lr, li = disc(are_ref[...], aim_ref[...], ldt_ref[...])
    pr, pi = lr, li
    for n in range(8):
        pow_ref[2 * n] = pr
        pow_ref[2 * n + 1] = pi
        pr, pi = _cmul(pr, pi, lr, li)

    are, aim = arer_ref[...], aimr_ref[...]
    lrr, lir = disc(are, aim, ldtr_ref[...])
    den = are * are + aim * aim
    zr = lrr - 1.0
    fr = (zr * are + lir * aim) / den
    fi = (lir * are - zr * aim) / den
    br, bi = bre_ref[...], bim_ref[...]
    bbr_ref[...] = fr * br - fi * bi
    bbi_ref[...] = fr * bi + fi * br


def _s5_prep(a_re, a_im, log_dt, b_re, b_im):
    rep = lambda a: jnp.repeat(a, SSM_CH, axis=1)
    ldt = jnp.broadcast_to(log_dt[:, None], (SSM_G, SSM_P))
    pw, bbr, bbi = pl.pallas_call(
        _s5_prep_body,
        out_shape=[jax.ShapeDtypeStruct((16, SSM_G, SSM_P), F32),
                   jax.ShapeDtypeStruct((SSM_G, SSM_P * SSM_CH), F32),
                   jax.ShapeDtypeStruct((SSM_G, SSM_P * SSM_CH), F32)],
        name="s5_prep",
    )(a_re, a_im, ldt, rep(a_re), rep(a_im), rep(ldt),
      b_re.reshape(SSM_G, -1), b_im.reshape(SSM_G, -1))
    return pw, bbr.reshape(SSM_G, SSM_P, SSM_CH), bbi.reshape(SSM_G, SSM_P, SSM_CH)


def _s5_tables(pw, bbr, bbi, c_re, c_im, d):
    eye = jnp.eye(GB, dtype=F32)
    def blk_in(bb):
        x = bb.reshape(NB, GB, SSM_P, SSM_CH)
        return jnp.einsum('ngpc,gh->ngchp', x, eye).reshape(NB, UL, SL)
    bblk = jnp.concatenate([blk_in(bbr), blk_in(bbi)], axis=2).astype(BF16)
    def blk_out(cc):
        x = cc.reshape(NB, GB, SSM_CH, SSM_P)
        return jnp.einsum('ngcp,gh->ngphc', x, eye).reshape(NB, SL, UL)
    cblk = jnp.concatenate([blk_out(c_re), blk_out(-c_im)], axis=1).astype(BF16)
    lanes = lambda a: a.reshape(NB, 1, SL)
    t_idx = jnp.arange(8)[None, :, None]
    tabs = []
    for n in (1, 2, 4):
        keep = (t_idx >= n).astype(F32)
        tabs += [lanes(pw[2 * (n - 1)]) * keep, lanes(pw[2 * (n - 1) + 1]) * keep]
    pr = jnp.stack([pw[2 * n] for n in range(8)], axis=0).reshape(8, NB, SL).transpose(1, 0, 2)
    pi = jnp.stack([pw[2 * n + 1] for n in range(8)], axis=0).reshape(8, NB, SL).transpose(1, 0, 2)
    tabs += [pr, pi]
    tab = jnp.stack(tabs, axis=1)
    lam1 = jnp.stack([lanes(pw[0]), lanes(pw[1])], axis=1).reshape(NB, 2, SL)
    dvec = d.reshape(NB, 1, UL)
    return bblk, cblk, tab, lam1, dvec


def _s5_body(u_ref, bblk_ref, cblk_ref, d_ref, tab_ref, h0r_ref, h0i_ref,
             y_ref, hr_ref, hi_ref, st_scr, cr_scr, ci_scr, *, chunk):
    k = pl.program_id(2)

    @pl.when(k == 0)
    def _():
        cr_scr[...] = h0r_ref[...]
        ci_scr[...] = h0i_ref[...]

    u = u_ref[...]
    st_scr[...] = jnp.dot(u.astype(BF16), bblk_ref[...], preferred_element_type=F32)

    def tile(r, carry):
        cr, ci = carry
        rows = pl.ds(pl.multiple_of(r * 8, 8), 8)
        xr = st_scr[rows, 0:SL]
        xi = st_scr[rows, SL:2 * SL]
        for lvl, sh in enumerate((1, 2, 4)):
            ar, ai = tab_ref[2 * lvl], tab_ref[2 * lvl + 1]
            sr, si = pltpu.roll(xr, sh, 0), pltpu.roll(xi, sh, 0)
            xr, xi = xr + (ar * sr - ai * si), xi + (ar * si + ai * sr)
        pr, pi = tab_ref[6], tab_ref[7]
        hr = xr + (pr * cr - pi * ci)
        hi = xi + (pr * ci + pi * cr)
        st_scr[rows, 0:SL] = hr
        st_scr[rows, SL:2 * SL] = hi
        return hr[7:8, :], hi[7:8, :]

    cr, ci = lax.fori_loop(0, chunk // 8, tile, (cr_scr[...], ci_scr[...]))
    cr_scr[...] = cr
    ci_scr[...] = ci
    y_ref[...] = (jnp.dot(st_scr[...].astype(BF16), cblk_ref[...], preferred_element_type=F32)
                  + d_ref[...] * u)

    @pl.when(k == pl.num_programs(2) - 1)
    def _():
        hr_ref[...] = cr
        hi_ref[...] = ci


def _s5_prompt(u, bblk, cblk, dvec, tab, h0r, h0i, nb, t, chunk):
    nk = t // chunk
    st = jax.ShapeDtypeStruct((nb, 1, SSM_G * SSM_P), F32)
    y, hr, hi = pl.pallas_call(
        functools.partial(_s5_body, chunk=chunk),
        grid=(nb, NB, nk),
        in_specs=[pl.BlockSpec((chunk, UL), lambda b, n, k: (b * nk + k, n)),
                  pl.BlockSpec((None, UL, 2 * SL), lambda b, n, k: (n, 0, 0)),
                  pl.BlockSpec((None, 2 * SL, UL), lambda b, n, k: (n, 0, 0)),
                  pl.BlockSpec((None, 1, UL), lambda b, n, k: (n, 0, 0)),
                  pl.BlockSpec((None, 8, 8, SL), lambda b, n, k: (n, 0, 0, 0)),
                  pl.BlockSpec((None, 1, SL), lambda b, n, k: (b, 0, n)),
                  pl.BlockSpec((None, 1, SL), lambda b, n, k: (b, 0, n))],
        out_specs=[pl.BlockSpec((chunk, UL), lambda b, n, k: (b * nk + k, n)),
                   pl.BlockSpec((None, 1, SL), lambda b, n, k: (b, 0, n)),
                   pl.BlockSpec((None, 1, SL), lambda b, n, k: (b, 0, n))],
        out_shape=[jax.ShapeDtypeStruct((nb * t, SSM_W), F32), st, st],
        scratch_shapes=[pltpu.VMEM((chunk, 2 * SL), F32), pltpu.VMEM((1, SL), F32),
                        pltpu.VMEM((1, SL), F32)],
        compiler_params=_cp("parallel", "parallel", "arbitrary"),
        name="s5_prompt",
    )(u, bblk, cblk, dvec, tab, h0r.reshape(nb, 1, -1), h0i.reshape(nb, 1, -1))
    return y, hr.reshape(nb, SSM_G, SSM_P), hi.reshape(nb, SSM_G, SSM_P)


def _s5_step_body(u_ref, bblk_ref, cblk_ref, d_ref, lam_ref, h0r_ref, h0i_ref, y_ref, hr_ref, hi_ref):
    u = u_ref[...]
    bu = jnp.dot(u.astype(BF16), bblk_ref[...], preferred_element_type=F32)
    lr, li = lam_ref[0:1, :], lam_ref[1:2, :]
    h0r, h0i = h0r_ref[...], h0i_ref[...]
    hr = bu[:, 0:SL] + (lr * h0r - li * h0i)
    hi = bu[:, SL:2 * SL] + (lr * h0i + li * h0r)
    hr_ref[...] = hr
    hi_ref[...] = hi
    hcat = jnp.concatenate([hr, hi], axis=1).astype(BF16)
    y_ref[...] = jnp.dot(hcat, cblk_ref[...], preferred_element_type=F32) + d_ref[...] * u


def _s5_sample(u, bblk, cblk, dvec, lam1, h0r, h0i):
    rows = u.shape[0]
    st = jax.ShapeDtypeStruct((rows, SSM_G * SSM_P), F32)
    return pl.pallas_call(
        _s5_step_body,
        grid=(NB,),
        in_specs=[pl.BlockSpec((rows, UL), lambda n: (0, n)),
                  pl.BlockSpec((None, UL, 2 * SL), lambda n: (n, 0, 0)),
                  pl.BlockSpec((None, 2 * SL, UL), lambda n: (n, 0, 0)),
                  pl.BlockSpec((None, 1, UL), lambda n: (n, 0, 0)),
                  pl.BlockSpec((None, 2, SL), lambda n: (n, 0, 0)),
                  pl.BlockSpec((rows, SL), lambda n: (0, n)),
                  pl.BlockSpec((rows, SL), lambda n: (0, n))],
        out_specs=[pl.BlockSpec((rows, UL), lambda n: (0, n)),
                   pl.BlockSpec((rows, SL), lambda n: (0, n)),
                   pl.BlockSpec((rows, SL), lambda n: (0, n))],
        out_shape=[jax.ShapeDtypeStruct((rows, SSM_W), F32), st, st],
        compiler_params=_cp("parallel"),
        name="s5_sample",
    )(u, bblk, cblk, dvec, lam1, h0r, h0i)


def _outproj_body(x_ref, a_ref, y_ref, wg_ref, bg_ref, wo_ref, g_ref, x2_ref, h2_ref, *, transposed):
    y = jax.nn.gelu(y_ref[...])
    gate = jnp.dot(y.astype(BF16), wg_ref[...], preferred_element_type=F32) + bg_ref[...]
    y = y * jax.nn.sigmoid(gate)
    mix = (jnp.dot(a_ref[...].astype(BF16), wo_ref[0:AW, :], preferred_element_type=F32)
           + jnp.dot(y.astype(BF16), wo_ref[AW:, :], preferred_element_type=F32))
    x2 = x_ref[...] + mix
    x2_ref[...] = x2
    ms = jnp.mean(x2 * x2, axis=-1, keepdims=True)
    h2 = x2 * lax.rsqrt(ms + EPS) * g_ref[...]
    h2_ref[...] = (h2.T if transposed else h2).astype(BF16)


def _outproj(x, attn, y, wg_bf, bg, wo_bf, g, tm, transposed):
    m = x.shape[0]
    row = lambda i: (i, 0)
    fix = lambda i: (0, 0)
    if transposed:
        h2_spec, h2_shape = pl.BlockSpec((D_MODEL, tm), lambda i: (0, i)), (D_MODEL, m)
    else:
        h2_spec, h2_shape = pl.BlockSpec((tm, D_MODEL), row), (m, D_MODEL)
    return pl.pallas_call(
        functools.partial(_outproj_body, transposed=transposed),
        grid=(m // tm,),
        in_specs=[pl.BlockSpec((tm, D_MODEL), row), pl.BlockSpec((tm, AW), row),
                  pl.BlockSpec((tm, SSM_W), row), pl.BlockSpec((SSM_W, SSM_W), fix),
                  pl.BlockSpec((1, SSM_W), fix), pl.BlockSpec((D_MODEL, D_MODEL), fix),
                  pl.BlockSpec((1, D_MODEL), fix)],
        out_specs=[pl.BlockSpec((tm, D_MODEL), row), h2_spec],
        out_shape=[jax.ShapeDtypeStruct((m, D_MODEL), F32), jax.ShapeDtypeStruct(h2_shape, BF16)],
        compiler_params=_cp("parallel"),
        name="outproj",
    )(x, attn, y, wg_bf, bg, wo_bf, g)


def _topk_body(h_ref, wq_ref, keys_ref, cnt_ref, w1_ref, rank_ref, e2_ref, q_scr, cand_scr, *, tb):
    q_scr[...] = jnp.dot(wq_ref[...], h_ref[...], preferred_element_type=F32).astype(BF16)
    ninf = -jnp.inf

    def head(h, carry):
        for lc in range(tb // 128):
            ls = slice(lc * 128, (lc + 1) * 128)
            sc = []
            for c in range(2):
                hc = 2 * h + c
                qrows = pl.ds(pl.multiple_of(hc * P_HALF, P_HALF), P_HALF)
                sc.append(jnp.dot(keys_ref[hc], q_scr[qrows, ls],
                                  preferred_element_type=F32))
            v1, v2 = [], []
            w = sc[0]
            for _ in range(P_TOPK):
                mx = jnp.max(w, axis=0, keepdims=True)
                v1.append(mx)
                w = jnp.where(w == mx, ninf, w)
            w = sc[1]
            rank = jnp.full(w.shape, float(P_TOPK), F32)
            for r in range(P_TOPK):
                mx = jnp.max(w, axis=0, keepdims=True)
                v2.append(mx)
                hit = w == mx
                rank = jnp.where(hit, float(r), rank)
                w = jnp.where(hit, ninf, w)
            cand_scr[...] = jnp.full(cand_scr.shape, ninf, F32)
            for n, (i, j) in enumerate(STAIR):
                cand_scr[n:n + 1, :] = v1[i] + v2[j]
            w = cand_scr[...]
            top = v1[0] + v2[0]
            z = jnp.zeros_like(top)
            tau = top
            for _ in range(P_TOPK):
                tau = jnp.max(w, axis=0, keepdims=True)
                z = z + jnp.exp(tau - top)
                w = jnp.where(w == tau, ninf, w)
            cnt = jnp.zeros(sc[0].shape, F32)
            for r in range(P_TOPK):
                cnt = cnt + jnp.where(sc[0] + v2[r] >= tau, 1.0, 0.0)
            rows = pl.ds(pl.multiple_of(h * P_NK, P_NK), P_NK)
            cnt_ref[rows, ls] = cnt
            w1_ref[rows, ls] = jnp.exp(sc[0] - v1[0]) / z
            rank_ref[rows, ls] = rank.astype(BF16)
            e2_ref[rows, ls] = jnp.exp(sc[1] - v2[0]).astype(BF16)
        return carry

    lax.fori_loop(0, P_HEADS, head, 0)


def _peer_topk(h2t, wqt_bf, keys_bf, tb):
    m = h2t.shape[1]
    f32 = jax.ShapeDtypeStruct((P_HEADS * P_NK, m), F32)
    bf16 = jax.ShapeDtypeStruct((P_HEADS * P_NK, m), BF16)
    col = lambda i: (0, i)
    return pl.pallas_call(
        functools.partial(_topk_body, tb=tb),
        grid=(m // tb,),
        in_specs=[pl.BlockSpec((D_MODEL, tb), col),
                  pl.BlockSpec((P_HEADS * 2 * P_HALF, D_MODEL), lambda i: (0, 0)),
                  pl.BlockSpec((P_HEADS * 2, P_NK, P_HALF), lambda i: (0, 0, 0))],
        out_specs=[pl.BlockSpec((P_HEADS * P_NK, tb), col)] * 4,
        out_shape=[f32, f32, bf16, bf16],
        scratch_shapes=[pltpu.VMEM((P_HEADS * 2 * P_HALF, tb), BF16), pltpu.VMEM((STAIR_ROWS, 128), F32)],
        compiler_params=_cp("parallel"),
        name="peer_topk",
    )(h2t, wqt_bf, keys_bf)


def _peer_body(h_ref, u_ref, vt_ref, cnt_ref, w1_ref, rank_ref, e2_ref, o_ref,
               a_scr, wm_scr, *, tb, na):
    step = pl.program_id(1)

    @pl.when(step == 0)
    def _():
        o_ref[...] = jnp.zeros(o_ref.shape, F32)

    a_scr[...] = jnp.dot(u_ref[...], h_ref[...], preferred_element_type=F32)

    def row_tile(ref, al, h, ls):
        packed = jnp.broadcast_to(ref[h, al:al + 1, ls], (16, 128)).astype(BF16)
        return jnp.tile(packed, (P_NK // 16, 1))

    for al in range(na):
        arows = slice(al * P_NK, (al + 1) * P_NK)
        for lc in range(tb // 128):
            ls = slice(lc * 128, (lc + 1) * 128)
            g = jnp.zeros((P_NK, 128), BF16)
            for h in range(P_HEADS):
                rows = slice(h * P_NK, (h + 1) * P_NK)
                sel = rank_ref[rows, ls] < row_tile(cnt_ref, al, h, ls)
                g = g + jnp.where(sel, e2_ref[rows, ls] * row_tile(w1_ref, al, h, ls), jnp.zeros((), BF16))
            act = jax.nn.gelu(a_scr[arows, ls])
            wm_scr[arows, ls] = (act * g.astype(F32)).astype(BF16)
    o_ref[...] += jnp.dot(vt_ref[...], wm_scr[...], preferred_element_type=F32)


def _peer_dense(h2t, u_bf, vt_bf, cnt, w1, rank, e2, tb, na):
    m = h2t.shape[1]
    col = lambda i, k: (0, i)
    return pl.pallas_call(
        functools.partial(_peer_body, tb=tb, na=na),
        grid=(m // tb, P_NK // na),
        in_specs=[pl.BlockSpec((D_MODEL, tb), col),
                  pl.BlockSpec((na * P_NK, D_MODEL), lambda i, k: (k, 0)),
                  pl.BlockSpec((D_MODEL, na * P_NK), lambda i, k: (0, k)),
                  pl.BlockSpec((P_HEADS, na, tb), lambda i, k: (0, k, i)),
                  pl.BlockSpec((P_HEADS, na, tb), lambda i, k: (0, k, i)),
                  pl.BlockSpec((P_HEADS * P_NK, tb), col),
                  pl.BlockSpec((P_HEADS * P_NK, tb), col)],
        out_specs=pl.BlockSpec((D_MODEL, tb), col),
        out_shape=jax.ShapeDtypeStruct((D_MODEL, m), F32),
        scratch_shapes=[pltpu.VMEM((na * P_NK, tb), F32), pltpu.VMEM((na * P_NK, tb), BF16)],
        compiler_params=_cp("parallel", "arbitrary"),
        name="peer_dense",
    )(h2t, u_bf, vt_bf, cnt, w1, rank, e2)


def _final_body(x_ref, pt_ref, g_ref, y_ref):
    tm = x_ref.shape[0]
    x = x_ref[...] + pt_ref[...].T[:tm]
    ms = jnp.mean(x * x, axis=-1, keepdims=True)
    y_ref[...] = x * lax.rsqrt(ms + EPS) * g_ref[...]


def _final(x2, pt, g, tm):
    m = x2.shape[0]
    row = lambda i: (i, 0)
    return pl.pallas_call(
        _final_body,
        grid=(m // tm,),
        in_specs=[pl.BlockSpec((tm, D_MODEL), row),
                  pl.BlockSpec((D_MODEL, max(tm, 128)), lambda i: (0, i)),
                  pl.BlockSpec((1, D_MODEL), lambda i: (0, 0))],
        out_specs=pl.BlockSpec((tm, D_MODEL), row),
        out_shape=jax.ShapeDtypeStruct((m, D_MODEL), F32),
        compiler_params=_cp("parallel"),
        name="final_norm",
    )(x2, pt, g)


def _rope_tables(pos):
    half = QK // 2
    inv = 1.0 / (ROPE_THETA ** (jnp.arange(half, dtype=F32) * 2.0 / QK))
    ang = pos.astype(F32)[:, None] * inv[None, :]
    c, s = jnp.cos(ang), jnp.sin(ang)
    cos = jnp.concatenate([c, c, c, c], axis=1)
    sin = jnp.concatenate([-s, s, -s, s], axis=1)
    return cos, sin


def _peer(h2t, wqt_bf, keys_bf, u_bf, vt_bf, tb, na):
    cnt, w1, rank, e2 = _peer_topk(h2t, wqt_bf, keys_bf, min(tb, 256))
    per_head = lambda a: a.reshape(P_HEADS, P_NK, -1)
    return _peer_dense(h2t, u_bf, vt_bf, per_head(cnt), per_head(w1), rank, e2, tb, na)


def kernel(x_prompt, x_sample, cache_k, cache_v, state_ssm_re, state_ssm_im, page_table, g_mix, w_in, lambda_q1, lambda_k1, lambda_q2, lambda_k2, g_sub, ssm_a_re, ssm_a_im, ssm_log_dt, ssm_b_re, ssm_b_im, ssm_c_re, ssm_c_im, ssm_d, w_glu, b_glu, w_out, g_ffn, peer_w_q, peer_sub_keys, peer_u, peer_v, g_final):
    bp, tp, _ = x_prompt.shape
    bs, ts, _ = x_sample.shape
    assert ts == 1 and g_mix.shape[0] == 1
    n_pages = page_table.shape[1]
    past_len = n_pages * PAGE
    mp = bp * tp
    srows = 16

    w_in_bf = w_in[0].astype(BF16)
    wg_bf = w_glu[0].astype(BF16)
    wo_bf = w_out[0].astype(BF16)
    wqt_bf = peer_w_q[0].T.astype(BF16)
    keys_bf = peer_sub_keys[0].reshape(P_HEADS * 2, P_NK, P_HALF).astype(BF16)
    u_bf = peer_u[0].astype(BF16)
    vt_bf = peer_v[0].T.astype(BF16)
    gmix = g_mix[0].reshape(1, -1)
    gffn = g_ffn[0].reshape(1, -1)
    gfin = g_final.reshape(1, -1)
    gsub = g_sub[0].reshape(1, -1)
    bg = b_glu[0].reshape(1, -1)
    lam4 = jnp.stack([lambda_q1[0], lambda_k1[0], lambda_q2[0], lambda_k2[0]], axis=0)
    pw, bbr, bbi = _s5_prep(ssm_a_re[0], ssm_a_im[0], ssm_log_dt[0], ssm_b_re[0], ssm_b_im[0])
    bblk, cblk, tab, lam1, dvec = _s5_tables(pw, bbr, bbi, ssm_c_re[0], ssm_c_im[0], ssm_d[0])

    xp = x_prompt.reshape(mp, D_MODEL)
    cos_p, sin_p = _rope_tables(jnp.tile(jnp.arange(tp), bp))
    qb, kp, kpb, vp, vpb, up = _inproj(xp, gmix, w_in_bf, cos_p, sin_p, 512)
    attn_p = _attn_prompt(lam4, qb, kpb, vpb, gsub, bp, tp, 512, 128)
    zeros = jnp.zeros((bp, SSM_G * SSM_P), F32)
    yp, rp, ip = _s5_prompt(up, bblk, cblk, dvec, tab, zeros, zeros, bp, tp, 512)
    x2p, h2pt = _outproj(xp, attn_p, yp, wg_bf, bg, wo_bf, gffn, 256, True)
    peer_p = _peer(h2pt, wqt_bf, keys_bf, u_bf, vt_bf, 512, 8)
    y_prompt = _final(x2p, peer_p, gfin, 512).reshape(bp, tp, D_MODEL)

    xs = jnp.pad(x_sample.reshape(bs, D_MODEL), ((0, srows - bs), (0, 0)))
    cos_s, sin_s = _rope_tables(jnp.full((srows,), past_len))
    qs, ks, _, vs, _, us = _inproj(xs, gmix, w_in_bf, cos_s, sin_s, srows)
    heads = lambda a: a[:bs].astype(F32).reshape(bs, 1, HEADS, VD)
    attn_s = _attn_sample(page_table, lam4, heads(qs)[:, 0], cache_k, cache_v, heads(ks), heads(vs),
                          gsub, 4 if n_pages % 4 == 0 else 1)
    pad_s = lambda a: jnp.pad(a.reshape(bs, -1), ((0, srows - bs), (0, 0)))
    ys, rs, is_ = _s5_sample(us, bblk, cblk, dvec, lam1, pad_s(state_ssm_re[0]), pad_s(state_ssm_im[0]))
    x2s, h2s = _outproj(xs, pad_s(attn_s), ys, wg_bf, bg, wo_bf, gffn, srows, False)
    h2s_rep = jnp.tile(h2s[:bs], (128 // bs, 1))
    peer_s = _peer(h2s_rep.T, wqt_bf, keys_bf, u_bf, vt_bf, 128, 8)
    y_sample = _final(x2s, peer_s, gfin, srows)[:bs].reshape(bs, ts, D_MODEL)

    return (y_prompt, y_sample,
            kp.reshape(1, bp, tp, HEADS, 2 * QK), vp.reshape(1, bp, tp, HEADS, VD),
            rp[None], ip[None],
            ks[:bs].reshape(1, bs, ts, HEADS, 2 * QK), vs[:bs].reshape(1, bs, ts, HEADS, VD),
            rs[:bs].reshape(1, bs, SSM_G, SSM_P), is_[:bs].reshape(1, bs, SSM_G, SSM_P))
```

```python
import functools
import math

import jax
import jax.numpy as jnp
from jax import lax
from jax.experimental import pallas as pl
from jax.experimental.pallas import tpu as pltpu

F32 = jnp.float32
BF16 = jnp.bfloat16

D_MODEL = 2048
PAGE = 128
HEADS = 8
QK = 64
VD = 128
AW = HEADS * VD
SSM_W = D_MODEL - AW
SSM_CH = 16
SSM_G = SSM_W // SSM_CH
SSM_P = 64
GB = 16
NB = SSM_G // GB
SL = GB * SSM_P
UL = GB * SSM_CH
P_HEADS = 8
P_TOPK = 16
P_NK = 128
P_HALF = 128
P_E = P_NK * P_NK
EPS = 1e-6
NEG = -1e30
ROPE_THETA = 10000.0
LAM_INIT = 0.8 - 0.6 * math.exp(-0.3 * 0)
VMEM_LIMIT = 56 * 1024 * 1024

STAIR = [(i, j) for i in range(P_TOPK) for j in range(P_TOPK) if (i + 1) * (j + 1) <= P_TOPK]
STAIR_ROWS = -(-len(STAIR) // 8) * 8


def _cp(*sem):
    return pltpu.CompilerParams(dimension_semantics=sem, vmem_limit_bytes=VMEM_LIMIT)


def _inproj_body(x_ref, g_ref, w_ref, cos_ref, sin_ref,
                 q_ref, k_ref, kb_ref, v_ref, vb_ref, u_ref, h_scr):
    j = pl.program_id(1)

    @pl.when(j == 0)
    def _():
        x = x_ref[...]
        ms = jnp.mean(x * x, axis=-1, keepdims=True)
        h_scr[...] = (x * lax.rsqrt(ms + EPS) * g_ref[...]).astype(BF16)

    z = jnp.dot(h_scr[...], w_ref[...], preferred_element_type=F32)

    def rope(zc):
        lane = lax.broadcasted_iota(jnp.int32, zc.shape, 1)
        first = (lane & (QK - 1)) < (QK // 2)
        partner = jnp.where(first, pltpu.roll(zc, 128 - QK // 2, 1), pltpu.roll(zc, QK // 2, 1))
        return zc * cos_ref[...] + partner * sin_ref[...]

    @pl.when(j == 0)
    def _():
        for c in range(AW // 128):
            sl = slice(c * 128, (c + 1) * 128)
            q_ref[:, sl] = (rope(z[:, sl]) * (QK ** -0.5)).astype(BF16)

    @pl.when(j == 1)
    def _():
        for c in range(AW // 128):
            sl = slice(c * 128, (c + 1) * 128)
            kc = rope(z[:, sl])
            k_ref[:, sl] = kc
            kb_ref[:, sl] = kc.astype(BF16)

    @pl.when(j == 2)
    def _():
        v_ref[...] = z
        vb_ref[...] = z.astype(BF16)

    @pl.when(j == 3)
    def _():
        u_ref[...] = z


def _inproj(x, g, w_bf, cos, sin, tm):
    m = x.shape[0]
    row = lambda i, j: (i, 0)
    outs = [jax.ShapeDtypeStruct((m, AW), BF16), jax.ShapeDtypeStruct((m, AW), F32),
            jax.ShapeDtypeStruct((m, AW), BF16), jax.ShapeDtypeStruct((m, AW), F32),
            jax.ShapeDtypeStruct((m, AW), BF16), jax.ShapeDtypeStruct((m, SSM_W), F32)]
    return pl.pallas_call(
        _inproj_body,
        grid=(m // tm, 4),
        in_specs=[pl.BlockSpec((tm, D_MODEL), row),
                  pl.BlockSpec((1, D_MODEL), lambda i, j: (0, 0)),
                  pl.BlockSpec((D_MODEL, AW), lambda i, j: (0, j)),
                  pl.BlockSpec((tm, 128), row),
                  pl.BlockSpec((tm, 128), row)],
        out_specs=[pl.BlockSpec((tm, AW), row) for _ in outs],
        out_shape=outs,
        scratch_shapes=[pltpu.VMEM((tm, D_MODEL), BF16)],
        compiler_params=_cp("parallel", "arbitrary"),
        name="inproj",
    )(x, g, w_bf, cos, sin)


def _lam(lam4_ref):
    l4 = lam4_ref[...]
    a = jnp.exp(jnp.sum(l4[0:1, :] * l4[1:2, :], axis=-1, keepdims=True))
    b = jnp.exp(jnp.sum(l4[2:3, :] * l4[3:4, :], axis=-1, keepdims=True))
    return a - b + LAM_INIT


def _subnorm(o, gsub):
    o = o * lax.rsqrt(jnp.mean(o * o, axis=-1, keepdims=True) + EPS)
    return o * gsub * (1.0 - LAM_INIT)


def _attn_body(lam4_ref, qt_ref, k_ref, vt_ref, gsub_ref, o_ref, acc_scr, *, tq, tk):
    qi = pl.program_id(2)
    unroll = tq // tk
    qt = qt_ref[...]
    row = lax.broadcasted_iota(jnp.int32, qt.shape, 0)
    zero = jnp.zeros_like(qt)
    qc = (jnp.where(row < QK, qt, zero), jnp.where(row >= QK, qt, zero))
    acc_scr[...] = jnp.zeros(acc_scr.shape, F32)
    rel = (lax.broadcasted_iota(jnp.int32, (tk, tq), 1) - lax.broadcasted_iota(jnp.int32, (tk, tq), 0))

    def group(gi, carry, masked):
        ms, ls = list(carry[0]), list(carry[1])
        blocks = [gi * unroll + u for u in range(unroll)]
        kbs = [k_ref[pl.ds(pl.multiple_of(b * tk, tk), tk), :] for b in blocks]
        sts = [[jnp.dot(kb, qc[c], preferred_element_type=F32) for c in range(2)] for kb in kbs]
        for u, b in enumerate(blocks):
            vtb = vt_ref[b]
            for c in range(2):
                st = sts[u][c]
                if masked:
                    st = jnp.where(rel >= u * tk, st, NEG)
                m_new = jnp.maximum(ms[c], jnp.max(st, axis=0, keepdims=True))
                alpha = jnp.exp(ms[c] - m_new)
                p = jnp.exp(st - m_new)
                ls[c] = alpha * ls[c] + jnp.sum(p, axis=0, keepdims=True)
                acc_scr[c] = alpha * acc_scr[c] + jnp.dot(vtb, p.astype(BF16), preferred_element_type=F32)
                ms[c] = m_new
        return tuple(ms), tuple(ls)

    m0 = jnp.full((1, tq), NEG, F32)
    l0 = jnp.zeros((1, tq), F32)
    carry = lax.fori_loop(0, qi, lambda g, c: group(g, c, False), ((m0, m0), (l0, l0)))
    _, ls = group(qi, carry, True)

    o = acc_scr[0] / ls[0] - _lam(lam4_ref) * (acc_scr[1] / ls[1])
    o = o * lax.rsqrt(jnp.mean(o * o, axis=0, keepdims=True) + EPS)
    o_ref[...] = (o * gsub_ref[...] * (1.0 - LAM_INIT)).T


def _attn_prompt(lam4, qb, kb, vb, gsub, nb, t, tq, tk):
    qt = qb.reshape(nb, t, AW).transpose(0, 2, 1)
    kb = kb.reshape(nb, t, AW)
    vt = vb.reshape(nb, t // tk, tk, AW).transpose(0, 1, 3, 2)
    gcol = jnp.broadcast_to(gsub.reshape(VD, 1), (VD, tq))
    out = pl.pallas_call(
        functools.partial(_attn_body, tq=tq, tk=tk),
        grid=(nb, HEADS, t // tq),
        in_specs=[pl.BlockSpec((4, QK), lambda b, h, i: (0, 0)),
                  pl.BlockSpec((None, VD, tq), lambda b, h, i: (b, h, i)),
                  pl.BlockSpec((None, t, VD), lambda b, h, i: (b, 0, h)),
                  pl.BlockSpec((None, t // tk, VD, tk), lambda b, h, i: (b, 0, h, 0)),
                  pl.BlockSpec((VD, tq), lambda b, h, i: (0, 0))],
        out_specs=pl.BlockSpec((None, tq, VD), lambda b, h, i: (b, i, h)),
        out_shape=jax.ShapeDtypeStruct((nb, t, AW), F32),
        scratch_shapes=[pltpu.VMEM((2, VD, tq), F32)],
        compiler_params=_cp("parallel", "parallel", "arbitrary"),
        name="attn_prompt",
    )(lam4, qt, kb, vt, gcol)
    return out.reshape(nb * t, AW)


def _sattn_body(pt_ref, lam4_ref, q_ref, *refs, gp):
    k_refs, v_refs = refs[:gp], refs[gp:2 * gp]
    kn_ref, vn_ref, gsub_ref, o_ref, m_scr, l_scr, acc_scr = refs[2 * gp:]
    j = pl.program_id(1)
    q = q_ref[...]
    first = lax.broadcasted_iota(jnp.int32, (HEADS, VD), 1) < QK
    qc = (jnp.where(first, q, 0.0), jnp.where(first, 0.0, q))

    @pl.when(j == 0)
    def _():
        m_scr[...] = jnp.full(m_scr.shape, NEG, F32)
        l_scr[...] = jnp.zeros(l_scr.shape, F32)
        acc_scr[...] = jnp.zeros(acc_scr.shape, F32)

    def update(kp, vp):
        for c in range(2):
            s = jnp.sum(kp * qc[c][None], axis=-1, keepdims=True)
            m_prev = m_scr[c]
            m_new = jnp.maximum(m_prev, jnp.max(s, axis=0))
            alpha = jnp.exp(m_prev - m_new)
            p = jnp.exp(s - m_new[None])
            l_scr[c] = alpha * l_scr[c] + jnp.sum(p, axis=0)
            acc_scr[c] = alpha * acc_scr[c] + jnp.sum(p * vp, axis=0)
            m_scr[c] = m_new

    kc = 32
    for g in range(gp):
        for i in range(PAGE // kc):
            update(k_refs[g][i * kc:(i + 1) * kc], v_refs[g][i * kc:(i + 1) * kc])

    @pl.when(j == pl.num_programs(1) - 1)
    def _():
        update(kn_ref[...], vn_ref[...])
        o = acc_scr[0] / l_scr[0] - _lam(lam4_ref) * (acc_scr[1] / l_scr[1])
        o_ref[...] = _subnorm(o, gsub_ref[...])


def _attn_sample(page_table, lam4, q, cache_k, cache_v, kn, vn, gsub, gp):
    bs, n_pages = page_table.shape
    pt = page_table.reshape(-1).astype(jnp.int32)

    def page(g):
        return pl.BlockSpec((None, None, PAGE, HEADS, VD),
                            lambda b, j, pt_ref: (0, pt_ref[b * n_pages + j * gp + g], 0, 0, 0))

    per_b = lambda b, j, p: (b, 0, 0, 0)
    grid_spec = pltpu.PrefetchScalarGridSpec(
        num_scalar_prefetch=1,
        grid=(bs, n_pages // gp),
        in_specs=[pl.BlockSpec((4, QK), lambda b, j, p: (0, 0)),
                  pl.BlockSpec((None, HEADS, VD), lambda b, j, p: (b, 0, 0))]
                 + [page(g) for g in range(gp)] + [page(g) for g in range(gp)]
                 + [pl.BlockSpec((None, 1, HEADS, VD), per_b),
                    pl.BlockSpec((None, 1, HEADS, VD), per_b),
                    pl.BlockSpec((1, VD), lambda b, j, p: (0, 0))],
        out_specs=pl.BlockSpec((None, HEADS, VD), lambda b, j, p: (b, 0, 0)),
        scratch_shapes=[pltpu.VMEM((2, HEADS, VD), F32), pltpu.VMEM((2, HEADS, VD), F32),
                        pltpu.VMEM((2, HEADS, VD), F32)],
    )
    out = pl.pallas_call(
        functools.partial(_sattn_body, gp=gp),
        grid_spec=grid_spec,
        out_shape=jax.ShapeDtypeStruct((bs, HEADS, VD), F32),
        compiler_params=_cp("parallel", "arbitrary"),
        name="attn_sample",
    )(pt, lam4, q, *([cache_k] * gp), *([cache_v] * gp), kn, vn, gsub)
    return out.reshape(bs, AW)


def _cmul(ar, ai, br, bi):
    return ar * br - ai * bi, ar * bi + ai * br


def _s5_prep_body(are_ref, aim_ref, ldt_ref, arer_ref, aimr_ref, ldtr_ref, bre_ref, bim_ref,
                  pow_ref, bbr_ref, bbi_ref):
    def disc(are, aim, ldt):
        dt = jnp.exp(ldt)
        mag = jnp.exp(are * dt)
        lr = mag * jnp.cos(aim * dt)
        li = mag * jnp.sin(aim * dt)
        return lr, li

    lr, li = disc(are_ref[...], aim_ref[...], ldt_ref[...])
    pr, pi = lr, li
    for n in range(8):
        pow_ref[2 * n] = pr
        pow_ref[2 * n + 1] = pi
        pr, pi = _cmul(pr, pi, lr, li)

    are, aim = arer_ref[...], aimr_ref[...]
    lrr, lir = disc(are, aim, ldtr_ref[...])
    den = are * are + aim * aim
    zr = lrr - 1.0
    fr = (zr * are + lir * aim) / den
    fi = (lir * are - zr * aim) / den
    br, bi = bre_ref[...], bim_ref[...]
    bbr_ref[...] = fr * br - fi * bi
    bbi_ref[...] = fr * bi + fi * br


def _s5_prep(a_re, a_im, log_dt, b_re, b_im):
    rep = lambda a: jnp.repeat(a, SSM_CH, axis=1)
    ldt = jnp.broadcast_to(log_dt[:, None], (SSM_G, SSM_P))
    pw, bbr, bbi = pl.pallas_call(
        _s5_prep_body,
        out_shape=[jax.ShapeDtypeStruct((16, SSM_G, SSM_P), F32),
                   jax.ShapeDtypeStruct((SSM_G, SSM_P * SSM_CH), F32),
                   jax.ShapeDtypeStruct((SSM_G, SSM_P * SSM_CH), F32)],
        name="s5_prep",
    )(a_re, a_im, ldt, rep(a_re), rep(a_im), rep(ldt),
      b_re.reshape(SSM_G, -1), b_im.reshape(SSM_G, -1))
    return pw, bbr.reshape(SSM_G, SSM_P, SSM_CH), bbi.reshape(SSM_G, SSM_P, SSM_CH)


def _s5_tables(pw, bbr, bbi, c_re, c_im, d):
    eye = jnp.eye(GB, dtype=F32)
    def blk_in(bb):
        x = bb.reshape(NB, GB, SSM_P, SSM_CH)
        return jnp.einsum('ngpc,gh->ngchp', x, eye).reshape(NB, UL, SL)
    bblk = jnp.concatenate([blk_in(bbr), blk_in(bbi)], axis=2).astype(BF16)
    def blk_out(cc):
        x = cc.reshape(NB, GB, SSM_CH, SSM_P)
        return jnp.einsum('ngcp,gh->ngphc', x, eye).reshape(NB, SL, UL)
    cblk = jnp.concatenate([blk_out(c_re), blk_out(-c_im)], axis=1).astype(BF16)
    lanes = lambda a: a.reshape(NB, 1, SL)
    t_idx = jnp.arange(8)[None, :, None]
    tabs = []
    for n in (1, 2, 4):
        keep = (t_idx >= n).astype(F32)
        tabs += [lanes(pw[2 * (n - 1)]) * keep, lanes(pw[2 * (n - 1) + 1]) * keep]
    pr = jnp.stack([pw[2 * n] for n in range(8)], axis=0).reshape(8, NB, SL).transpose(1, 0, 2)
    pi = jnp.stack([pw[2 * n + 1] for n in range(8)], axis=0).reshape(8, NB, SL).transpose(1, 0, 2)
    tabs += [pr, pi]
    tab = jnp.stack(tabs, axis=1)
    lam1 = jnp.stack([lanes(pw[0]), lanes(pw[1])], axis=1).reshape(NB, 2, SL)
    dvec = d.reshape(NB, 1, UL)
    return bblk, cblk, tab, lam1, dvec


def _s5_body(u_ref, bblk_ref, cblk_ref, d_ref, tab_ref, h0r_ref, h0i_ref,
             y_ref, hr_ref, hi_ref, st_scr, cr_scr, ci_scr, *, chunk):
    k = pl.program_id(2)

    @pl.when(k == 0)
    def _():
        cr_scr[...] = h0r_ref[...]
        ci_scr[...] = h0i_ref[...]

    u = u_ref[...]
    st_scr[...] = jnp.dot(u.astype(BF16), bblk_ref[...], preferred_element_type=F32)

    def tile(r, carry):
        cr, ci = carry
        rows = pl.ds(pl.multiple_of(r * 8, 8), 8)
        xr = st_scr[rows, 0:SL]
        xi = st_scr[rows, SL:2 * SL]
        for lvl, sh in enumerate((1, 2, 4)):
            ar, ai = tab_ref[2 * lvl], tab_ref[2 * lvl + 1]
            sr, si = pltpu.roll(xr, sh, 0), pltpu.roll(xi, sh, 0)
            xr, xi = xr + (ar * sr - ai * si), xi + (ar * si + ai * sr)
        pr, pi = tab_ref[6], tab_ref[7]
        hr = xr + (pr * cr - pi * ci)
        hi = xi + (pr * ci + pi * cr)
        st_scr[rows, 0:SL] = hr
        st_scr[rows, SL:2 * SL] = hi
        return hr[7:8, :], hi[7:8, :]

    cr, ci = lax.fori_loop(0, chunk // 8, tile, (cr_scr[...], ci_scr[...]))
    cr_scr[...] = cr
    ci_scr[...] = ci
    y_ref[...] = (jnp.dot(st_scr[...].astype(BF16), cblk_ref[...], preferred_element_type=F32)
                  + d_ref[...] * u)

    @pl.when(k == pl.num_programs(2) - 1)
    def _():
        hr_ref[...] = cr
        hi_ref[...] = ci


def _s5_prompt(u, bblk, cblk, dvec, tab, h0r, h0i, nb, t, chunk):
    nk = t // chunk
    st = jax.ShapeDtypeStruct((nb, 1, SSM_G * SSM_P), F32)
    y, hr, hi = pl.pallas_call(
        functools.partial(_s5_body, chunk=chunk),
        grid=(nb, NB, nk),
        in_specs=[pl.BlockSpec((chunk, UL), lambda b, n, k: (b * nk + k, n)),
                  pl.BlockSpec((None, UL, 2 * SL), lambda b, n, k: (n, 0, 0)),
                  pl.BlockSpec((None, 2 * SL, UL), lambda b, n, k: (n, 0, 0)),
                  pl.BlockSpec((None, 1, UL), lambda b, n, k: (n, 0, 0)),
                  pl.BlockSpec((None, 8, 8, SL), lambda b, n, k: (n, 0, 0, 0)),
                  pl.BlockSpec((None, 1, SL), lambda b, n, k: (b, 0, n)),
                  pl.BlockSpec((None, 1, SL), lambda b, n, k: (b, 0, n))],
        out_specs=[pl.BlockSpec((chunk, UL), lambda b, n, k: (b * nk + k, n)),
                   pl.BlockSpec((None, 1, SL), lambda b, n, k: (b, 0, n)),
                   pl.BlockSpec((None, 1, SL), lambda b, n, k: (b, 0, n))],
        out_shape=[jax.ShapeDtypeStruct((nb * t, SSM_W), F32), st, st],
        scratch_shapes=[pltpu.VMEM((chunk, 2 * SL), F32), pltpu.VMEM((1, SL), F32),
                        pltpu.VMEM((1, SL), F32)],
        compiler_params=_cp("parallel", "parallel", "arbitrary"),
        name="s5_prompt",
    )(u, bblk, cblk, dvec, tab, h0r.reshape(nb, 1, -1), h0i.reshape(nb, 1, -1))
    return y, hr.reshape(nb, SSM_G, SSM_P), hi.reshape(nb, SSM_G, SSM_P)


def _s5_step_body(u_ref, bblk_ref, cblk_ref, d_ref, lam_ref, h0r_ref, h0i_ref, y_ref, hr_ref, hi_ref):
    u = u_ref[...]
    bu = jnp.dot(u.astype(BF16), bblk_ref[...], preferred_element_type=F32)
    lr, li = lam_ref[0:1, :], lam_ref[1:2, :]
    h0r, h0i = h0r_ref[...], h0i_ref[...]
    hr = bu[:, 0:SL] + (lr * h0r - li * h0i)
    hi = bu[:, SL:2 * SL] + (lr * h0i + li * h0r)
    hr_ref[...] = hr
    hi_ref[...] = hi
    hcat = jnp.concatenate([hr, hi], axis=1).astype(BF16)
    y_ref[...] = jnp.dot(hcat, cblk_ref[...], preferred_element_type=F32) + d_ref[...] * u


def _s5_sample(u, bblk, cblk, dvec, lam1, h0r, h0i):
    rows = u.shape[0]
    st = jax.ShapeDtypeStruct((rows, SSM_G * SSM_P), F32)
    return pl.pallas_call(
        _s5_step_body,
        grid=(NB,),
        in_specs=[pl.BlockSpec((rows, UL), lambda n: (0, n)),
                  pl.BlockSpec((None, UL, 2 * SL), lambda n: (n, 0, 0)),
                  pl.BlockSpec((None, 2 * SL, UL), lambda n: (n, 0, 0)),
                  pl.BlockSpec((None, 1, UL), lambda n: (n, 0, 0)),
                  pl.BlockSpec((None, 2, SL), lambda n: (n, 0, 0)),
                  pl.BlockSpec((rows, SL), lambda n: (0, n)),
                  pl.BlockSpec((rows, SL), lambda n: (0, n))],
        out_specs=[pl.BlockSpec((rows, UL), lambda n: (0, n)),
                   pl.BlockSpec((rows, SL), lambda n: (0, n)),
                   pl.BlockSpec((rows, SL), lambda n: (0, n))],
        out_shape=[jax.ShapeDtypeStruct((rows, SSM_W), F32), st, st],
        compiler_params=_cp("parallel"),
        name="s5_sample",
    )(u, bblk, cblk, dvec, lam1, h0r, h0i)


def _outproj_body(x_ref, a_ref, y_ref, wg_ref, bg_ref, wo_ref, g_ref, x2_ref, h2_ref, *, transposed):
    y = jax.nn.gelu(y_ref[...])
    gate = jnp.dot(y.astype(BF16), wg_ref[...], preferred_element_type=F32) + bg_ref[...]
    y = y * jax.nn.sigmoid(gate)
    mix = (jnp.dot(a_ref[...].astype(BF16), wo_ref[0:AW, :], preferred_element_type=F32)
           + jnp.dot(y.astype(BF16), wo_ref[AW:, :], preferred_element_type=F32))
    x2 = x_ref[...] + mix
    x2_ref[...] = x2
    ms = jnp.mean(x2 * x2, axis=-1, keepdims=True)
    h2 = x2 * lax.rsqrt(ms + EPS) * g_ref[...]
    h2_ref[...] = (h2.T if transposed else h2).astype(BF16)


def _outproj(x, attn, y, wg_bf, bg, wo_bf, g, tm, transposed):
    m = x.shape[0]
    row = lambda i: (i, 0)
    fix = lambda i: (0, 0)
    if transposed:
        h2_spec, h2_shape = pl.BlockSpec((D_MODEL, tm), lambda i: (0, i)), (D_MODEL, m)
    else:
        h2_spec, h2_shape = pl.BlockSpec((tm, D_MODEL), row), (m, D_MODEL)
    return pl.pallas_call(
        functools.partial(_outproj_body, transposed=transposed),
        grid=(m // tm,),
        in_specs=[pl.BlockSpec((tm, D_MODEL), row), pl.BlockSpec((tm, AW), row),
                  pl.BlockSpec((tm, SSM_W), row), pl.BlockSpec((SSM_W, SSM_W), fix),
                  pl.BlockSpec((1, SSM_W), fix), pl.BlockSpec((D_MODEL, D_MODEL), fix),
                  pl.BlockSpec((1, D_MODEL), fix)],
        out_specs=[pl.BlockSpec((tm, D_MODEL), row), h2_spec],
        out_shape=[jax.ShapeDtypeStruct((m, D_MODEL), F32), jax.ShapeDtypeStruct(h2_shape, BF16)],
        compiler_params=_cp("parallel"),
        name="outproj",
    )(x, attn, y, wg_bf, bg, wo_bf, g)


def _topk_body(h_ref, wq_ref, keys_ref, cnt_ref, w1_ref, rank_ref, e2_ref, q_scr, cand_scr, *, tb):
    q_scr[...] = jnp.dot(wq_ref[...], h_ref[...], preferred_element_type=F32).astype(BF16)
    ninf = -jnp.inf

    def head(h, carry):
        for lc in range(tb // 128):
            ls = slice(lc * 128, (lc + 1) * 128)
            sc = []
            for c in range(2):
                hc = 2 * h + c
                qrows = pl.ds(pl.multiple_of(hc * P_HALF, P_HALF), P_HALF)
                sc.append(jnp.dot(keys_ref[hc], q_scr[qrows, ls],
                                  preferred_element_type=F32))
            v1, v2 = [], []
            w = sc[0]
            for _ in range(P_TOPK):
                mx = jnp.max(w, axis=0, keepdims=True)
                v1.append(mx)
                w = jnp.where(w == mx, ninf, w)
            w = sc[1]
            rank = jnp.full(w.shape, float(P_TOPK), F32)
            for r in range(P_TOPK):
                mx = jnp.max(w, axis=0, keepdims=True)
                v2.append(mx)
                hit = w == mx
                rank = jnp.where(hit, float(r), rank)
                w = jnp.where(hit, ninf, w)
            cand_scr[...] = jnp.full(cand_scr.shape, ninf, F32)
            for n, (i, j) in enumerate(STAIR):
                cand_scr[n:n + 1, :] = v1[i] + v2[j]
            w = cand_scr[...]
            top = v1[0] + v2[0]
            z = jnp.zeros_like(top)
            tau = top
            for _ in range(P_TOPK):
                tau = jnp.max(w, axis=0, keepdims=True)
                z = z + jnp.exp(tau - top)
                w = jnp.where(w == tau, ninf, w)
            cnt = jnp.zeros(sc[0].shape, F32)
            for r in range(P_TOPK):
                cnt = cnt + jnp.where(sc[0] + v2[r] >= tau, 1.0, 0.0)
            rows = pl.ds(pl.multiple_of(h * P_NK, P_NK), P_NK)
            cnt_ref[rows, ls] = cnt
            w1_ref[rows, ls] = jnp.exp(sc[0] - v1[0]) / z
            rank_ref[rows, ls] = rank
            e2_ref[rows, ls] = jnp.exp(sc[1] - v2[0])
        return carry

    lax.fori_loop(0, P_HEADS, head, 0)


def _peer_topk(h2t, wqt_bf, keys_bf, tb):
    m = h2t.shape[1]
    f32 = jax.ShapeDtypeStruct((P_HEADS * P_NK, m), F32)
    bf16 = jax.ShapeDtypeStruct((P_HEADS * P_NK, m), BF16)
    col = lambda i: (0, i)
    return pl.pallas_call(
        functools.partial(_topk_body, tb=tb),
        grid=(m // tb,),
        in_specs=[pl.BlockSpec((D_MODEL, tb), col),
                  pl.BlockSpec((P_HEADS * 2 * P_HALF, D_MODEL), lambda i: (0, 0)),
                  pl.BlockSpec((P_HEADS * 2, P_NK, P_HALF), lambda i: (0, 0, 0))],
        out_specs=[pl.BlockSpec((P_HEADS * P_NK, tb), col)] * 4,
        out_shape=[f32, f32, f32, f32],
        scratch_shapes=[pltpu.VMEM((P_HEADS * 2 * P_HALF, tb), BF16), pltpu.VMEM((STAIR_ROWS, 128), F32)],
        compiler_params=_cp("parallel"),
        name="peer_topk",
    )(h2t, wqt_bf, keys_bf)


def _peer_body(h_ref, u_ref, vt_ref, cnt_ref, w1_ref, rank_in, e2_in, o_ref,
               a_scr, wm_scr, rank_ref, e2_ref, *, tb, na):
    step = pl.program_id(1)

    @pl.when(step == 0)
    def _():
        o_ref[...] = jnp.zeros(o_ref.shape, F32)
        rank_ref[...] = rank_in[...].astype(BF16)
        e2_ref[...] = e2_in[...].astype(BF16)

    a_scr[...] = jnp.dot(u_ref[...], h_ref[...], preferred_element_type=F32)

    def row_tile(ref, al, h, ls):
        packed = jnp.broadcast_to(ref[h, al:al + 1, ls], (16, 128)).astype(BF16)
        return jnp.tile(packed, (P_NK // 16, 1))

    for al in range(na):
        arows = slice(al * P_NK, (al + 1) * P_NK)
        for lc in range(tb // 128):
            ls = slice(lc * 128, (lc + 1) * 128)
            g = jnp.zeros((P_NK, 128), BF16)
            for h in range(P_HEADS):
                rows = slice(h * P_NK, (h + 1) * P_NK)
                sel = rank_ref[rows, ls] < row_tile(cnt_ref, al, h, ls)
                g = g + jnp.where(sel, e2_ref[rows, ls] * row_tile(w1_ref, al, h, ls), jnp.zeros((), BF16))
            act = jax.nn.gelu(a_scr[arows, ls])
            wm_scr[arows, ls] = (act * g.astype(F32)).astype(BF16)
    o_ref[...] += jnp.dot(vt_ref[...], wm_scr[...], preferred_element_type=F32)


def _peer_dense(h2t, u_bf, vt_bf, cnt, w1, rank, e2, tb, na):
    m = h2t.shape[1]
    col = lambda i, k: (0, i)
    return pl.pallas_call(
        functools.partial(_peer_body, tb=tb, na=na),
        grid=(m // tb, P_NK // na),
        in_specs=[pl.BlockSpec((D_MODEL, tb), col),
                  pl.BlockSpec((na * P_NK, D_MODEL), lambda i, k: (k, 0)),
                  pl.BlockSpec((D_MODEL, na * P_NK), lambda i, k: (0, k)),
                  pl.BlockSpec((P_HEADS, na, tb), lambda i, k: (0, k, i)),
                  pl.BlockSpec((P_HEADS, na, tb), lambda i, k: (0, k, i)),
                  pl.BlockSpec((P_HEADS * P_NK, tb), col),
                  pl.BlockSpec((P_HEADS * P_NK, tb), col)],
        out_specs=pl.BlockSpec((D_MODEL, tb), col),
        out_shape=jax.ShapeDtypeStruct((D_MODEL, m), F32),
        scratch_shapes=[pltpu.VMEM((na * P_NK, tb), F32), pltpu.VMEM((na * P_NK, tb), BF16),
                        pltpu.VMEM((P_HEADS * P_NK, tb), BF16), pltpu.VMEM((P_HEADS * P_NK, tb), BF16)],
        compiler_params=_cp("parallel", "arbitrary"),
        name="peer_dense",
    )(h2t, u_bf, vt_bf, cnt, w1, rank, e2)


def _final_body(x_ref, pt_ref, g_ref, y_ref):
    tm = x_ref.shape[0]
    x = x_ref[...] + pt_ref[...].T[:tm]
    ms = jnp.mean(x * x, axis=-1, keepdims=True)
    y_ref[...] = x * lax.rsqrt(ms + EPS) * g_ref[...]


def _final(x2, pt, g, tm):
    m = x2.shape[0]
    row = lambda i: (i, 0)
    return pl.pallas_call(
        _final_body,
        grid=(m // tm,),
        in_specs=[pl.BlockSpec((tm, D_MODEL), row),
                  pl.BlockSpec((D_MODEL, max(tm, 128)), lambda i: (0, i)),
                  pl.BlockSpec((1, D_MODEL), lambda i: (0, 0))],
        out_specs=pl.BlockSpec((tm, D_MODEL), row),
        out_shape=jax.ShapeDtypeStruct((m, D_MODEL), F32),
        compiler_params=_cp("parallel"),
        name="final_norm",
    )(x2, pt, g)


def _rope_tables(pos):
    half = QK // 2
    inv = 1.0 / (ROPE_THETA ** (jnp.arange(half, dtype=F32) * 2.0 / QK))
    ang = pos.astype(F32)[:, None] * inv[None, :]
    c, s = jnp.cos(ang), jnp.sin(ang)
    cos = jnp.concatenate([c, c, c, c], axis=1)
    sin = jnp.concatenate([-s, s, -s, s], axis=1)
    return cos, sin


def _peer(h2t, wqt_bf, keys_bf, u_bf, vt_bf, tb, na):
    cnt, w1, rank, e2 = _peer_topk(h2t, wqt_bf, keys_bf, min(tb, 256))
    per_head = lambda a: a.reshape(P_HEADS, P_NK, -1)
    return _peer_dense(h2t, u_bf, vt_bf, per_head(cnt), per_head(w1), rank, e2, tb, na)


def kernel(x_prompt, x_sample, cache_k, cache_v, state_ssm_re, state_ssm_im, page_table, g_mix, w_in, lambda_q1, lambda_k1, lambda_q2, lambda_k2, g_sub, ssm_a_re, ssm_a_im, ssm_log_dt, ssm_b_re, ssm_b_im, ssm_c_re, ssm_c_im, ssm_d, w_glu, b_glu, w_out, g_ffn, peer_w_q, peer_sub_keys, peer_u, peer_v, g_final):
    bp, tp, _ = x_prompt.shape
    bs, ts, _ = x_sample.shape
    assert ts == 1 and g_mix.shape[0] == 1
    n_pages = page_table.shape[1]
    past_len = n_pages * PAGE
    mp = bp * tp
    srows = 16

    w_in_bf = w_in[0].astype(BF16)
    wg_bf = w_glu[0].astype(BF16)
    wo_bf = w_out[0].astype(BF16)
    wqt_bf = peer_w_q[0].T.astype(BF16)
    keys_bf = peer_sub_keys[0].reshape(P_HEADS * 2, P_NK, P_HALF).astype(BF16)
    u_bf = peer_u[0].astype(BF16)
    vt_bf = peer_v[0].T.astype(BF16)
    gmix = g_mix[0].reshape(1, -1)
    gffn = g_ffn[0].reshape(1, -1)
    gfin = g_final.reshape(1, -1)
    gsub = g_sub[0].reshape(1, -1)
    bg = b_glu[0].reshape(1, -1)
    lam4 = jnp.stack([lambda_q1[0], lambda_k1[0], lambda_q2[0], lambda_k2[0]], axis=0)
    pw, bbr, bbi = _s5_prep(ssm_a_re[0], ssm_a_im[0], ssm_log_dt[0], ssm_b_re[0], ssm_b_im[0])
    bblk, cblk, tab, lam1, dvec = _s5_tables(pw, bbr, bbi, ssm_c_re[0], ssm_c_im[0], ssm_d[0])

    xp = x_prompt.reshape(mp, D_MODEL)
    cos_p, sin_p = _rope_tables(jnp.tile(jnp.arange(tp), bp))
    qb, kp, kpb, vp, vpb, up = _inproj(xp, gmix, w_in_bf, cos_p, sin_p, 512)
    attn_p = _attn_prompt(lam4, qb, kpb, vpb, gsub, bp, tp, 512, 128)
    zeros = jnp.zeros((bp, SSM_G * SSM_P), F32)
    yp, rp, ip = _s5_prompt(up, bblk, cblk, dvec, tab, zeros, zeros, bp, tp, 512)
    x2p, h2pt = _outproj(xp, attn_p, yp, wg_bf, bg, wo_bf, gffn, 256, True)
    peer_p = _peer(h2pt, wqt_bf, keys_bf, u_bf, vt_bf, 512, 8)
    y_prompt = _final(x2p, peer_p, gfin, 512).reshape(bp, tp, D_MODEL)

    xs = jnp.pad(x_sample.reshape(bs, D_MODEL), ((0, srows - bs), (0, 0)))
    cos_s, sin_s = _rope_tables(jnp.full((srows,), past_len))
    qs, ks, _, vs, _, us = _inproj(xs, gmix, w_in_bf, cos_s, sin_s, srows)
    heads = lambda a: a[:bs].astype(F32).reshape(bs, 1, HEADS, VD)
    attn_s = _attn_sample(page_table, lam4, heads(qs)[:, 0], cache_k, cache_v, heads(ks), heads(vs),
                          gsub, 4 if n_pages % 4 == 0 else 1)
    pad_s = lambda a: jnp.pad(a.reshape(bs, -1), ((0, srows - bs), (0, 0)))
    ys, rs, is_ = _s5_sample(us, bblk, cblk, dvec, lam1, pad_s(state_ssm_re[0]), pad_s(state_ssm_im[0]))
    x2s, h2s = _outproj(xs, pad_s(attn_s), ys, wg_bf, bg, wo_bf, gffn, srows, False)
    h2s_rep = jnp.tile(h2s[:bs], (128 // bs, 1))
    peer_s = _peer(h2s_rep.T, wqt_bf, keys_bf, u_bf, vt_bf, 128, 8)
    y_sample = _final(x2s, peer_s, gfin, srows)[:bs].reshape(bs, ts, D_MODEL)

    return (y_prompt, y_sample,
            kp.reshape(1, bp, tp, HEADS, 2 * QK), vp.reshape(1, bp, tp, HEADS, VD),
            rp[None], ip[None],
            ks[:bs].reshape(1, bs, ts, HEADS, 2 * QK), vs[:bs].reshape(1, bs, ts, HEADS, VD),
            rs[:bs].reshape(1, bs, SSM_G, SSM_P), is_[:bs].reshape(1, bs, SSM_G, SSM_P))
```

```python
import functools
import math

import jax
import jax.numpy as jnp
from jax import lax
from jax.experimental import pallas as pl
from jax.experimental.pallas import tpu as pltpu

F32 = jnp.float32
BF16 = jnp.bfloat16

D_MODEL = 2048
PAGE = 128
HEADS = 8
QK = 64
VD = 128
AW = HEADS * VD
SSM_W = D_MODEL - AW
SSM_CH = 16
SSM_G = SSM_W // SSM_CH
SSM_P = 64
GB = 16
NB = SSM_G // GB
SL = GB * SSM_P
UL = GB * SSM_CH
P_HEADS = 8
P_TOPK = 16
P_NK = 128
P_HALF = 128
P_E = P_NK * P_NK
EPS = 1e-6
NEG = -1e30
ROPE_THETA = 10000.0
LAM_INIT = 0.8 - 0.6 * math.exp(-0.3 * 0)
VMEM_LIMIT = 56 * 1024 * 1024
PEER_NA = 8

STAIR = [(i, j) for i in range(P_TOPK) for j in range(P_TOPK) if (i + 1) * (j + 1) <= P_TOPK]
STAIR_ROWS = -(-len(STAIR) // 8) * 8


def _cp(*sem):
    return pltpu.CompilerParams(dimension_semantics=sem, vmem_limit_bytes=VMEM_LIMIT)


def _inproj_body(x_ref, g_ref, w_ref, cos_ref, sin_ref,
                 q_ref, k_ref, kb_ref, v_ref, vb_ref, u_ref, h_scr):
    j = pl.program_id(1)

    @pl.when(j == 0)
    def _():
        x = x_ref[...]
        ms = jnp.mean(x * x, axis=-1, keepdims=True)
        h_scr[...] = (x * lax.rsqrt(ms + EPS) * g_ref[...]).astype(BF16)

    z = jnp.dot(h_scr[...], w_ref[...], preferred_element_type=F32)

    def rope(zc):
        lane = lax.broadcasted_iota(jnp.int32, zc.shape, 1)
        first = (lane & (QK - 1)) < (QK // 2)
        partner = jnp.where(first, pltpu.roll(zc, 128 - QK // 2, 1), pltpu.roll(zc, QK // 2, 1))
        return zc * cos_ref[...] + partner * sin_ref[...]

    @pl.when(j == 0)
    def _():
        for c in range(AW // 128):
            sl = slice(c * 128, (c + 1) * 128)
            q_ref[:, sl] = (rope(z[:, sl]) * (QK ** -0.5)).astype(BF16)

    @pl.when(j == 1)
    def _():
        for c in range(AW // 128):
            sl = slice(c * 128, (c + 1) * 128)
            kc = rope(z[:, sl])
            k_ref[:, sl] = kc
            kb_ref[:, sl] = kc.astype(BF16)

    @pl.when(j == 2)
    def _():
        v_ref[...] = z
        vb_ref[...] = z.astype(BF16)

    @pl.when(j == 3)
    def _():
        u_ref[...] = z


def _inproj(x, g, w_bf, cos, sin, tm):
    m = x.shape[0]
    row = lambda i, j: (i, 0)
    outs = [jax.ShapeDtypeStruct((m, AW), BF16), jax.ShapeDtypeStruct((m, AW), F32),
            jax.ShapeDtypeStruct((m, AW), BF16), jax.ShapeDtypeStruct((m, AW), F32),
            jax.ShapeDtypeStruct((m, AW), BF16), jax.ShapeDtypeStruct((m, SSM_W), F32)]
    return pl.pallas_call(
        _inproj_body,
        grid=(m // tm, 4),
        in_specs=[pl.BlockSpec((tm, D_MODEL), row),
                  pl.BlockSpec((1, D_MODEL), lambda i, j: (0, 0)),
                  pl.BlockSpec((D_MODEL, AW), lambda i, j: (0, j)),
                  pl.BlockSpec((tm, 128), row),
                  pl.BlockSpec((tm, 128), row)],
        out_specs=[pl.BlockSpec((tm, AW), row) for _ in outs],
        out_shape=outs,
        scratch_shapes=[pltpu.VMEM((tm, D_MODEL), BF16)],
        compiler_params=_cp("parallel", "arbitrary"),
        name="inproj",
    )(x, g, w_bf, cos, sin)


def _lam(lam4_ref):
    l4 = lam4_ref[...]
    a = jnp.exp(jnp.sum(l4[0:1, :] * l4[1:2, :], axis=-1, keepdims=True))
    b = jnp.exp(jnp.sum(l4[2:3, :] * l4[3:4, :], axis=-1, keepdims=True))
    return a - b + LAM_INIT


def _subnorm(o, gsub):
    o = o * lax.rsqrt(jnp.mean(o * o, axis=-1, keepdims=True) + EPS)
    return o * gsub * (1.0 - LAM_INIT)


def _attn_body(lam4_ref, qt_ref, k_ref, vt_ref, gsub_ref, o_ref, acc_scr, *, tq, tk):
    qi = pl.program_id(2)
    unroll = tq // tk
    qt = qt_ref[...]
    row = lax.broadcasted_iota(jnp.int32, qt.shape, 0)
    zero = jnp.zeros_like(qt)
    qc = (jnp.where(row < QK, qt, zero), jnp.where(row >= QK, qt, zero))
    acc_scr[...] = jnp.zeros(acc_scr.shape, F32)
    rel = (lax.broadcasted_iota(jnp.int32, (tk, tq), 1) - lax.broadcasted_iota(jnp.int32, (tk, tq), 0))

    def group(gi, carry, masked):
        ms, ls = list(carry[0]), list(carry[1])
        blocks = [gi * unroll + u for u in range(unroll)]
        kbs = [k_ref[pl.ds(pl.multiple_of(b * tk, tk), tk), :] for b in blocks]
        sts = [[jnp.dot(kb, qc[c], preferred_element_type=F32) for c in range(2)] for kb in kbs]
        for u, b in enumerate(blocks):
            vtb = vt_ref[b]
            for c in range(2):
                st = sts[u][c]
                if masked:
                    st = jnp.where(rel >= u * tk, st, NEG)
                m_new = jnp.maximum(ms[c], jnp.max(st, axis=0, keepdims=True))
                alpha = jnp.exp(ms[c] - m_new)
                p = jnp.exp(st - m_new)
                ls[c] = alpha * ls[c] + jnp.sum(p, axis=0, keepdims=True)
                acc_scr[c] = alpha * acc_scr[c] + jnp.dot(vtb, p.astype(BF16), preferred_element_type=F32)
                ms[c] = m_new
        return tuple(ms), tuple(ls)

    m0 = jnp.full((1, tq), NEG, F32)
    l0 = jnp.zeros((1, tq), F32)
    carry = lax.fori_loop(0, qi, lambda g, c: group(g, c, False), ((m0, m0), (l0, l0)))
    _, ls = group(qi, carry, True)

    o = acc_scr[0] / ls[0] - _lam(lam4_ref) * (acc_scr[1] / ls[1])
    o = o * lax.rsqrt(jnp.mean(o * o, axis=0, keepdims=True) + EPS)
    o_ref[...] = (o * gsub_ref[...] * (1.0 - LAM_INIT)).T


def _attn_prompt(lam4, qb, kb, vb, gsub, nb, t, tq, tk):
    qt = qb.reshape(nb, t, AW).transpose(0, 2, 1)
    kb = kb.reshape(nb, t, AW)
    vt = vb.reshape(nb, t // tk, tk, AW).transpose(0, 1, 3, 2)
    gcol = jnp.broadcast_to(gsub.reshape(VD, 1), (VD, tq))
    out = pl.pallas_call(
        functools.partial(_attn_body, tq=tq, tk=tk),
        grid=(nb, HEADS, t // tq),
        in_specs=[pl.BlockSpec((4, QK), lambda b, h, i: (0, 0)),
                  pl.BlockSpec((None, VD, tq), lambda b, h, i: (b, h, i)),
                  pl.BlockSpec((None, t, VD), lambda b, h, i: (b, 0, h)),
                  pl.BlockSpec((None, t // tk, VD, tk), lambda b, h, i: (b, 0, h, 0)),
                  pl.BlockSpec((VD, tq), lambda b, h, i: (0, 0))],
        out_specs=pl.BlockSpec((None, tq, VD), lambda b, h, i: (b, i, h)),
        out_shape=jax.ShapeDtypeStruct((nb, t, AW), F32),
        scratch_shapes=[pltpu.VMEM((2, VD, tq), F32)],
        compiler_params=_cp("parallel", "parallel", "arbitrary"),
        name="attn_prompt",
    )(lam4, qt, kb, vt, gcol)
    return out.reshape(nb * t, AW)


def _sattn_body(pt_ref, lam4_ref, q_ref, *refs, gp):
    k_refs, v_refs = refs[:gp], refs[gp:2 * gp]
    kn_ref, vn_ref, gsub_ref, o_ref, m_scr, l_scr, acc_scr = refs[2 * gp:]
    j = pl.program_id(1)
    q = q_ref[...]
    first = lax.broadcasted_iota(jnp.int32, (HEADS, VD), 1) < QK
    q = q * math.log2(math.e)
    qc = (jnp.where(first, q, 0.0), jnp.where(first, 0.0, q))

    @pl.when(j == 0)
    def _():
        m_scr[...] = jnp.full(m_scr.shape, NEG, F32)
        l_scr[...] = jnp.zeros(l_scr.shape, F32)
        acc_scr[...] = jnp.zeros(acc_scr.shape, F32)

    def update(kp, vp):
        for c in range(2):
            s = jnp.sum(kp * qc[c][None], axis=-1, keepdims=True)
            m_prev = m_scr[c]
            m_new = jnp.maximum(m_prev, jnp.max(s, axis=0))
            alpha = jnp.exp2(m_prev - m_new)
            p = jnp.exp2(s - m_new[None])
            l_scr[c] = alpha * l_scr[c] + jnp.sum(p, axis=0)
            acc_scr[c] = alpha * acc_scr[c] + jnp.sum(p * vp, axis=0)
            m_scr[c] = m_new

    kc = 32
    for g in range(gp):
        for i in range(PAGE // kc):
            update(k_refs[g][i * kc:(i + 1) * kc], v_refs[g][i * kc:(i + 1) * kc])

    @pl.when(j == pl.num_programs(1) - 1)
    def _():
        update(kn_ref[...], vn_ref[...])
        o = acc_scr[0] / l_scr[0] - _lam(lam4_ref) * (acc_scr[1] / l_scr[1])
        o_ref[...] = _subnorm(o, gsub_ref[...])


def _attn_sample(page_table, lam4, q, cache_k, cache_v, kn, vn, gsub, gp):
    bs, n_pages = page_table.shape
    pt = page_table.reshape(-1).astype(jnp.int32)

    def page(g):
        return pl.BlockSpec((None, None, PAGE, HEADS, VD),
                            lambda b, j, pt_ref: (0, pt_ref[b * n_pages + j * gp + g], 0, 0, 0))

    per_b = lambda b, j, p: (b, 0, 0, 0)
    grid_spec = pltpu.PrefetchScalarGridSpec(
        num_scalar_prefetch=1,
        grid=(bs, n_pages // gp),
        in_specs=[pl.BlockSpec((4, QK), lambda b, j, p: (0, 0)),
                  pl.BlockSpec((None, HEADS, VD), lambda b, j, p: (b, 0, 0))]
                 + [page(g) for g in range(gp)] + [page(g) for g in range(gp)]
                 + [pl.BlockSpec((None, 1, HEADS, VD), per_b),
                    pl.BlockSpec((None, 1, HEADS, VD), per_b),
                    pl.BlockSpec((1, VD), lambda b, j, p: (0, 0))],
        out_specs=pl.BlockSpec((None, HEADS, VD), lambda b, j, p: (b, 0, 0)),
        scratch_shapes=[pltpu.VMEM((2, HEADS, VD), F32), pltpu.VMEM((2, HEADS, VD), F32),
                        pltpu.VMEM((2, HEADS, VD), F32)],
    )
    out = pl.pallas_call(
        functools.partial(_sattn_body, gp=gp),
        grid_spec=grid_spec,
        out_shape=jax.ShapeDtypeStruct((bs, HEADS, VD), F32),
        compiler_params=_cp("parallel", "arbitrary"),
        name="attn_sample",
    )(pt, lam4, q, *([cache_k] * gp), *([cache_v] * gp), kn, vn, gsub)
    return out.reshape(bs, AW)


def _cmul(ar, ai, br, bi):
    return ar * br - ai * bi, ar * bi + ai * br


def _s5_prep_body(are_ref, aim_ref, ldt_ref, arer_ref, aimr_ref, ldtr_ref, bre_ref, bim_ref,
                  pow_ref, bbr_ref, bbi_ref):
    def disc(are, aim, ldt):
        dt = jnp.exp(ldt)
        mag = jnp.exp(are * dt)
        lr = mag * jnp.cos(aim * dt)
        li = mag * jnp.sin(aim * dt)
        return lr, li

    lr, li = disc(are_ref[...], aim_ref[...], ldt_ref[...])
    pr, pi = lr, li
    for n in range(8):
        pow_ref[2 * n] = pr
        pow_ref[2 * n + 1] = pi
        pr, pi = _cmul(pr, pi, lr, li)

    are, aim = arer_ref[...], aimr_ref[...]
    lrr, lir = disc(are, aim, ldtr_ref[...])
    den = are * are + aim * aim
    zr = lrr - 1.0
    fr = (zr * are + lir * aim) / den
    fi = (lir * are - zr * aim) / den
    br, bi = bre_ref[...], bim_ref[...]
    bbr_ref[...] = fr * br - fi * bi
    bbi_ref[...] = fr * bi + fi * br


def _s5_prep(a_re, a_im, log_dt, b_re, b_im):
    rep = lambda a: jnp.repeat(a, SSM_CH, axis=1)
    ldt = jnp.broadcast_to(log_dt[:, None], (SSM_G, SSM_P))
    pw, bbr, bbi = pl.pallas_call(
        _s5_prep_body,
        out_shape=[jax.ShapeDtypeStruct((16, SSM_G, SSM_P), F32),
                   jax.ShapeDtypeStruct((SSM_G, SSM_P * SSM_CH), F32),
                   jax.ShapeDtypeStruct((SSM_G, SSM_P * SSM_CH), F32)],
        name="s5_prep",
    )(a_re, a_im, ldt, rep(a_re), rep(a_im), rep(ldt),
      b_re.reshape(SSM_G, -1), b_im.reshape(SSM_G, -1))
    return pw, bbr.reshape(SSM_G, SSM_P, SSM_CH), bbi.reshape(SSM_G, SSM_P, SSM_CH)


def _s5_tables(pw, bbr, bbi, c_re, c_im, d):
    eye = jnp.eye(GB, dtype=F32)
    def blk_in(bb):
        x = bb.reshape(NB, GB, SSM_P, SSM_CH)
        return jnp.einsum('ngpc,gh->ngchp', x, eye).reshape(NB, UL, SL)
    bblk = jnp.concatenate([blk_in(bbr), blk_in(bbi)], axis=2).astype(BF16)
    def blk_out(cc):
        x = cc.reshape(NB, GB, SSM_CH, SSM_P)
        return jnp.einsum('ngcp,gh->ngphc', x, eye).reshape(NB, SL, UL)
    cblk = jnp.concatenate([blk_out(c_re), blk_out(-c_im)], axis=1).astype(BF16)
    lanes = lambda a: a.reshape(NB, 1, SL)
    t_idx = jnp.arange(8)[None, :, None]
    tabs = []
    for n in (1, 2, 4):
        keep = (t_idx >= n).astype(F32)
        tabs += [lanes(pw[2 * (n - 1)]) * keep, lanes(pw[2 * (n - 1) + 1]) * keep]
    pr = jnp.stack([pw[2 * n] for n in range(8)], axis=0).reshape(8, NB, SL).transpose(1, 0, 2)
    pi = jnp.stack([pw[2 * n + 1] for n in range(8)], axis=0).reshape(8, NB, SL).transpose(1, 0, 2)
    tabs += [pr, pi]
    tab = jnp.stack(tabs, axis=1)
    lam1 = jnp.stack([lanes(pw[0]), lanes(pw[1])], axis=1).reshape(NB, 2, SL)
    dvec = d.reshape(NB, 1, UL)
    return bblk, cblk, tab, lam1, dvec


def _s5_body(u_ref, bblk_ref, cblk_ref, d_ref, tab_ref, h0r_ref, h0i_ref,
             y_ref, hr_ref, hi_ref, st_scr, cr_scr, ci_scr, *, chunk):
    k = pl.program_id(2)

    @pl.when(k == 0)
    def _():
        cr_scr[...] = h0r_ref[...]
        ci_scr[...] = h0i_ref[...]

    u = u_ref[...]
    st_scr[...] = jnp.dot(u.astype(BF16), bblk_ref[...], preferred_element_type=F32)

    def tile(r, carry):
        cr, ci = carry
        rows = pl.ds(pl.multiple_of(r * 8, 8), 8)
        xr = st_scr[rows, 0:SL]
        xi = st_scr[rows, SL:2 * SL]
        for lvl, sh in enumerate((1, 2, 4)):
            ar, ai = tab_ref[2 * lvl], tab_ref[2 * lvl + 1]
            sr, si = pltpu.roll(xr, sh, 0), pltpu.roll(xi, sh, 0)
            xr, xi = xr + (ar * sr - ai * si), xi + (ar * si + ai * sr)
        pr, pi = tab_ref[6], tab_ref[7]
        hr = xr + (pr * cr - pi * ci)
        hi = xi + (pr * ci + pi * cr)
        st_scr[rows, 0:SL] = hr
        st_scr[rows, SL:2 * SL] = hi
        return hr[7:8, :], hi[7:8, :]

    cr, ci = lax.fori_loop(0, chunk // 8, tile, (cr_scr[...], ci_scr[...]), unroll=2)
    cr_scr[...] = cr
    ci_scr[...] = ci
    y_ref[...] = (jnp.dot(st_scr[...].astype(BF16), cblk_ref[...], preferred_element_type=F32)
                  + d_ref[...] * u)

    @pl.when(k == pl.num_programs(2) - 1)
    def _():
        hr_ref[...] = cr
        hi_ref[...] = ci


def _s5_prompt(u, bblk, cblk, dvec, tab, h0r, h0i, nb, t, chunk):
    nk = t // chunk
    st = jax.ShapeDtypeStruct((nb, 1, SSM_G * SSM_P), F32)
    y, hr, hi = pl.pallas_call(
        functools.partial(_s5_body, chunk=chunk),
        grid=(nb, NB, nk),
        in_specs=[pl.BlockSpec((chunk, UL), lambda b, n, k: (b * nk + k, n)),
                  pl.BlockSpec((None, UL, 2 * SL), lambda b, n, k: (n, 0, 0)),
                  pl.BlockSpec((None, 2 * SL, UL), lambda b, n, k: (n, 0, 0)),
                  pl.BlockSpec((None, 1, UL), lambda b, n, k: (n, 0, 0)),
                  pl.BlockSpec((None, 8, 8, SL), lambda b, n, k: (n, 0, 0, 0)),
                  pl.BlockSpec((None, 1, SL), lambda b, n, k: (b, 0, n)),
                  pl.BlockSpec((None, 1, SL), lambda b, n, k: (b, 0, n))],
        out_specs=[pl.BlockSpec((chunk, UL), lambda b, n, k: (b * nk + k, n)),
                   pl.BlockSpec((None, 1, SL), lambda b, n, k: (b, 0, n)),
                   pl.BlockSpec((None, 1, SL), lambda b, n, k: (b, 0, n))],
        out_shape=[jax.ShapeDtypeStruct((nb * t, SSM_W), F32), st, st],
        scratch_shapes=[pltpu.VMEM((chunk, 2 * SL), F32), pltpu.VMEM((1, SL), F32),
                        pltpu.VMEM((1, SL), F32)],
        compiler_params=_cp("parallel", "parallel", "arbitrary"),
        name="s5_prompt",
    )(u, bblk, cblk, dvec, tab, h0r.reshape(nb, 1, -1), h0i.reshape(nb, 1, -1))
    return y, hr.reshape(nb, SSM_G, SSM_P), hi.reshape(nb, SSM_G, SSM_P)


def _s5_step_body(u_ref, bblk_ref, cblk_ref, d_ref, lam_ref, h0r_ref, h0i_ref, y_ref, hr_ref, hi_ref):
    u = u_ref[...]
    bu = jnp.dot(u.astype(BF16), bblk_ref[...], preferred_element_type=F32)
    lr, li = lam_ref[0:1, :], lam_ref[1:2, :]
    h0r, h0i = h0r_ref[...], h0i_ref[...]
    hr = bu[:, 0:SL] + (lr * h0r - li * h0i)
    hi = bu[:, SL:2 * SL] + (lr * h0i + li * h0r)
    hr_ref[...] = hr
    hi_ref[...] = hi
    hcat = jnp.concatenate([hr, hi], axis=1).astype(BF16)
    y_ref[...] = jnp.dot(hcat, cblk_ref[...], preferred_element_type=F32) + d_ref[...] * u


def _s5_sample(u, bblk, cblk, dvec, lam1, h0r, h0i):
    rows = u.shape[0]
    st = jax.ShapeDtypeStruct((rows, SSM_G * SSM_P), F32)
    return pl.pallas_call(
        _s5_step_body,
        grid=(NB,),
        in_specs=[pl.BlockSpec((rows, UL), lambda n: (0, n)),
                  pl.BlockSpec((None, UL, 2 * SL), lambda n: (n, 0, 0)),
                  pl.BlockSpec((None, 2 * SL, UL), lambda n: (n, 0, 0)),
                  pl.BlockSpec((None, 1, UL), lambda n: (n, 0, 0)),
                  pl.BlockSpec((None, 2, SL), lambda n: (n, 0, 0)),
                  pl.BlockSpec((rows, SL), lambda n: (0, n)),
                  pl.BlockSpec((rows, SL), lambda n: (0, n))],
        out_specs=[pl.BlockSpec((rows, UL), lambda n: (0, n)),
                   pl.BlockSpec((rows, SL), lambda n: (0, n)),
                   pl.BlockSpec((rows, SL), lambda n: (0, n))],
        out_shape=[jax.ShapeDtypeStruct((rows, SSM_W), F32), st, st],
        compiler_params=_cp("parallel"),
        name="s5_sample",
    )(u, bblk, cblk, dvec, lam1, h0r, h0i)


def _outproj_body(x_ref, a_ref, y_ref, wg_ref, bg_ref, wo_ref, g_ref, x2_ref, h2_ref, *, transposed):
    y = jax.nn.gelu(y_ref[...])
    gate = jnp.dot(y.astype(BF16), wg_ref[...], preferred_element_type=F32) + bg_ref[...]
    y = y * jax.nn.sigmoid(gate)
    mix = (jnp.dot(a_ref[...].astype(BF16), wo_ref[0:AW, :], preferred_element_type=F32)
           + jnp.dot(y.astype(BF16), wo_ref[AW:, :], preferred_element_type=F32))
    x2 = x_ref[...] + mix
    x2_ref[...] = x2
    ms = jnp.mean(x2 * x2, axis=-1, keepdims=True)
    h2 = x2 * lax.rsqrt(ms + EPS) * g_ref[...]
    h2_ref[...] = (h2.T if transposed else h2).astype(BF16)


def _outproj(x, attn, y, wg_bf, bg, wo_bf, g, tm, transposed):
    m = x.shape[0]
    row = lambda i: (i, 0)
    fix = lambda i: (0, 0)
    if transposed:
        h2_spec, h2_shape = pl.BlockSpec((D_MODEL, tm), lambda i: (0, i)), (D_MODEL, m)
    else:
        h2_spec, h2_shape = pl.BlockSpec((tm, D_MODEL), row), (m, D_MODEL)
    return pl.pallas_call(
        functools.partial(_outproj_body, transposed=transposed),
        grid=(m // tm,),
        in_specs=[pl.BlockSpec((tm, D_MODEL), row), pl.BlockSpec((tm, AW), row),
                  pl.BlockSpec((tm, SSM_W), row), pl.BlockSpec((SSM_W, SSM_W), fix),
                  pl.BlockSpec((1, SSM_W), fix), pl.BlockSpec((D_MODEL, D_MODEL), fix),
                  pl.BlockSpec((1, D_MODEL), fix)],
        out_specs=[pl.BlockSpec((tm, D_MODEL), row), h2_spec],
        out_shape=[jax.ShapeDtypeStruct((m, D_MODEL), F32), jax.ShapeDtypeStruct(h2_shape, BF16)],
        compiler_params=_cp("parallel"),
        name="outproj",
    )(x, attn, y, wg_bf, bg, wo_bf, g)


def _topk_body(h_ref, wq_ref, keys_ref, cnt_ref, w1_ref, rank_ref, e2_ref, q_scr, cand_scr, *, tb):
    q_scr[...] = jnp.dot(wq_ref[...], h_ref[...], preferred_element_type=F32).astype(BF16)
    ninf = -jnp.inf

    def head(h, carry):
        for lc in range(tb // 128):
            ls = slice(lc * 128, (lc + 1) * 128)
            sc = []
            for c in range(2):
                hc = 2 * h + c
                qrows = pl.ds(pl.multiple_of(hc * P_HALF, P_HALF), P_HALF)
                sc.append(jnp.dot(keys_ref[hc], q_scr[qrows, ls],
                                  preferred_element_type=F32))
            v1, v2 = [], []
            w = sc[0]
            for _ in range(P_TOPK):
                mx = jnp.max(w, axis=0, keepdims=True)
                v1.append(mx)
                w = jnp.where(w == mx, ninf, w)
            w = sc[1]
            rank = jnp.full(w.shape, float(P_TOPK), F32)
            for r in range(P_TOPK):
                mx = jnp.max(w, axis=0, keepdims=True)
                v2.append(mx)
                hit = w == mx
                rank = jnp.where(hit, float(r), rank)
                w = jnp.where(hit, ninf, w)
            cand_scr[...] = jnp.full(cand_scr.shape, ninf, F32)
            for n, (i, j) in enumerate(STAIR):
                cand_scr[n:n + 1, :] = v1[i] + v2[j]
            w = cand_scr[...]
            top = v1[0] + v2[0]
            z = jnp.zeros_like(top)
            tau = top
            for _ in range(P_TOPK):
                tau = jnp.max(w, axis=0, keepdims=True)
                z = z + jnp.exp(tau - top)
                w = jnp.where(w == tau, ninf, w)
            cnt = jnp.zeros(sc[0].shape, F32)
            for stride in (8, 4, 2, 1):
                probe = None
                for base in range(0, P_TOPK, 2 * stride):
                    row = v2[base + stride - 1]
                    probe = row if probe is None else jnp.where(cnt >= float(base), row, probe)
                cnt = cnt + jnp.where(sc[0] + probe >= tau, float(stride), 0.0)
            cnt = cnt + jnp.where(sc[0] + v2[P_TOPK - 1] >= tau, 1.0, 0.0)
            rows = pl.ds(pl.multiple_of(h * P_NK, P_NK), P_NK)
            cnt_ref[rows, ls] = cnt
            w1_ref[rows, ls] = jnp.exp(sc[0] - v1[0]) / z
            rank_ref[rows, ls] = rank
            e2_ref[rows, ls] = jnp.exp(sc[1] - v2[0])
        return carry

    lax.fori_loop(0, P_HEADS, head, 0)


def _peer_topk(h2t, wqt_bf, keys_bf, tb):
    m = h2t.shape[1]
    f32 = jax.ShapeDtypeStruct((P_HEADS * P_NK, m), F32)
    col = lambda i: (0, i)
    return pl.pallas_call(
        functools.partial(_topk_body, tb=tb),
        grid=(m // tb,),
        in_specs=[pl.BlockSpec((D_MODEL, tb), col),
                  pl.BlockSpec((P_HEADS * 2 * P_HALF, D_MODEL), lambda i: (0, 0)),
                  pl.BlockSpec((P_HEADS * 2, P_NK, P_HALF), lambda i: (0, 0, 0))],
        out_specs=[pl.BlockSpec((P_HEADS * P_NK, tb), col)] * 4,
        out_shape=[f32, f32, f32, f32],
        scratch_shapes=[pltpu.VMEM((P_HEADS * 2 * P_HALF, tb), BF16), pltpu.VMEM((STAIR_ROWS, 128), F32)],
        compiler_params=_cp("parallel"),
        name="peer_topk",
    )(h2t, wqt_bf, keys_bf)


def _peer_body(h_ref, u_ref, vt_ref, cnt_ref, w1_ref, rank_in, e2_in, o_ref,
               a_scr, wm_scr, rank_ref, e2_ref, *, tb, na):
    step = pl.program_id(1)

    @pl.when(step == 0)
    def _():
        o_ref[...] = jnp.zeros(o_ref.shape, F32)
        rank_ref[...] = rank_in[...].astype(BF16)
        e2_ref[...] = e2_in[...].astype(BF16)

    a_scr[...] = jnp.dot(u_ref[...], h_ref[...], preferred_element_type=F32)

    def row_tile(ref, al, h, ls):
        packed = jnp.broadcast_to(ref[h, al:al + 1, ls], (16, 128)).astype(BF16)
        return jnp.tile(packed, (P_NK // 16, 1))

    for al in range(na):
        arows = slice(al * P_NK, (al + 1) * P_NK)
        for lc in range(tb // 128):
            ls = slice(lc * 128, (lc + 1) * 128)
            g = jnp.zeros((P_NK, 128), BF16)
            for h in range(P_HEADS):
                rows = slice(h * P_NK, (h + 1) * P_NK)
                sel = rank_ref[rows, ls] < row_tile(cnt_ref, al, h, ls)
                g = g + jnp.where(sel, e2_ref[rows, ls] * row_tile(w1_ref, al, h, ls), jnp.zeros((), BF16))
            act = jax.nn.gelu(a_scr[arows, ls])
            wm_scr[arows, ls] = (act * g.astype(F32)).astype(BF16)
    o_ref[...] += jnp.dot(vt_ref[...], wm_scr[...], preferred_element_type=F32)


def _peer_dense(h2t, u_bf, vt_bf, cnt, w1, rank, e2, tb, na):
    m = h2t.shape[1]
    col = lambda i, k: (0, i)
    return pl.pallas_call(
        functools.partial(_peer_body, tb=tb, na=na),
        grid=(m // tb, P_NK // na),
        in_specs=[pl.BlockSpec((D_MODEL, tb), col),
                  pl.BlockSpec((na * P_NK, D_MODEL), lambda i, k: (k, 0)),
                  pl.BlockSpec((None, D_MODEL, na * P_NK), lambda i, k: (k, 0, 0)),
                  pl.BlockSpec((P_HEADS, na, tb), lambda i, k: (0, k, i)),
                  pl.BlockSpec((P_HEADS, na, tb), lambda i, k: (0, k, i)),
                  pl.BlockSpec((P_HEADS * P_NK, tb), col),
                  pl.BlockSpec((P_HEADS * P_NK, tb), col)],
        out_specs=pl.BlockSpec((D_MODEL, tb), col),
        out_shape=jax.ShapeDtypeStruct((D_MODEL, m), F32),
        scratch_shapes=[pltpu.VMEM((na * P_NK, tb), F32), pltpu.VMEM((na * P_NK, tb), BF16),
                        pltpu.VMEM((P_HEADS * P_NK, tb), BF16), pltpu.VMEM((P_HEADS * P_NK, tb), BF16)],
        compiler_params=_cp("parallel", "arbitrary"),
        name="peer_dense",
    )(h2t, u_bf, vt_bf, cnt, w1, rank, e2)


def _final_body(x_ref, pt_ref, g_ref, y_ref):
    tm = x_ref.shape[0]
    x = x_ref[...] + pt_ref[...].T[:tm]
    ms = jnp.mean(x * x, axis=-1, keepdims=True)
    y_ref[...] = x * lax.rsqrt(ms + EPS) * g_ref[...]


def _final(x2, pt, g, tm):
    m = x2.shape[0]
    row = lambda i: (i, 0)
    return pl.pallas_call(
        _final_body,
        grid=(m // tm,),
        in_specs=[pl.BlockSpec((tm, D_MODEL), row),
                  pl.BlockSpec((D_MODEL, max(tm, 128)), lambda i: (0, i)),
                  pl.BlockSpec((1, D_MODEL), lambda i: (0, 0))],
        out_specs=pl.BlockSpec((tm, D_MODEL), row),
        out_shape=jax.ShapeDtypeStruct((m, D_MODEL), F32),
        compiler_params=_cp("parallel"),
        name="final_norm",
    )(x2, pt, g)


def _rope_tables(pos):
    half = QK // 2
    inv = 1.0 / (ROPE_THETA ** (jnp.arange(half, dtype=F32) * 2.0 / QK))
    ang = pos.astype(F32)[:, None] * inv[None, :]
    c, s = jnp.cos(ang), jnp.sin(ang)
    cos = jnp.concatenate([c, c, c, c], axis=1)
    sin = jnp.concatenate([-s, s, -s, s], axis=1)
    return cos, sin


def _peer(h2t, wqt_bf, keys_bf, u_bf, vt_bf, tb, na):
    cnt, w1, rank, e2 = _peer_topk(h2t, wqt_bf, keys_bf, min(tb, 256))
    per_head = lambda a: a.reshape(P_HEADS, P_NK, -1)
    return _peer_dense(h2t, u_bf, vt_bf, per_head(cnt), per_head(w1), rank, e2, tb, na)


def kernel(x_prompt, x_sample, cache_k, cache_v, state_ssm_re, state_ssm_im, page_table, g_mix, w_in, lambda_q1, lambda_k1, lambda_q2, lambda_k2, g_sub, ssm_a_re, ssm_a_im, ssm_log_dt, ssm_b_re, ssm_b_im, ssm_c_re, ssm_c_im, ssm_d, w_glu, b_glu, w_out, g_ffn, peer_w_q, peer_sub_keys, peer_u, peer_v, g_final):
    bp, tp, _ = x_prompt.shape
    bs, ts, _ = x_sample.shape
    assert ts == 1 and g_mix.shape[0] == 1
    n_pages = page_table.shape[1]
    past_len = n_pages * PAGE
    mp = bp * tp
    srows = 16

    w_in_bf = w_in[0].astype(BF16)
    wg_bf = w_glu[0].astype(BF16)
    wo_bf = w_out[0].astype(BF16)
    wqt_bf = peer_w_q[0].T.astype(BF16)
    keys_bf = peer_sub_keys[0].reshape(P_HEADS * 2, P_NK, P_HALF).astype(BF16)
    u_bf = peer_u[0].astype(BF16)
    vt_bf = peer_v[0].reshape(P_NK // PEER_NA, PEER_NA * P_NK, D_MODEL).transpose(0, 2, 1).astype(BF16)
    gmix = g_mix[0].reshape(1, -1)
    gffn = g_ffn[0].reshape(1, -1)
    gfin = g_final.reshape(1, -1)
    gsub = g_sub[0].reshape(1, -1)
    bg = b_glu[0].reshape(1, -1)
    lam4 = jnp.stack([lambda_q1[0], lambda_k1[0], lambda_q2[0], lambda_k2[0]], axis=0)
    pw, bbr, bbi = _s5_prep(ssm_a_re[0], ssm_a_im[0], ssm_log_dt[0], ssm_b_re[0], ssm_b_im[0])
    bblk, cblk, tab, lam1, dvec = _s5_tables(pw, bbr, bbi, ssm_c_re[0], ssm_c_im[0], ssm_d[0])

    xp = x_prompt.reshape(mp, D_MODEL)
    cos_p, sin_p = _rope_tables(jnp.tile(jnp.arange(tp), bp))
    qb, kp, kpb, vp, vpb, up = _inproj(xp, gmix, w_in_bf, cos_p, sin_p, 512)
    attn_p = _attn_prompt(lam4, qb, kpb, vpb, gsub, bp, tp, 512, 128)
    zeros = jnp.zeros((bp, SSM_G * SSM_P), F32)
    yp, rp, ip = _s5_prompt(up, bblk, cblk, dvec, tab, zeros, zeros, bp, tp, 512)
    x2p, h2pt = _outproj(xp, attn_p, yp, wg_bf, bg, wo_bf, gffn, 256, True)
    peer_p = _peer(h2pt, wqt_bf, keys_bf, u_bf, vt_bf, 512, PEER_NA)
    y_prompt = _final(x2p, peer_p, gfin, 512).reshape(bp, tp, D_MODEL)

    xs = jnp.pad(x_sample.reshape(bs, D_MODEL), ((0, srows - bs), (0, 0)))
    cos_s, sin_s = _rope_tables(jnp.full((srows,), past_len))
    qs, ks, _, vs, _, us = _inproj(xs, gmix, w_in_bf, cos_s, sin_s, srows)
    heads = lambda a: a[:bs].astype(F32).reshape(bs, 1, HEADS, VD)
    attn_s = _attn_sample(page_table, lam4, heads(qs)[:, 0], cache_k, cache_v, heads(ks), heads(vs),
                          gsub, 4 if n_pages % 4 == 0 else 1)
    pad_s = lambda a: jnp.pad(a.reshape(bs, -1), ((0, srows - bs), (0, 0)))
    ys, rs, is_ = _s5_sample(us, bblk, cblk, dvec, lam1, pad_s(state_ssm_re[0]), pad_s(state_ssm_im[0]))
    x2s, h2s = _outproj(xs, pad_s(attn_s), ys, wg_bf, bg, wo_bf, gffn, srows, False)
    h2s_rep = jnp.tile(h2s[:bs], (128 // bs, 1))
    peer_s = _peer(h2s_rep.T, wqt_bf, keys_bf, u_bf, vt_bf, 128, PEER_NA)
    y_sample = _final(x2s, peer_s, gfin, srows)[:bs].reshape(bs, ts, D_MODEL)

    return (y_prompt, y_sample,
            kp.reshape(1, bp, tp, HEADS, 2 * QK), vp.reshape(1, bp, tp, HEADS, VD),
            rp[None], ip[None],
            ks[:bs].reshape(1, bs, ts, HEADS, 2 * QK), vs[:bs].reshape(1, bs, ts, HEADS, VD),
            rs[:bs].reshape(1, bs, SSM_G, SSM_P), is_[:bs].reshape(1, bs, SSM_G, SSM_P))
```

```python
import functools
import math

import jax
import jax.numpy as jnp
from jax import lax
from jax.experimental import pallas as pl
from jax.experimental.pallas import tpu as pltpu

F32 = jnp.float32
BF16 = jnp.bfloat16

D_MODEL = 2048
PAGE = 128
HEADS = 8
QK = 64
VD = 128
AW = HEADS * VD
SSM_W = D_MODEL - AW
SSM_CH = 16
SSM_G = SSM_W // SSM_CH
SSM_P = 64
GB = 16
NB = SSM_G // GB
SL = GB * SSM_P
UL = GB * SSM_CH
P_HEADS = 8
P_TOPK = 16
P_NK = 128
P_HALF = 128
P_E = P_NK * P_NK
EPS = 1e-6
NEG = -1e30
ROPE_THETA = 10000.0
LAM_INIT = 0.8 - 0.6 * math.exp(-0.3 * 0)
VMEM_LIMIT = 56 * 1024 * 1024
Q_SCALE = QK ** -0.5 * math.log2(math.e)
PEER_NA = 8

STAIR = [(i, j) for i in range(P_TOPK) for j in range(P_TOPK) if (i + 1) * (j + 1) <= P_TOPK]
STAIR_ROWS = -(-len(STAIR) // 8) * 8


def _cp(*sem):
    return pltpu.CompilerParams(dimension_semantics=sem, vmem_limit_bytes=VMEM_LIMIT)


def _inproj_body(x_ref, g_ref, w_ref, cos_ref, sin_ref,
                 q_ref, k_ref, kb_ref, v_ref, vb_ref, u_ref, h_scr):
    j = pl.program_id(1)

    @pl.when(j == 0)
    def _():
        x = x_ref[...]
        ms = jnp.mean(x * x, axis=-1, keepdims=True)
        h_scr[...] = (x * lax.rsqrt(ms + EPS) * g_ref[...]).astype(BF16)

    z = jnp.dot(h_scr[...], w_ref[...], preferred_element_type=F32)

    def rope(zc):
        lane = lax.broadcasted_iota(jnp.int32, zc.shape, 1)
        first = (lane & (QK - 1)) < (QK // 2)
        partner = jnp.where(first, pltpu.roll(zc, 128 - QK // 2, 1), pltpu.roll(zc, QK // 2, 1))
        return zc * cos_ref[...] + partner * sin_ref[...]

    @pl.when(j == 0)
    def _():
        for c in range(AW // 128):
            sl = slice(c * 128, (c + 1) * 128)
            q_ref[:, sl] = (rope(z[:, sl]) * Q_SCALE).astype(BF16)

    @pl.when(j == 1)
    def _():
        for c in range(AW // 128):
            sl = slice(c * 128, (c + 1) * 128)
            kc = rope(z[:, sl])
            k_ref[:, sl] = kc
            kb_ref[:, sl] = kc.astype(BF16)

    @pl.when(j == 2)
    def _():
        v_ref[...] = z
        vb_ref[...] = z.astype(BF16)

    @pl.when(j == 3)
    def _():
        u_ref[...] = z


def _inproj(x, g, w_bf, cos, sin, tm):
    m = x.shape[0]
    row = lambda i, j: (i, 0)
    outs = [jax.ShapeDtypeStruct((m, AW), BF16), jax.ShapeDtypeStruct((m, AW), F32),
            jax.ShapeDtypeStruct((m, AW), BF16), jax.ShapeDtypeStruct((m, AW), F32),
            jax.ShapeDtypeStruct((m, AW), BF16), jax.ShapeDtypeStruct((m, SSM_W), F32)]
    return pl.pallas_call(
        _inproj_body,
        grid=(m // tm, 4),
        in_specs=[pl.BlockSpec((tm, D_MODEL), row),
                  pl.BlockSpec((1, D_MODEL), lambda i, j: (0, 0)),
                  pl.BlockSpec((D_MODEL, AW), lambda i, j: (0, j)),
                  pl.BlockSpec((tm, 128), row),
                  pl.BlockSpec((tm, 128), row)],
        out_specs=[pl.BlockSpec((tm, AW), row) for _ in outs],
        out_shape=outs,
        scratch_shapes=[pltpu.VMEM((tm, D_MODEL), BF16)],
        compiler_params=_cp("parallel", "arbitrary"),
        name="inproj",
    )(x, g, w_bf, cos, sin)


def _lam(lam4_ref):
    l4 = lam4_ref[...]
    a = jnp.exp(jnp.sum(l4[0:1, :] * l4[1:2, :], axis=-1, keepdims=True))
    b = jnp.exp(jnp.sum(l4[2:3, :] * l4[3:4, :], axis=-1, keepdims=True))
    return a - b + LAM_INIT


def _subnorm(o, gsub):
    o = o * lax.rsqrt(jnp.mean(o * o, axis=-1, keepdims=True) + EPS)
    return o * gsub * (1.0 - LAM_INIT)


def _attn_body(lam4_ref, qt_ref, k_ref, vt_ref, gsub_ref, o_ref, acc_scr, *, tq, tk):
    qi = pl.program_id(2)
    unroll = tq // tk
    qt = qt_ref[...]
    row = lax.broadcasted_iota(jnp.int32, qt.shape, 0)
    zero = jnp.zeros_like(qt)
    qc = (jnp.where(row < QK, qt, zero), jnp.where(row >= QK, qt, zero))
    acc_scr[...] = jnp.zeros(acc_scr.shape, F32)
    rel = (lax.broadcasted_iota(jnp.int32, (tk, tq), 1) - lax.broadcasted_iota(jnp.int32, (tk, tq), 0))

    def group(gi, carry, masked):
        ms, ls = list(carry[0]), list(carry[1])
        blocks = [gi * unroll + u for u in range(unroll)]
        kbs = [k_ref[pl.ds(pl.multiple_of(b * tk, tk), tk), :] for b in blocks]
        sts = [[jnp.dot(kb, qc[c], preferred_element_type=F32) for c in range(2)] for kb in kbs]
        for u, b in enumerate(blocks):
            vtb = vt_ref[b]
            for c in range(2):
                st = sts[u][c]
                if masked:
                    st = jnp.where(rel >= u * tk, st, NEG)
                m_new = jnp.maximum(ms[c], jnp.max(st, axis=0, keepdims=True))
                alpha = jnp.exp2(ms[c] - m_new)
                p = jnp.exp2(st - m_new)
                ls[c] = alpha * ls[c] + jnp.sum(p, axis=0, keepdims=True)
                acc_scr[c] = alpha * acc_scr[c] + jnp.dot(vtb, p.astype(BF16), preferred_element_type=F32)
                ms[c] = m_new
        return tuple(ms), tuple(ls)

    m0 = jnp.full((1, tq), NEG, F32)
    l0 = jnp.zeros((1, tq), F32)
    carry = lax.fori_loop(0, qi, lambda g, c: group(g, c, False), ((m0, m0), (l0, l0)))
    _, ls = group(qi, carry, True)

    o = acc_scr[0] / ls[0] - _lam(lam4_ref) * (acc_scr[1] / ls[1])
    o = o * lax.rsqrt(jnp.mean(o * o, axis=0, keepdims=True) + EPS)
    o_ref[...] = (o * gsub_ref[...] * (1.0 - LAM_INIT)).T


def _attn_prompt(lam4, qb, kb, vb, gsub, nb, t, tq, tk):
    qt = qb.reshape(nb, t, AW).transpose(0, 2, 1)
    kb = kb.reshape(nb, t, AW)
    vt = vb.reshape(nb, t // tk, tk, AW).transpose(0, 1, 3, 2)
    gcol = jnp.broadcast_to(gsub.reshape(VD, 1), (VD, tq))
    out = pl.pallas_call(
        functools.partial(_attn_body, tq=tq, tk=tk),
        grid=(nb, HEADS, t // tq),
        in_specs=[pl.BlockSpec((4, QK), lambda b, h, i: (0, 0)),
                  pl.BlockSpec((None, VD, tq), lambda b, h, i: (b, h, i)),
                  pl.BlockSpec((None, t, VD), lambda b, h, i: (b, 0, h)),
                  pl.BlockSpec((None, t // tk, VD, tk), lambda b, h, i: (b, 0, h, 0)),
                  pl.BlockSpec((VD, tq), lambda b, h, i: (0, 0))],
        out_specs=pl.BlockSpec((None, tq, VD), lambda b, h, i: (b, i, h)),
        out_shape=jax.ShapeDtypeStruct((nb, t, AW), F32),
        scratch_shapes=[pltpu.VMEM((2, VD, tq), F32)],
        compiler_params=_cp("parallel", "parallel", "arbitrary"),
        name="attn_prompt",
    )(lam4, qt, kb, vt, gcol)
    return out.reshape(nb * t, AW)


def _sattn_body(pt_ref, lam4_ref, q_ref, *refs, gp):
    k_refs, v_refs = refs[:gp], refs[gp:2 * gp]
    kn_ref, vn_ref, gsub_ref, o_ref, m_scr, l_scr, acc_scr = refs[2 * gp:]
    j = pl.program_id(1)
    q = q_ref[...]
    first = lax.broadcasted_iota(jnp.int32, (HEADS, VD), 1) < QK
    qc = (jnp.where(first, q, 0.0), jnp.where(first, 0.0, q))

    @pl.when(j == 0)
    def _():
        m_scr[...] = jnp.full(m_scr.shape, NEG, F32)
        l_scr[...] = jnp.zeros(l_scr.shape, F32)
        acc_scr[...] = jnp.zeros(acc_scr.shape, F32)

    def update(kp, vp):
        for c in range(2):
            s = jnp.sum(kp * qc[c][None], axis=-1, keepdims=True)
            m_prev = m_scr[c]
            m_new = jnp.maximum(m_prev, jnp.max(s, axis=0))
            alpha = jnp.exp2(m_prev - m_new)
            p = jnp.exp2(s - m_new[None])
            l_scr[c] = alpha * l_scr[c] + jnp.sum(p, axis=0)
            acc_scr[c] = alpha * acc_scr[c] + jnp.sum(p * vp, axis=0)
            m_scr[c] = m_new

    kc = 32
    for g in range(gp):
        for i in range(PAGE // kc):
            update(k_refs[g][i * kc:(i + 1) * kc], v_refs[g][i * kc:(i + 1) * kc])

    @pl.when(j == pl.num_programs(1) - 1)
    def _():
        update(kn_ref[...], vn_ref[...])
        o = acc_scr[0] / l_scr[0] - _lam(lam4_ref) * (acc_scr[1] / l_scr[1])
        o_ref[...] = _subnorm(o, gsub_ref[...])


def _attn_sample(page_table, lam4, q, cache_k, cache_v, kn, vn, gsub, gp):
    bs, n_pages = page_table.shape
    pt = page_table.reshape(-1).astype(jnp.int32)

    def page(g):
        return pl.BlockSpec((None, None, PAGE, HEADS, VD),
                            lambda b, j, pt_ref: (0, pt_ref[b * n_pages + j * gp + g], 0, 0, 0))

    per_b = lambda b, j, p: (b, 0, 0, 0)
    grid_spec = pltpu.PrefetchScalarGridSpec(
        num_scalar_prefetch=1,
        grid=(bs, n_pages // gp),
        in_specs=[pl.BlockSpec((4, QK), lambda b, j, p: (0, 0)),
                  pl.BlockSpec((None, HEADS, VD), lambda b, j, p: (b, 0, 0))]
                 + [page(g) for g in range(gp)] + [page(g) for g in range(gp)]
                 + [pl.BlockSpec((None, 1, HEADS, VD), per_b),
                    pl.BlockSpec((None, 1, HEADS, VD), per_b),
                    pl.BlockSpec((1, VD), lambda b, j, p: (0, 0))],
        out_specs=pl.BlockSpec((None, HEADS, VD), lambda b, j, p: (b, 0, 0)),
        scratch_shapes=[pltpu.VMEM((2, HEADS, VD), F32), pltpu.VMEM((2, HEADS, VD), F32),
                        pltpu.VMEM((2, HEADS, VD), F32)],
    )
    out = pl.pallas_call(
        functools.partial(_sattn_body, gp=gp),
        grid_spec=grid_spec,
        out_shape=jax.ShapeDtypeStruct((bs, HEADS, VD), F32),
        compiler_params=_cp("parallel", "arbitrary"),
        name="attn_sample",
    )(pt, lam4, q, *([cache_k] * gp), *([cache_v] * gp), kn, vn, gsub)
    return out.reshape(bs, AW)


def _cmul(ar, ai, br, bi):
    return ar * br - ai * bi, ar * bi + ai * br


def _s5_prep_body(are_ref, aim_ref, ldt_ref, arer_ref, aimr_ref, ldtr_ref, bre_ref, bim_ref,
                  pow_ref, bbr_ref, bbi_ref):
    def disc(are, aim, ldt):
        dt = jnp.exp(ldt)
        mag = jnp.exp(are * dt)
        lr = mag * jnp.cos(aim * dt)
        li = mag * jnp.sin(aim * dt)
        return lr, li

    lr, li = disc(are_ref[...], aim_ref[...], ldt_ref[...])
    pr, pi = lr, li
    for n in range(8):
        pow_ref[2 * n] = pr
        pow_ref[2 * n + 1] = pi
        pr, pi = _cmul(pr, pi, lr, li)

    are, aim = arer_ref[...], aimr_ref[...]
    lrr, lir = disc(are, aim, ldtr_ref[...])
    den = are * are + aim * aim
    zr = lrr - 1.0
    fr = (zr * are + lir * aim) / den
    fi = (lir * are - zr * aim) / den
    br, bi = bre_ref[...], bim_ref[...]
    bbr_ref[...] = fr * br - fi * bi
    bbi_ref[...] = fr * bi + fi * br


def _s5_prep(a_re, a_im, log_dt, b_re, b_im):
    rep = lambda a: jnp.repeat(a, SSM_CH, axis=1)
    ldt = jnp.broadcast_to(log_dt[:, None], (SSM_G, SSM_P))
    pw, bbr, bbi = pl.pallas_call(
        _s5_prep_body,
        out_shape=[jax.ShapeDtypeStruct((16, SSM_G, SSM_P), F32),
                   jax.ShapeDtypeStruct((SSM_G, SSM_P * SSM_CH), F32),
                   jax.ShapeDtypeStruct((SSM_G, SSM_P * SSM_CH), F32)],
        name="s5_prep",
    )(a_re, a_im, ldt, rep(a_re), rep(a_im), rep(ldt),
      b_re.reshape(SSM_G, -1), b_im.reshape(SSM_G, -1))
    return pw, bbr.reshape(SSM_G, SSM_P, SSM_CH), bbi.reshape(SSM_G, SSM_P, SSM_CH)


def _s5_tables(pw, bbr, bbi, c_re, c_im, d):
    eye = jnp.eye(GB, dtype=F32)
    def blk_in(bb):
        x = bb.reshape(NB, GB, SSM_P, SSM_CH)
        return jnp.einsum('ngpc,gh->ngchp', x, eye).reshape(NB, UL, SL)
    bblk = jnp.concatenate([blk_in(bbr), blk_in(bbi)], axis=2).astype(BF16)
    def blk_out(cc):
        x = cc.reshape(NB, GB, SSM_CH, SSM_P)
        return jnp.einsum('ngcp,gh->ngphc', x, eye).reshape(NB, SL, UL)
    cblk = jnp.concatenate([blk_out(c_re), blk_out(-c_im)], axis=1).astype(BF16)
    lanes = lambda a: a.reshape(NB, 1, SL)
    t_idx = jnp.arange(8)[None, :, None]
    tabs = []
    for n in (1, 2, 4):
        keep = (t_idx >= n).astype(F32)
        tabs += [lanes(pw[2 * (n - 1)]) * keep, lanes(pw[2 * (n - 1) + 1]) * keep]
    pr = jnp.stack([pw[2 * n] for n in range(8)], axis=0).reshape(8, NB, SL).transpose(1, 0, 2)
    pi = jnp.stack([pw[2 * n + 1] for n in range(8)], axis=0).reshape(8, NB, SL).transpose(1, 0, 2)
    tabs += [pr, pi]
    tab = jnp.stack(tabs, axis=1)
    lam1 = jnp.stack([lanes(pw[0]), lanes(pw[1])], axis=1).reshape(NB, 2, SL)
    dvec = d.reshape(NB, 1, UL)
    return bblk, cblk, tab, lam1, dvec


def _s5_body(u_ref, bblk_ref, cblk_ref, d_ref, tab_ref, h0r_ref, h0i_ref,
             y_ref, hr_ref, hi_ref, st_scr, cr_scr, ci_scr, *, chunk):
    k = pl.program_id(2)

    @pl.when(k == 0)
    def _():
        cr_scr[...] = h0r_ref[...]
        ci_scr[...] = h0i_ref[...]

    u = u_ref[...]
    st_scr[...] = jnp.dot(u.astype(BF16), bblk_ref[...], preferred_element_type=F32)

    def tile(r, carry):
        cr, ci = carry
        rows = pl.ds(pl.multiple_of(r * 8, 8), 8)
        xr = st_scr[rows, 0:SL]
        xi = st_scr[rows, SL:2 * SL]
        for lvl, sh in enumerate((1, 2, 4)):
            ar, ai = tab_ref[2 * lvl], tab_ref[2 * lvl + 1]
            sr, si = pltpu.roll(xr, sh, 0), pltpu.roll(xi, sh, 0)
            xr, xi = xr + (ar * sr - ai * si), xi + (ar * si + ai * sr)
        pr, pi = tab_ref[6], tab_ref[7]
        hr = xr + (pr * cr - pi * ci)
        hi = xi + (pr * ci + pi * cr)
        st_scr[rows, 0:SL] = hr
        st_scr[rows, SL:2 * SL] = hi
        return hr[7:8, :], hi[7:8, :]

    cr, ci = lax.fori_loop(0, chunk // 8, tile, (cr_scr[...], ci_scr[...]), unroll=2)
    cr_scr[...] = cr
    ci_scr[...] = ci
    y_ref[...] = (jnp.dot(st_scr[...].astype(BF16), cblk_ref[...], preferred_element_type=F32)
                  + d_ref[...] * u)

    @pl.when(k == pl.num_programs(2) - 1)
    def _():
        hr_ref[...] = cr
        hi_ref[...] = ci


def _s5_prompt(u, bblk, cblk, dvec, tab, h0r, h0i, nb, t, chunk):
    nk = t // chunk
    st = jax.ShapeDtypeStruct((nb, 1, SSM_G * SSM_P), F32)
    y, hr, hi = pl.pallas_call(
        functools.partial(_s5_body, chunk=chunk),
        grid=(nb, NB, nk),
        in_specs=[pl.BlockSpec((chunk, UL), lambda b, n, k: (b * nk + k, n)),
                  pl.BlockSpec((None, UL, 2 * SL), lambda b, n, k: (n, 0, 0)),
                  pl.BlockSpec((None, 2 * SL, UL), lambda b, n, k: (n, 0, 0)),
                  pl.BlockSpec((None, 1, UL), lambda b, n, k: (n, 0, 0)),
                  pl.BlockSpec((None, 8, 8, SL), lambda b, n, k: (n, 0, 0, 0)),
                  pl.BlockSpec((None, 1, SL), lambda b, n, k: (b, 0, n)),
                  pl.BlockSpec((None, 1, SL), lambda b, n, k: (b, 0, n))],
        out_specs=[pl.BlockSpec((chunk, UL), lambda b, n, k: (b * nk + k, n)),
                   pl.BlockSpec((None, 1, SL), lambda b, n, k: (b, 0, n)),
                   pl.BlockSpec((None, 1, SL), lambda b, n, k: (b, 0, n))],
        out_shape=[jax.ShapeDtypeStruct((nb * t, SSM_W), F32), st, st],
        scratch_shapes=[pltpu.VMEM((chunk, 2 * SL), F32), pltpu.VMEM((1, SL), F32),
                        pltpu.VMEM((1, SL), F32)],
        compiler_params=_cp("parallel", "parallel", "arbitrary"),
        name="s5_prompt",
    )(u, bblk, cblk, dvec, tab, h0r.reshape(nb, 1, -1), h0i.reshape(nb, 1, -1))
    return y, hr.reshape(nb, SSM_G, SSM_P), hi.reshape(nb, SSM_G, SSM_P)


def _s5_step_body(u_ref, bblk_ref, cblk_ref, d_ref, lam_ref, h0r_ref, h0i_ref, y_ref, hr_ref, hi_ref):
    u = u_ref[...]
    bu = jnp.dot(u.astype(BF16), bblk_ref[...], preferred_element_type=F32)
    lr, li = lam_ref[0:1, :], lam_ref[1:2, :]
    h0r, h0i = h0r_ref[...], h0i_ref[...]
    hr = bu[:, 0:SL] + (lr * h0r - li * h0i)
    hi = bu[:, SL:2 * SL] + (lr * h0i + li * h0r)
    hr_ref[...] = hr
    hi_ref[...] = hi
    hcat = jnp.concatenate([hr, hi], axis=1).astype(BF16)
    y_ref[...] = jnp.dot(hcat, cblk_ref[...], preferred_element_type=F32) + d_ref[...] * u


def _s5_sample(u, bblk, cblk, dvec, lam1, h0r, h0i):
    rows = u.shape[0]
    st = jax.ShapeDtypeStruct((rows, SSM_G * SSM_P), F32)
    return pl.pallas_call(
        _s5_step_body,
        grid=(NB,),
        in_specs=[pl.BlockSpec((rows, UL), lambda n: (0, n)),
                  pl.BlockSpec((None, UL, 2 * SL), lambda n: (n, 0, 0)),
                  pl.BlockSpec((None, 2 * SL, UL), lambda n: (n, 0, 0)),
                  pl.BlockSpec((None, 1, UL), lambda n: (n, 0, 0)),
                  pl.BlockSpec((None, 2, SL), lambda n: (n, 0, 0)),
                  pl.BlockSpec((rows, SL), lambda n: (0, n)),
                  pl.BlockSpec((rows, SL), lambda n: (0, n))],
        out_specs=[pl.BlockSpec((rows, UL), lambda n: (0, n)),
                   pl.BlockSpec((rows, SL), lambda n: (0, n)),
                   pl.BlockSpec((rows, SL), lambda n: (0, n))],
        out_shape=[jax.ShapeDtypeStruct((rows, SSM_W), F32), st, st],
        compiler_params=_cp("parallel"),
        name="s5_sample",
    )(u, bblk, cblk, dvec, lam1, h0r, h0i)


def _outproj_body(x_ref, a_ref, y_ref, wg_ref, bg_ref, wo_ref, g_ref, x2_ref, h2_ref, *, transposed):
    y = jax.nn.gelu(y_ref[...])
    gate = jnp.dot(y.astype(BF16), wg_ref[...], preferred_element_type=F32) + bg_ref[...]
    y = y * jax.nn.sigmoid(gate)
    mix = (jnp.dot(a_ref[...].astype(BF16), wo_ref[0:AW, :], preferred_element_type=F32)
           + jnp.dot(y.astype(BF16), wo_ref[AW:, :], preferred_element_type=F32))
    x2 = x_ref[...] + mix
    x2_ref[...] = x2
    ms = jnp.mean(x2 * x2, axis=-1, keepdims=True)
    h2 = x2 * lax.rsqrt(ms + EPS) * g_ref[...]
    h2_ref[...] = (h2.T if transposed else h2).astype(BF16)


def _outproj(x, attn, y, wg_bf, bg, wo_bf, g, tm, transposed):
    m = x.shape[0]
    row = lambda i: (i, 0)
    fix = lambda i: (0, 0)
    if transposed:
        h2_spec, h2_shape = pl.BlockSpec((D_MODEL, tm), lambda i: (0, i)), (D_MODEL, m)
    else:
        h2_spec, h2_shape = pl.BlockSpec((tm, D_MODEL), row), (m, D_MODEL)
    return pl.pallas_call(
        functools.partial(_outproj_body, transposed=transposed),
        grid=(m // tm,),
        in_specs=[pl.BlockSpec((tm, D_MODEL), row), pl.BlockSpec((tm, AW), row),
                  pl.BlockSpec((tm, SSM_W), row), pl.BlockSpec((SSM_W, SSM_W), fix),
                  pl.BlockSpec((1, SSM_W), fix), pl.BlockSpec((D_MODEL, D_MODEL), fix),
                  pl.BlockSpec((1, D_MODEL), fix)],
        out_specs=[pl.BlockSpec((tm, D_MODEL), row), h2_spec],
        out_shape=[jax.ShapeDtypeStruct((m, D_MODEL), F32), jax.ShapeDtypeStruct(h2_shape, BF16)],
        compiler_params=_cp("parallel"),
        name="outproj",
    )(x, attn, y, wg_bf, bg, wo_bf, g)


def _topk_body(h_ref, wq_ref, keys_ref, cnt_ref, w1_ref, rank_ref, e2_ref, q_scr, cand_scr, *, tb):
    q_scr[...] = jnp.dot(wq_ref[...], h_ref[...], preferred_element_type=F32).astype(BF16)
    ninf = -jnp.inf

    def head(h, carry):
        for lc in range(tb // 128):
            ls = slice(lc * 128, (lc + 1) * 128)
            sc = []
            for c in range(2):
                hc = 2 * h + c
                qrows = pl.ds(pl.multiple_of(hc * P_HALF, P_HALF), P_HALF)
                sc.append(jnp.dot(keys_ref[hc], q_scr[qrows, ls],
                                  preferred_element_type=F32))
            v1, v2 = [], []
            w = sc[0]
            for _ in range(P_TOPK):
                mx = jnp.max(w, axis=0, keepdims=True)
                v1.append(mx)
                w = jnp.where(w == mx, ninf, w)
            w = sc[1]
            rank = jnp.full(w.shape, float(P_TOPK), F32)
            for r in range(P_TOPK):
                mx = jnp.max(w, axis=0, keepdims=True)
                v2.append(mx)
                hit = w == mx
                rank = jnp.where(hit, float(r), rank)
                w = jnp.where(hit, ninf, w)
            cand_scr[...] = jnp.full(cand_scr.shape, ninf, F32)
            for n, (i, j) in enumerate(STAIR):
                cand_scr[n:n + 1, :] = v1[i] + v2[j]
            w = cand_scr[...]
            top = v1[0] + v2[0]
            z = jnp.zeros_like(top)
            tau = top
            for _ in range(P_TOPK):
                tau = jnp.max(w, axis=0, keepdims=True)
                z = z + jnp.exp(tau - top)
                w = jnp.where(w == tau, ninf, w)
            cnt = jnp.zeros(sc[0].shape, F32)
            for stride in (8, 4, 2, 1):
                probe = None
                for base in range(0, P_TOPK, 2 * stride):
                    row = v2[base + stride - 1]
                    probe = row if probe is None else jnp.where(cnt >= float(base), row, probe)
                cnt = cnt + jnp.where(sc[0] + probe >= tau, float(stride), 0.0)
            cnt = cnt + jnp.where(sc[0] + v2[P_TOPK - 1] >= tau, 1.0, 0.0)
            rows = pl.ds(pl.multiple_of(h * P_NK, P_NK), P_NK)
            cnt_ref[rows, ls] = cnt
            w1_ref[rows, ls] = jnp.exp(sc[0] - v1[0]) / z
            rank_ref[rows, ls] = rank
            e2_ref[rows, ls] = jnp.exp(sc[1] - v2[0])
        return carry

    lax.fori_loop(0, P_HEADS, head, 0)


def _peer_topk(h2t, wqt_bf, keys_bf, tb):
    m = h2t.shape[1]
    f32 = jax.ShapeDtypeStruct((P_HEADS * P_NK, m), F32)
    col = lambda i: (0, i)
    return pl.pallas_call(
        functools.partial(_topk_body, tb=tb),
        grid=(m // tb,),
        in_specs=[pl.BlockSpec((D_MODEL, tb), col),
                  pl.BlockSpec((P_HEADS * 2 * P_HALF, D_MODEL), lambda i: (0, 0)),
                  pl.BlockSpec((P_HEADS * 2, P_NK, P_HALF), lambda i: (0, 0, 0))],
        out_specs=[pl.BlockSpec((P_HEADS * P_NK, tb), col)] * 4,
        out_shape=[f32, f32, f32, f32],
        scratch_shapes=[pltpu.VMEM((P_HEADS * 2 * P_HALF, tb), BF16), pltpu.VMEM((STAIR_ROWS, 128), F32)],
        compiler_params=_cp("parallel"),
        name="peer_topk",
    )(h2t, wqt_bf, keys_bf)


def _peer_body(h_ref, u_ref, vt_ref, cnt_ref, w1_ref, rank_in, e2_in, o_ref,
               a_scr, wm_scr, rank_ref, e2_ref, *, tb, na):
    step = pl.program_id(1)

    @pl.when(step == 0)
    def _():
        o_ref[...] = jnp.zeros(o_ref.shape, F32)
        rank_ref[...] = rank_in[...].astype(BF16)
        e2_ref[...] = e2_in[...].astype(BF16)

    a_scr[...] = jnp.dot(u_ref[...], h_ref[...], preferred_element_type=F32)

    def row_tile(ref, al, h, ls):
        packed = jnp.broadcast_to(ref[h, al:al + 1, ls], (16, 128)).astype(BF16)
        return jnp.tile(packed, (P_NK // 16, 1))

    for al in range(na):
        arows = slice(al * P_NK, (al + 1) * P_NK)
        for lc in range(tb // 128):
            ls = slice(lc * 128, (lc + 1) * 128)
            g = jnp.zeros((P_NK, 128), BF16)
            for h in range(P_HEADS):
                rows = slice(h * P_NK, (h + 1) * P_NK)
                sel = rank_ref[rows, ls] < row_tile(cnt_ref, al, h, ls)
                g = g + jnp.where(sel, e2_ref[rows, ls] * row_tile(w1_ref, al, h, ls), jnp.zeros((), BF16))
            act = jax.nn.gelu(a_scr[arows, ls])
            wm_scr[arows, ls] = (act * g.astype(F32)).astype(BF16)
    o_ref[...] += jnp.dot(vt_ref[...], wm_scr[...], preferred_element_type=F32)


def _peer_dense(h2t, u_bf, vt_bf, cnt, w1, rank, e2, tb, na):
    m = h2t.shape[1]
    col = lambda i, k: (0, i)
    return pl.pallas_call(
        functools.partial(_peer_body, tb=tb, na=na),
        grid=(m // tb, P_NK // na),
        in_specs=[pl.BlockSpec((D_MODEL, tb), col),
                  pl.BlockSpec((na * P_NK, D_MODEL), lambda i, k: (k, 0)),
                  pl.BlockSpec((None, D_MODEL, na * P_NK), lambda i, k: (k, 0, 0)),
                  pl.BlockSpec((P_HEADS, na, tb), lambda i, k: (0, k, i)),
                  pl.BlockSpec((P_HEADS, na, tb), lambda i, k: (0, k, i)),
                  pl.BlockSpec((P_HEADS * P_NK, tb), col),
                  pl.BlockSpec((P_HEADS * P_NK, tb), col)],
        out_specs=pl.BlockSpec((D_MODEL, tb), col),
        out_shape=jax.ShapeDtypeStruct((D_MODEL, m), F32),
        scratch_shapes=[pltpu.VMEM((na * P_NK, tb), F32), pltpu.VMEM((na * P_NK, tb), BF16),
                        pltpu.VMEM((P_HEADS * P_NK, tb), BF16), pltpu.VMEM((P_HEADS * P_NK, tb), BF16)],
        compiler_params=_cp("parallel", "arbitrary"),
        name="peer_dense",
    )(h2t, u_bf, vt_bf, cnt, w1, rank, e2)


def _final_body(x_ref, pt_ref, g_ref, y_ref):
    tm = x_ref.shape[0]
    x = x_ref[...] + pt_ref[...].T[:tm]
    ms = jnp.mean(x * x, axis=-1, keepdims=True)
    y_ref[...] = x * lax.rsqrt(ms + EPS) * g_ref[...]


def _final(x2, pt, g, tm):
    m = x2.shape[0]
    row = lambda i: (i, 0)
    return pl.pallas_call(
        _final_body,
        grid=(m // tm,),
        in_specs=[pl.BlockSpec((tm, D_MODEL), row),
                  pl.BlockSpec((D_MODEL, max(tm, 128)), lambda i: (0, i)),
                  pl.BlockSpec((1, D_MODEL), lambda i: (0, 0))],
        out_specs=pl.BlockSpec((tm, D_MODEL), row),
        out_shape=jax.ShapeDtypeStruct((m, D_MODEL), F32),
        compiler_params=_cp("parallel"),
        name="final_norm",
    )(x2, pt, g)


def _rope_tables(pos):
    half = QK // 2
    inv = 1.0 / (ROPE_THETA ** (jnp.arange(half, dtype=F32) * 2.0 / QK))
    ang = pos.astype(F32)[:, None] * inv[None, :]
    c, s = jnp.cos(ang), jnp.sin(ang)
    cos = jnp.concatenate([c, c, c, c], axis=1)
    sin = jnp.concatenate([-s, s, -s, s], axis=1)
    return cos, sin


def _peer(h2t, wqt_bf, keys_bf, u_bf, vt_bf, tb, na):
    cnt, w1, rank, e2 = _peer_topk(h2t, wqt_bf, keys_bf, min(tb, 256))
    per_head = lambda a: a.reshape(P_HEADS, P_NK, -1)
    return _peer_dense(h2t, u_bf, vt_bf, per_head(cnt), per_head(w1), rank, e2, tb, na)


def kernel(x_prompt, x_sample, cache_k, cache_v, state_ssm_re, state_ssm_im, page_table, g_mix, w_in, lambda_q1, lambda_k1, lambda_q2, lambda_k2, g_sub, ssm_a_re, ssm_a_im, ssm_log_dt, ssm_b_re, ssm_b_im, ssm_c_re, ssm_c_im, ssm_d, w_glu, b_glu, w_out, g_ffn, peer_w_q, peer_sub_keys, peer_u, peer_v, g_final):
    bp, tp, _ = x_prompt.shape
    bs, ts, _ = x_sample.shape
    assert ts == 1 and g_mix.shape[0] == 1
    n_pages = page_table.shape[1]
    past_len = n_pages * PAGE
    mp = bp * tp
    srows = 16

    w_in_bf = w_in[0].astype(BF16)
    wg_bf = w_glu[0].astype(BF16)
    wo_bf = w_out[0].astype(BF16)
    wqt_bf = peer_w_q[0].T.astype(BF16)
    keys_bf = peer_sub_keys[0].reshape(P_HEADS * 2, P_NK, P_HALF).astype(BF16)
    u_bf = peer_u[0].astype(BF16)
    vt_bf = peer_v[0].reshape(P_NK // PEER_NA, PEER_NA * P_NK, D_MODEL).transpose(0, 2, 1).astype(BF16)
    gmix = g_mix[0].reshape(1, -1)
    gffn = g_ffn[0].reshape(1, -1)
    gfin = g_final.reshape(1, -1)
    gsub = g_sub[0].reshape(1, -1)
    bg = b_glu[0].reshape(1, -1)
    lam4 = jnp.stack([lambda_q1[0], lambda_k1[0], lambda_q2[0], lambda_k2[0]], axis=0)
    pw, bbr, bbi = _s5_prep(ssm_a_re[0], ssm_a_im[0], ssm_log_dt[0], ssm_b_re[0], ssm_b_im[0])
    bblk, cblk, tab, lam1, dvec = _s5_tables(pw, bbr, bbi, ssm_c_re[0], ssm_c_im[0], ssm_d[0])

    xp = x_prompt.reshape(mp, D_MODEL)
    cos_p, sin_p = _rope_tables(jnp.tile(jnp.arange(tp), bp))
    qb, kp, kpb, vp, vpb, up = _inproj(xp, gmix, w_in_bf, cos_p, sin_p, 512)
    attn_p = _attn_prompt(lam4, qb, kpb, vpb, gsub, bp, tp, 512, 128)
    zeros = jnp.zeros((bp, SSM_G * SSM_P), F32)
    yp, rp, ip = _s5_prompt(up, bblk, cblk, dvec, tab, zeros, zeros, bp, tp, 512)
    x2p, h2pt = _outproj(xp, attn_p, yp, wg_bf, bg, wo_bf, gffn, 256, True)
    peer_p = _peer(h2pt, wqt_bf, keys_bf, u_bf, vt_bf, 512, PEER_NA)
    y_prompt = _final(x2p, peer_p, gfin, 512).reshape(bp, tp, D_MODEL)

    xs = jnp.pad(x_sample.reshape(bs, D_MODEL), ((0, srows - bs), (0, 0)))
    cos_s, sin_s = _rope_tables(jnp.full((srows,), past_len))
    qs, ks, _, vs, _, us = _inproj(xs, gmix, w_in_bf, cos_s, sin_s, srows)
    heads = lambda a: a[:bs].astype(F32).reshape(bs, 1, HEADS, VD)
    attn_s = _attn_sample(page_table, lam4, heads(qs)[:, 0], cache_k, cache_v, heads(ks), heads(vs),
                          gsub, 4 if n_pages % 4 == 0 else 1)
    pad_s = lambda a: jnp.pad(a.reshape(bs, -1), ((0, srows - bs), (0, 0)))
    ys, rs, is_ = _s5_sample(us, bblk, cblk, dvec, lam1, pad_s(state_ssm_re[0]), pad_s(state_ssm_im[0]))
    x2s, h2s = _outproj(xs, pad_s(attn_s), ys, wg_bf, bg, wo_bf, gffn, srows, False)
    h2s_rep = jnp.tile(h2s[:bs], (128 // bs, 1))
    peer_s = _peer(h2s_rep.T, wqt_bf, keys_bf, u_bf, vt_bf, 128, PEER_NA)
    y_sample = _final(x2s, peer_s, gfin, srows)[:bs].reshape(bs, ts, D_MODEL)

    return (y_prompt, y_sample,
            kp.reshape(1, bp, tp, HEADS, 2 * QK), vp.reshape(1, bp, tp, HEADS, VD),
            rp[None], ip[None],
            ks[:bs].reshape(1, bs, ts, HEADS, 2 * QK), vs[:bs].reshape(1, bs, ts, HEADS, VD),
            rs[:bs].reshape(1, bs, SSM_G, SSM_P), is_[:bs].reshape(1, bs, SSM_G, SSM_P))
```

```python
import functools
import math

import jax
import jax.numpy as jnp
from jax import lax
from jax.experimental import pallas as pl
from jax.experimental.pallas import tpu as pltpu

F32 = jnp.float32
BF16 = jnp.bfloat16

D_MODEL = 2048
PAGE = 128
HEADS = 8
QK = 64
VD = 128
AW = HEADS * VD
SSM_W = D_MODEL - AW
SSM_CH = 16
SSM_G = SSM_W // SSM_CH
SSM_P = 64
GB = 16
NB = SSM_G // GB
SL = GB * SSM_P
UL = GB * SSM_CH
P_HEADS = 8
P_TOPK = 16
P_NK = 128
P_HALF = 128
P_E = P_NK * P_NK
EPS = 1e-6
NEG = -1e30
ROPE_THETA = 10000.0
LAM_INIT = 0.8 - 0.6 * math.exp(-0.3 * 0)
VMEM_LIMIT = 56 * 1024 * 1024
Q_SCALE = QK ** -0.5 * math.log2(math.e)
PEER_NA = 8

STAIR = [(i, j) for i in range(P_TOPK) for j in range(P_TOPK) if (i + 1) * (j + 1) <= P_TOPK]
STAIR_ROWS = -(-len(STAIR) // 8) * 8


def _cp(*sem):
    return pltpu.CompilerParams(dimension_semantics=sem, vmem_limit_bytes=VMEM_LIMIT)


def _inproj_body(x_ref, g_ref, w_ref, cos_ref, sin_ref,
                 q_ref, k_ref, kb_ref, v_ref, vb_ref, u_ref, h_scr):
    j = pl.program_id(1)

    @pl.when(j == 0)
    def _():
        x = x_ref[...]
        ms = jnp.mean(x * x, axis=-1, keepdims=True)
        h_scr[...] = (x * lax.rsqrt(ms + EPS) * g_ref[...]).astype(BF16)

    z = jnp.dot(h_scr[...], w_ref[...], preferred_element_type=F32)

    def rope(zc):
        lane = lax.broadcasted_iota(jnp.int32, zc.shape, 1)
        first = (lane & (QK - 1)) < (QK // 2)
        partner = jnp.where(first, pltpu.roll(zc, 128 - QK // 2, 1), pltpu.roll(zc, QK // 2, 1))
        return zc * cos_ref[...] + partner * sin_ref[...]

    @pl.when(j == 0)
    def _():
        for c in range(AW // 128):
            sl = slice(c * 128, (c + 1) * 128)
            q_ref[:, sl] = (rope(z[:, sl]) * Q_SCALE).astype(BF16)

    @pl.when(j == 1)
    def _():
        for c in range(AW // 128):
            sl = slice(c * 128, (c + 1) * 128)
            kc = rope(z[:, sl])
            k_ref[:, sl] = kc
            kb_ref[:, sl] = kc.astype(BF16)

    @pl.when(j == 2)
    def _():
        v_ref[...] = z
        vb_ref[...] = z.astype(BF16)

    @pl.when(j == 3)
    def _():
        u_ref[...] = z


def _inproj(x, g, w_bf, cos, sin, tm):
    m = x.shape[0]
    row = lambda i, j: (i, 0)
    outs = [jax.ShapeDtypeStruct((m, AW), BF16), jax.ShapeDtypeStruct((m, AW), F32),
            jax.ShapeDtypeStruct((m, AW), BF16), jax.ShapeDtypeStruct((m, AW), F32),
            jax.ShapeDtypeStruct((m, AW), BF16), jax.ShapeDtypeStruct((m, SSM_W), F32)]
    return pl.pallas_call(
        _inproj_body,
        grid=(m // tm, 4),
        in_specs=[pl.BlockSpec((tm, D_MODEL), row),
                  pl.BlockSpec((1, D_MODEL), lambda i, j: (0, 0)),
                  pl.BlockSpec((D_MODEL, AW), lambda i, j: (0, j)),
                  pl.BlockSpec((tm, 128), row),
                  pl.BlockSpec((tm, 128), row)],
        out_specs=[pl.BlockSpec((tm, AW), row) for _ in outs],
        out_shape=outs,
        scratch_shapes=[pltpu.VMEM((tm, D_MODEL), BF16)],
        compiler_params=_cp("parallel", "arbitrary"),
        name="inproj",
    )(x, g, w_bf, cos, sin)


def _lam(lam4_ref):
    l4 = lam4_ref[...]
    a = jnp.exp(jnp.sum(l4[0:1, :] * l4[1:2, :], axis=-1, keepdims=True))
    b = jnp.exp(jnp.sum(l4[2:3, :] * l4[3:4, :], axis=-1, keepdims=True))
    return a - b + LAM_INIT


def _subnorm(o, gsub):
    o = o * lax.rsqrt(jnp.mean(o * o, axis=-1, keepdims=True) + EPS)
    return o * gsub * (1.0 - LAM_INIT)


def _attn_body(lam4_ref, qt_ref, k_ref, vt_ref, gsub_ref, o_ref, acc_scr, *, tq, tk):
    qi = pl.program_id(2)
    unroll = tq // tk
    qt = qt_ref[...]
    row = lax.broadcasted_iota(jnp.int32, qt.shape, 0)
    zero = jnp.zeros_like(qt)
    qc = (jnp.where(row < QK, qt, zero), jnp.where(row >= QK, qt, zero))
    acc_scr[...] = jnp.zeros(acc_scr.shape, F32)
    rel = (lax.broadcasted_iota(jnp.int32, (tk, tq), 1) - lax.broadcasted_iota(jnp.int32, (tk, tq), 0))

    def group(gi, carry, masked):
        ms, ls = list(carry[0]), list(carry[1])
        blocks = [gi * unroll + u for u in range(unroll)]
        kbs = [k_ref[pl.ds(pl.multiple_of(b * tk, tk), tk), :] for b in blocks]
        sts = [[jnp.dot(kb, qc[c], preferred_element_type=F32) for c in range(2)] for kb in kbs]
        for u, b in enumerate(blocks):
            vtb = vt_ref[b]
            for c in range(2):
                st = sts[u][c]
                if masked:
                    st = jnp.where(rel >= u * tk, st, NEG)
                m_new = jnp.maximum(ms[c], jnp.max(st, axis=0, keepdims=True))
                alpha = jnp.exp2(ms[c] - m_new)
                p = jnp.exp2(st - m_new)
                ls[c] = alpha * ls[c] + jnp.sum(p, axis=0, keepdims=True)
                acc_scr[c] = alpha * acc_scr[c] + jnp.dot(vtb, p.astype(BF16), preferred_element_type=F32)
                ms[c] = m_new
        return tuple(ms), tuple(ls)

    m0 = jnp.full((1, tq), NEG, F32)
    l0 = jnp.zeros((1, tq), F32)
    carry = lax.fori_loop(0, qi, lambda g, c: group(g, c, False), ((m0, m0), (l0, l0)))
    _, ls = group(qi, carry, True)

    o = acc_scr[0] / ls[0] - _lam(lam4_ref) * (acc_scr[1] / ls[1])
    o = o * lax.rsqrt(jnp.mean(o * o, axis=0, keepdims=True) + EPS)
    o_ref[...] = (o * gsub_ref[...] * (1.0 - LAM_INIT)).T


def _attn_prompt(lam4, qb, kb, vb, gsub, nb, t, tq, tk):
    qt = qb.reshape(nb, t, AW).transpose(0, 2, 1)
    kb = kb.reshape(nb, t, AW)
    vt = vb.reshape(nb, t // tk, tk, AW).transpose(0, 1, 3, 2)
    gcol = jnp.broadcast_to(gsub.reshape(VD, 1), (VD, tq))
    out = pl.pallas_call(
        functools.partial(_attn_body, tq=tq, tk=tk),
        grid=(nb, HEADS, t // tq),
        in_specs=[pl.BlockSpec((4, QK), lambda b, h, i: (0, 0)),
                  pl.BlockSpec((None, VD, tq), lambda b, h, i: (b, h, i)),
                  pl.BlockSpec((None, t, VD), lambda b, h, i: (b, 0, h)),
                  pl.BlockSpec((None, t // tk, VD, tk), lambda b, h, i: (b, 0, h, 0)),
                  pl.BlockSpec((VD, tq), lambda b, h, i: (0, 0))],
        out_specs=pl.BlockSpec((None, tq, VD), lambda b, h, i: (b, i, h)),
        out_shape=jax.ShapeDtypeStruct((nb, t, AW), F32),
        scratch_shapes=[pltpu.VMEM((2, VD, tq), F32)],
        compiler_params=_cp("parallel", "parallel", "arbitrary"),
        name="attn_prompt",
    )(lam4, qt, kb, vt, gcol)
    return out.reshape(nb * t, AW)


def _sattn_body(pt_ref, lam4_ref, q_ref, *refs, gp):
    k_refs, v_refs = refs[:gp], refs[gp:2 * gp]
    kn_ref, vn_ref, gsub_ref, o_ref, m_scr, l_scr, acc_scr = refs[2 * gp:]
    j = pl.program_id(1)
    q = q_ref[...]
    first = lax.broadcasted_iota(jnp.int32, (HEADS, VD), 1) < QK
    qc = (jnp.where(first, q, 0.0), jnp.where(first, 0.0, q))

    @pl.when(j == 0)
    def _():
        m_scr[...] = jnp.full(m_scr.shape, NEG, F32)
        l_scr[...] = jnp.zeros(l_scr.shape, F32)
        acc_scr[...] = jnp.zeros(acc_scr.shape, F32)

    def update(kp, vp):
        for c in range(2):
            s = jnp.sum(kp * qc[c][None], axis=-1, keepdims=True)
            m_prev = m_scr[c]
            m_new = jnp.maximum(m_prev, jnp.max(s, axis=0))
            alpha = jnp.exp2(m_prev - m_new)
            p = jnp.exp2(s - m_new[None])
            l_scr[c] = alpha * l_scr[c] + jnp.sum(p, axis=0)
            acc_scr[c] = alpha * acc_scr[c] + jnp.sum(p * vp, axis=0)
            m_scr[c] = m_new

    kc = 32
    for g in range(gp):
        for i in range(PAGE // kc):
            update(k_refs[g][i * kc:(i + 1) * kc], v_refs[g][i * kc:(i + 1) * kc])

    @pl.when(j == pl.num_programs(1) - 1)
    def _():
        update(kn_ref[...], vn_ref[...])
        o = acc_scr[0] / l_scr[0] - _lam(lam4_ref) * (acc_scr[1] / l_scr[1])
        o_ref[...] = _subnorm(o, gsub_ref[...])


def _attn_sample(page_table, lam4, q, cache_k, cache_v, kn, vn, gsub, gp):
    bs, n_pages = page_table.shape
    pt = page_table.reshape(-1).astype(jnp.int32)

    def page(g):
        return pl.BlockSpec((None, None, PAGE, HEADS, VD),
                            lambda b, j, pt_ref: (0, pt_ref[b * n_pages + j * gp + g], 0, 0, 0))

    per_b = lambda b, j, p: (b, 0, 0, 0)
    grid_spec = pltpu.PrefetchScalarGridSpec(
        num_scalar_prefetch=1,
        grid=(bs, n_pages // gp),
        in_specs=[pl.BlockSpec((4, QK), lambda b, j, p: (0, 0)),
                  pl.BlockSpec((None, HEADS, VD), lambda b, j, p: (b, 0, 0))]
                 + [page(g) for g in range(gp)] + [page(g) for g in range(gp)]
                 + [pl.BlockSpec((None, 1, HEADS, VD), per_b),
                    pl.BlockSpec((None, 1, HEADS, VD), per_b),
                    pl.BlockSpec((1, VD), lambda b, j, p: (0, 0))],
        out_specs=pl.BlockSpec((None, HEADS, VD), lambda b, j, p: (b, 0, 0)),
        scratch_shapes=[pltpu.VMEM((2, HEADS, VD), F32), pltpu.VMEM((2, HEADS, VD), F32),
                        pltpu.VMEM((2, HEADS, VD), F32)],
    )
    out = pl.pallas_call(
        functools.partial(_sattn_body, gp=gp),
        grid_spec=grid_spec,
        out_shape=jax.ShapeDtypeStruct((bs, HEADS, VD), F32),
        compiler_params=_cp("parallel", "arbitrary"),
        name="attn_sample",
    )(pt, lam4, q, *([cache_k] * gp), *([cache_v] * gp), kn, vn, gsub)
    return out.reshape(bs, AW)


def _cmul(ar, ai, br, bi):
    return ar * br - ai * bi, ar * bi + ai * br


def _s5_prep_body(are_ref, aim_ref, ldt_ref, arer_ref, aimr_ref, ldtr_ref, bre_ref, bim_ref,
                  pow_ref, bbr_ref, bbi_ref):
    def disc(are, aim, ldt):
        dt = jnp.exp(ldt)
        mag = jnp.exp(are * dt)
        lr = mag * jnp.cos(aim * dt)
        li = mag * jnp.sin(aim * dt)
        return lr, li

    lr, li = disc(are_ref[...], aim_ref[...], ldt_ref[...])
    pr, pi = lr, li
    for n in range(8):
        pow_ref[2 * n] = pr
        pow_ref[2 * n + 1] = pi
        pr, pi = _cmul(pr, pi, lr, li)

    are, aim = arer_ref[...], aimr_ref[...]
    lrr, lir = disc(are, aim, ldtr_ref[...])
    den = are * are + aim * aim
    zr = lrr - 1.0
    fr = (zr * are + lir * aim) / den
    fi = (lir * are - zr * aim) / den
    br, bi = bre_ref[...], bim_ref[...]
    bbr_ref[...] = fr * br - fi * bi
    bbi_ref[...] = fr * bi + fi * br


def _s5_prep(a_re, a_im, log_dt, b_re, b_im):
    rep = lambda a: jnp.repeat(a, SSM_CH, axis=1)
    ldt = jnp.broadcast_to(log_dt[:, None], (SSM_G, SSM_P))
    pw, bbr, bbi = pl.pallas_call(
        _s5_prep_body,
        out_shape=[jax.ShapeDtypeStruct((16, SSM_G, SSM_P), F32),
                   jax.ShapeDtypeStruct((SSM_G, SSM_P * SSM_CH), F32),
                   jax.ShapeDtypeStruct((SSM_G, SSM_P * SSM_CH), F32)],
        name="s5_prep",
    )(a_re, a_im, ldt, rep(a_re), rep(a_im), rep(ldt),
      b_re.reshape(SSM_G, -1), b_im.reshape(SSM_G, -1))
    return pw, bbr.reshape(SSM_G, SSM_P, SSM_CH), bbi.reshape(SSM_G, SSM_P, SSM_CH)


def _s5_tables(pw, bbr, bbi, c_re, c_im, d):
    eye = jnp.eye(GB, dtype=F32)
    def blk_in(bb):
        x = bb.reshape(NB, GB, SSM_P, SSM_CH)
        return jnp.einsum('ngpc,gh->ngchp', x, eye).reshape(NB, UL, SL)
    bblk = jnp.concatenate([blk_in(bbr), blk_in(bbi)], axis=2).astype(BF16)
    def blk_out(cc):
        x = cc.reshape(NB, GB, SSM_CH, SSM_P)
        return jnp.einsum('ngcp,gh->ngphc', x, eye).reshape(NB, SL, UL)
    cblk = jnp.concatenate([blk_out(c_re), blk_out(-c_im)], axis=1).astype(BF16)
    lanes = lambda a: a.reshape(NB, 1, SL)
    t_idx = jnp.arange(8)[None, :, None]
    tabs = []
    for n in (1, 2, 4):
        keep = (t_idx >= n).astype(F32)
        tabs += [lanes(pw[2 * (n - 1)]) * keep, lanes(pw[2 * (n - 1) + 1]) * keep]
    pr = jnp.stack([pw[2 * n] for n in range(8)], axis=0).reshape(8, NB, SL).transpose(1, 0, 2)
    pi = jnp.stack([pw[2 * n + 1] for n in range(8)], axis=0).reshape(8, NB, SL).transpose(1, 0, 2)
    tabs += [pr, pi]
    tab = jnp.stack(tabs, axis=1)
    lam1 = jnp.stack([lanes(pw[0]), lanes(pw[1])], axis=1).reshape(NB, 2, SL)
    dvec = d.reshape(NB, 1, UL)
    return bblk, cblk, tab, lam1, dvec


def _s5_body(u_ref, bblk_ref, cblk_ref, d_ref, tab_ref, h0r_ref, h0i_ref,
             y_ref, hr_ref, hi_ref, st_scr, cr_scr, ci_scr, *, chunk):
    k = pl.program_id(2)

    @pl.when(k == 0)
    def _():
        cr_scr[...] = h0r_ref[...]
        ci_scr[...] = h0i_ref[...]

    u = u_ref[...]
    st_scr[...] = jnp.dot(u.astype(BF16), bblk_ref[...], preferred_element_type=F32)

    def tile(r, carry):
        cr, ci = carry
        rows = pl.ds(pl.multiple_of(r * 8, 8), 8)
        xr = st_scr[rows, 0:SL]
        xi = st_scr[rows, SL:2 * SL]
        for lvl, sh in enumerate((1, 2, 4)):
            ar, ai = tab_ref[2 * lvl], tab_ref[2 * lvl + 1]
            sr, si = pltpu.roll(xr, sh, 0), pltpu.roll(xi, sh, 0)
            xr, xi = xr + (ar * sr - ai * si), xi + (ar * si + ai * sr)
        pr, pi = tab_ref[6], tab_ref[7]
        hr = xr + (pr * cr - pi * ci)
        hi = xi + (pr * ci + pi * cr)
        st_scr[rows, 0:SL] = hr
        st_scr[rows, SL:2 * SL] = hi
        return hr[7:8, :], hi[7:8, :]

    cr, ci = lax.fori_loop(0, chunk // 8, tile, (cr_scr[...], ci_scr[...]), unroll=2)
    cr_scr[...] = cr
    ci_scr[...] = ci
    y_ref[...] = (jnp.dot(st_scr[...].astype(BF16), cblk_ref[...], preferred_element_type=F32)
                  + d_ref[...] * u)

    @pl.when(k == pl.num_programs(2) - 1)
    def _():
        hr_ref[...] = cr
        hi_ref[...] = ci


def _s5_prompt(u, bblk, cblk, dvec, tab, h0r, h0i, nb, t, chunk):
    nk = t // chunk
    st = jax.ShapeDtypeStruct((nb, 1, SSM_G * SSM_P), F32)
    y, hr, hi = pl.pallas_call(
        functools.partial(_s5_body, chunk=chunk),
        grid=(nb, NB, nk),
        in_specs=[pl.BlockSpec((chunk, UL), lambda b, n, k: (b * nk + k, n)),
                  pl.BlockSpec((None, UL, 2 * SL), lambda b, n, k: (n, 0, 0)),
                  pl.BlockSpec((None, 2 * SL, UL), lambda b, n, k: (n, 0, 0)),
                  pl.BlockSpec((None, 1, UL), lambda b, n, k: (n, 0, 0)),
                  pl.BlockSpec((None, 8, 8, SL), lambda b, n, k: (n, 0, 0, 0)),
                  pl.BlockSpec((None, 1, SL), lambda b, n, k: (b, 0, n)),
                  pl.BlockSpec((None, 1, SL), lambda b, n, k: (b, 0, n))],
        out_specs=[pl.BlockSpec((chunk, UL), lambda b, n, k: (b * nk + k, n)),
                   pl.BlockSpec((None, 1, SL), lambda b, n, k: (b, 0, n)),
                   pl.BlockSpec((None, 1, SL), lambda b, n, k: (b, 0, n))],
        out_shape=[jax.ShapeDtypeStruct((nb * t, SSM_W), F32), st, st],
        scratch_shapes=[pltpu.VMEM((chunk, 2 * SL), F32), pltpu.VMEM((1, SL), F32),
                        pltpu.VMEM((1, SL), F32)],
        compiler_params=_cp("parallel", "parallel", "arbitrary"),
        name="s5_prompt",
    )(u, bblk, cblk, dvec, tab, h0r.reshape(nb, 1, -1), h0i.reshape(nb, 1, -1))
    return y, hr.reshape(nb, SSM_G, SSM_P), hi.reshape(nb, SSM_G, SSM_P)


def _s5_step_body(u_ref, bblk_ref, cblk_ref, d_ref, lam_ref, h0r_ref, h0i_ref, y_ref, hr_ref, hi_ref):
    u = u_ref[...]
    bu = jnp.dot(u.astype(BF16), bblk_ref[...], preferred_element_type=F32)
    lr, li = lam_ref[0:1, :], lam_ref[1:2, :]
    h0r, h0i = h0r_ref[...], h0i_ref[...]
    hr = bu[:, 0:SL] + (lr * h0r - li * h0i)
    hi = bu[:, SL:2 * SL] + (lr * h0i + li * h0r)
    hr_ref[...] = hr
    hi_ref[...] = hi
    hcat = jnp.concatenate([hr, hi], axis=1).astype(BF16)
    y_ref[...] = jnp.dot(hcat, cblk_ref[...], preferred_element_type=F32) + d_ref[...] * u


def _s5_sample(u, bblk, cblk, dvec, lam1, h0r, h0i):
    rows = u.shape[0]
    st = jax.ShapeDtypeStruct((rows, SSM_G * SSM_P), F32)
    return pl.pallas_call(
        _s5_step_body,
        grid=(NB,),
        in_specs=[pl.BlockSpec((rows, UL), lambda n: (0, n)),
                  pl.BlockSpec((None, UL, 2 * SL), lambda n: (n, 0, 0)),
                  pl.BlockSpec((None, 2 * SL, UL), lambda n: (n, 0, 0)),
                  pl.BlockSpec((None, 1, UL), lambda n: (n, 0, 0)),
                  pl.BlockSpec((None, 2, SL), lambda n: (n, 0, 0)),
                  pl.BlockSpec((rows, SL), lambda n: (0, n)),
                  pl.BlockSpec((rows, SL), lambda n: (0, n))],
        out_specs=[pl.BlockSpec((rows, UL), lambda n: (0, n)),
                   pl.BlockSpec((rows, SL), lambda n: (0, n)),
                   pl.BlockSpec((rows, SL), lambda n: (0, n))],
        out_shape=[jax.ShapeDtypeStruct((rows, SSM_W), F32), st, st],
        compiler_params=_cp("parallel"),
        name="s5_sample",
    )(u, bblk, cblk, dvec, lam1, h0r, h0i)


def _outproj_body(x_ref, a_ref, y_ref, wg_ref, bg_ref, wo_ref, g_ref, x2_ref, h2_ref, *, transposed):
    y = jax.nn.gelu(y_ref[...])
    gate = jnp.dot(y.astype(BF16), wg_ref[...], preferred_element_type=F32) + bg_ref[...]
    y = y * jax.nn.sigmoid(gate)
    mix = (jnp.dot(a_ref[...].astype(BF16), wo_ref[0:AW, :], preferred_element_type=F32)
           + jnp.dot(y.astype(BF16), wo_ref[AW:, :], preferred_element_type=F32))
    x2 = x_ref[...] + mix
    x2_ref[...] = x2
    ms = jnp.mean(x2 * x2, axis=-1, keepdims=True)
    h2 = x2 * lax.rsqrt(ms + EPS) * g_ref[...]
    h2_ref[...] = (h2.T if transposed else h2).astype(BF16)


def _outproj(x, attn, y, wg_bf, bg, wo_bf, g, tm, transposed):
    m = x.shape[0]
    row = lambda i: (i, 0)
    fix = lambda i: (0, 0)
    if transposed:
        h2_spec, h2_shape = pl.BlockSpec((D_MODEL, tm), lambda i: (0, i)), (D_MODEL, m)
    else:
        h2_spec, h2_shape = pl.BlockSpec((tm, D_MODEL), row), (m, D_MODEL)
    return pl.pallas_call(
        functools.partial(_outproj_body, transposed=transposed),
        grid=(m // tm,),
        in_specs=[pl.BlockSpec((tm, D_MODEL), row), pl.BlockSpec((tm, AW), row),
                  pl.BlockSpec((tm, SSM_W), row), pl.BlockSpec((SSM_W, SSM_W), fix),
                  pl.BlockSpec((1, SSM_W), fix), pl.BlockSpec((D_MODEL, D_MODEL), fix),
                  pl.BlockSpec((1, D_MODEL), fix)],
        out_specs=[pl.BlockSpec((tm, D_MODEL), row), h2_spec],
        out_shape=[jax.ShapeDtypeStruct((m, D_MODEL), F32), jax.ShapeDtypeStruct(h2_shape, BF16)],
        compiler_params=_cp("parallel"),
        name="outproj",
    )(x, attn, y, wg_bf, bg, wo_bf, g)


def _topk_body(h_ref, wq_ref, keys_ref, cnt_ref, w1_ref, rank_ref, e2_ref, q_scr, cand_scr, *, tb):
    q_scr[...] = jnp.dot(wq_ref[...], h_ref[...], preferred_element_type=F32).astype(BF16)
    ninf = -jnp.inf

    def head(h, carry):
        for lc in range(tb // 128):
            ls = slice(lc * 128, (lc + 1) * 128)
            sc = []
            for c in range(2):
                hc = 2 * h + c
                qrows = pl.ds(pl.multiple_of(hc * P_HALF, P_HALF), P_HALF)
                sc.append(jnp.dot(keys_ref[hc], q_scr[qrows, ls],
                                  preferred_element_type=F32))
            v1, v2 = [], []
            w = sc[0]
            for _ in range(P_TOPK):
                mx = jnp.max(w, axis=0, keepdims=True)
                v1.append(mx)
                w = jnp.where(w == mx, ninf, w)
            w = sc[1]
            rank = jnp.full(w.shape, float(P_TOPK), F32)
            for r in range(P_TOPK):
                mx = jnp.max(w, axis=0, keepdims=True)
                v2.append(mx)
                hit = w == mx
                rank = jnp.where(hit, float(r), rank)
                w = jnp.where(hit, ninf, w)
            cand_scr[...] = jnp.full(cand_scr.shape, ninf, F32)
            for n, (i, j) in enumerate(STAIR):
                cand_scr[n:n + 1, :] = v1[i] + v2[j]
            w = cand_scr[...]
            top = v1[0] + v2[0]
            z = jnp.zeros_like(top)
            tau = top
            for _ in range(P_TOPK):
                tau = jnp.max(w, axis=0, keepdims=True)
                z = z + jnp.exp(tau - top)
                w = jnp.where(w == tau, ninf, w)
            cnt = jnp.zeros(sc[0].shape, F32)
            for stride in (8, 4, 2, 1):
                probe = None
                for base in range(0, P_TOPK, 2 * stride):
                    row = v2[base + stride - 1]
                    probe = row if probe is None else jnp.where(cnt >= float(base), row, probe)
                cnt = cnt + jnp.where(sc[0] + probe >= tau, float(stride), 0.0)
            cnt = cnt + jnp.where(sc[0] + v2[P_TOPK - 1] >= tau, 1.0, 0.0)
            rows = pl.ds(pl.multiple_of(h * P_NK, P_NK), P_NK)
            cnt_ref[rows, ls] = cnt
            w1_ref[rows, ls] = jnp.exp(sc[0] - v1[0]) / z
            rank_ref[rows, ls] = rank
            e2_ref[rows, ls] = jnp.exp(sc[1] - v2[0])
        return carry

    lax.fori_loop(0, P_HEADS, head, 0)


def _peer_topk(h2t, wqt_bf, keys_bf, tb):
    m = h2t.shape[1]
    f32 = jax.ShapeDtypeStruct((P_HEADS * P_NK, m), F32)
    col = lambda i: (0, i)
    return pl.pallas_call(
        functools.partial(_topk_body, tb=tb),
        grid=(m // tb,),
        in_specs=[pl.BlockSpec((D_MODEL, tb), col),
                  pl.BlockSpec((P_HEADS * 2 * P_HALF, D_MODEL), lambda i: (0, 0)),
                  pl.BlockSpec((P_HEADS * 2, P_NK, P_HALF), lambda i: (0, 0, 0))],
        out_specs=[pl.BlockSpec((P_HEADS * P_NK, tb), col)] * 4,
        out_shape=[f32, f32, f32, f32],
        scratch_shapes=[pltpu.VMEM((P_HEADS * 2 * P_HALF, tb), BF16), pltpu.VMEM((STAIR_ROWS, 128), F32)],
        compiler_params=_cp("parallel"),
        name="peer_topk",
    )(h2t, wqt_bf, keys_bf)


def _peer_body(h_ref, u_ref, vt_ref, cnt_ref, w1_ref, rank_in, e2_in, o_ref,
               a_scr, wm_scr, rank_ref, e2_ref, *, tb, na):
    step = pl.program_id(1)

    @pl.when(step == 0)
    def _():
        o_ref[...] = jnp.zeros(o_ref.shape, F32)
        rank_ref[...] = rank_in[...].astype(BF16)
        e2_ref[...] = e2_in[...].astype(BF16)

    a_scr[...] = jnp.dot(u_ref[...], h_ref[...], preferred_element_type=F32)

    def row_tile(ref, al, h, ls):
        packed = jnp.broadcast_to(ref[h, al:al + 1, ls], (16, 128)).astype(BF16)
        return jnp.tile(packed, (P_NK // 16, 1))

    for al in range(na):
        arows = slice(al * P_NK, (al + 1) * P_NK)
        for lc in range(tb // 128):
            ls = slice(lc * 128, (lc + 1) * 128)
            g = jnp.zeros((P_NK, 128), BF16)
            for h in range(P_HEADS):
                rows = slice(h * P_NK, (h + 1) * P_NK)
                sel = rank_ref[rows, ls] < row_tile(cnt_ref, al, h, ls)
                g = g + jnp.where(sel, e2_ref[rows, ls] * row_tile(w1_ref, al, h, ls), jnp.zeros((), BF16))
            act = jax.nn.gelu(a_scr[arows, ls])
            wm_scr[arows, ls] = (act * g.astype(F32)).astype(BF16)
    o_ref[...] += jnp.dot(vt_ref[...], wm_scr[...], preferred_element_type=F32)


def _peer_dense(h2t, u_bf, vt_bf, cnt, w1, rank, e2, tb, na):
    m = h2t.shape[1]
    col = lambda i, k: (0, i)
    return pl.pallas_call(
        functools.partial(_peer_body, tb=tb, na=na),
        grid=(m // tb, P_NK // na),
        in_specs=[pl.BlockSpec((D_MODEL, tb), col),
                  pl.BlockSpec((na * P_NK, D_MODEL), lambda i, k: (k, 0)),
                  pl.BlockSpec((None, D_MODEL, na * P_NK), lambda i, k: (k, 0, 0)),
                  pl.BlockSpec((P_HEADS, na, tb), lambda i, k: (0, k, i)),
                  pl.BlockSpec((P_HEADS, na, tb), lambda i, k: (0, k, i)),
                  pl.BlockSpec((P_HEADS * P_NK, tb), col),
                  pl.BlockSpec((P_HEADS * P_NK, tb), col)],
        out_specs=pl.BlockSpec((D_MODEL, tb), col),
        out_shape=jax.ShapeDtypeStruct((D_MODEL, m), F32),
        scratch_shapes=[pltpu.VMEM((na * P_NK, tb), F32), pltpu.VMEM((na * P_NK, tb), BF16),
                        pltpu.VMEM((P_HEADS * P_NK, tb), BF16), pltpu.VMEM((P_HEADS * P_NK, tb), BF16)],
        compiler_params=_cp("parallel", "arbitrary"),
        name="peer_dense",
    )(h2t, u_bf, vt_bf, cnt, w1, rank, e2)


def _final_body(x_ref, pt_ref, g_ref, y_ref):
    tm = x_ref.shape[0]
    x = x_ref[...] + pt_ref[...].T[:tm]
    ms = jnp.mean(x * x, axis=-1, keepdims=True)
    y_ref[...] = x * lax.rsqrt(ms + EPS) * g_ref[...]


def _final(x2, pt, g, tm):
    m = x2.shape[0]
    row = lambda i: (i, 0)
    return pl.pallas_call(
        _final_body,
        grid=(m // tm,),
        in_specs=[pl.BlockSpec((tm, D_MODEL), row),
                  pl.BlockSpec((D_MODEL, max(tm, 128)), lambda i: (0, i)),
                  pl.BlockSpec((1, D_MODEL), lambda i: (0, 0))],
        out_specs=pl.BlockSpec((tm, D_MODEL), row),
        out_shape=jax.ShapeDtypeStruct((m, D_MODEL), F32),
        compiler_params=_cp("parallel"),
        name="final_norm",
    )(x2, pt, g)


def _rope_tables(pos):
    half = QK // 2
    inv = 1.0 / (ROPE_THETA ** (jnp.arange(half, dtype=F32) * 2.0 / QK))
    ang = pos.astype(F32)[:, None] * inv[None, :]
    c, s = jnp.cos(ang), jnp.sin(ang)
    cos = jnp.concatenate([c, c, c, c], axis=1)
    sin = jnp.concatenate([-s, s, -s, s], axis=1)
    return cos, sin


def _peer(h2t, wqt_bf, keys_bf, u_bf, vt_bf, tb, na):
    cnt, w1, rank, e2 = _peer_topk(h2t, wqt_bf, keys_bf, min(tb, 256))
    per_head = lambda a: a.reshape(P_HEADS, P_NK, -1)
    return _peer_dense(h2t, u_bf, vt_bf, per_head(cnt), per_head(w1), rank, e2, tb, na)


def kernel(x_prompt, x_sample, cache_k, cache_v, state_ssm_re, state_ssm_im, page_table, g_mix, w_in, lambda_q1, lambda_k1, lambda_q2, lambda_k2, g_sub, ssm_a_re, ssm_a_im, ssm_log_dt, ssm_b_re, ssm_b_im, ssm_c_re, ssm_c_im, ssm_d, w_glu, b_glu, w_out, g_ffn, peer_w_q, peer_sub_keys, peer_u, peer_v, g_final):
    bp, tp, _ = x_prompt.shape
    bs, ts, _ = x_sample.shape
    assert ts == 1 and g_mix.shape[0] == 1
    n_pages = page_table.shape[1]
    past_len = n_pages * PAGE
    mp = bp * tp
    srows = 16

    w_in_bf = w_in[0].astype(BF16)
    wg_bf = w_glu[0].astype(BF16)
    wo_bf = w_out[0].astype(BF16)
    wqt_bf = peer_w_q[0].T.astype(BF16)
    keys_bf = peer_sub_keys[0].reshape(P_HEADS * 2, P_NK, P_HALF).astype(BF16)
    u_bf = peer_u[0].astype(BF16)
    vt_bf = peer_v[0].reshape(P_NK // PEER_NA, PEER_NA * P_NK, D_MODEL).transpose(0, 2, 1).astype(BF16)
    gmix = g_mix[0].reshape(1, -1)
    gffn = g_ffn[0].reshape(1, -1)
    gfin = g_final.reshape(1, -1)
    gsub = g_sub[0].reshape(1, -1)
    bg = b_glu[0].reshape(1, -1)
    lam4 = jnp.stack([lambda_q1[0], lambda_k1[0], lambda_q2[0], lambda_k2[0]], axis=0)
    pw, bbr, bbi = _s5_prep(ssm_a_re[0], ssm_a_im[0], ssm_log_dt[0], ssm_b_re[0], ssm_b_im[0])
    bblk, cblk, tab, lam1, dvec = _s5_tables(pw, bbr, bbi, ssm_c_re[0], ssm_c_im[0], ssm_d[0])

    xp = x_prompt.reshape(mp, D_MODEL)
    cos_p, sin_p = _rope_tables(jnp.tile(jnp.arange(tp), bp))
    qb, kp, kpb, vp, vpb, up = _inproj(xp, gmix, w_in_bf, cos_p, sin_p, 512)
    attn_p = _attn_prompt(lam4, qb, kpb, vpb, gsub, bp, tp, 512, 256)
    zeros = jnp.zeros((bp, SSM_G * SSM_P), F32)
    yp, rp, ip = _s5_prompt(up, bblk, cblk, dvec, tab, zeros, zeros, bp, tp, 512)
    x2p, h2pt = _outproj(xp, attn_p, yp, wg_bf, bg, wo_bf, gffn, 256, True)
    peer_p = _peer(h2pt, wqt_bf, keys_bf, u_bf, vt_bf, 512, PEER_NA)
    y_prompt = _final(x2p, peer_p, gfin, 512).reshape(bp, tp, D_MODEL)

    xs = jnp.pad(x_sample.reshape(bs, D_MODEL), ((0, srows - bs), (0, 0)))
    cos_s, sin_s = _rope_tables(jnp.full((srows,), past_len))
    qs, ks, _, vs, _, us = _inproj(xs, gmix, w_in_bf, cos_s, sin_s, srows)
    heads = lambda a: a[:bs].astype(F32).reshape(bs, 1, HEADS, VD)
    attn_s = _attn_sample(page_table, lam4, heads(qs)[:, 0], cache_k, cache_v, heads(ks), heads(vs),
                          gsub, 4 if n_pages % 4 == 0 else 1)
    pad_s = lambda a: jnp.pad(a.reshape(bs, -1), ((0, srows - bs), (0, 0)))
    ys, rs, is_ = _s5_sample(us, bblk, cblk, dvec, lam1, pad_s(state_ssm_re[0]), pad_s(state_ssm_im[0]))
    x2s, h2s = _outproj(xs, pad_s(attn_s), ys, wg_bf, bg, wo_bf, gffn, srows, False)
    h2s_rep = jnp.tile(h2s[:bs], (128 // bs, 1))
    peer_s = _peer(h2s_rep.T, wqt_bf, keys_bf, u_bf, vt_bf, 128, PEER_NA)
    y_sample = _final(x2s, peer_s, gfin, srows)[:bs].reshape(bs, ts, D_MODEL)

    return (y_prompt, y_sample,
            kp.reshape(1, bp, tp, HEADS, 2 * QK), vp.reshape(1, bp, tp, HEADS, VD),
            rp[None], ip[None],
            ks[:bs].reshape(1, bs, ts, HEADS, 2 * QK), vs[:bs].reshape(1, bs, ts, HEADS, VD),
            rs[:bs].reshape(1, bs, SSM_G, SSM_P), is_[:bs].reshape(1, bs, SSM_G, SSM_P))
```

```python
import functools
import math

import jax
import jax.numpy as jnp
from jax import lax
from jax.experimental import pallas as pl
from jax.experimental.pallas import tpu as pltpu

F32 = jnp.float32
BF16 = jnp.bfloat16

D_MODEL = 2048
PAGE = 128
HEADS = 8
QK = 64
VD = 128
AW = HEADS * VD
SSM_W = D_MODEL - AW
SSM_CH = 16
SSM_G = SSM_W // SSM_CH
SSM_P = 64
GB = 16
NB = SSM_G // GB
SL = GB * SSM_P
UL = GB * SSM_CH
P_HEADS = 8
P_TOPK = 16
P_NK = 128
P_HALF = 128
P_E = P_NK * P_NK
EPS = 1e-6
NEG = -1e30
ROPE_THETA = 10000.0
LAM_INIT = 0.8 - 0.6 * math.exp(-0.3 * 0)
VMEM_LIMIT = 56 * 1024 * 1024
Q_SCALE = QK ** -0.5 * math.log2(math.e)
PEER_NA = 8

STAIR = [(i, j) for i in range(P_TOPK) for j in range(P_TOPK) if (i + 1) * (j + 1) <= P_TOPK]
STAIR_ROWS = -(-len(STAIR) // 8) * 8


def _cp(*sem):
    return pltpu.CompilerParams(dimension_semantics=sem, vmem_limit_bytes=VMEM_LIMIT)


def _inproj_body(x_ref, g_ref, w_ref, cos_ref, sin_ref,
                 q_ref, k_ref, kb_ref, v_ref, vb_ref, u_ref, h_scr):
    j = pl.program_id(1)

    @pl.when(j == 0)
    def _():
        x = x_ref[...]
        ms = jnp.mean(x * x, axis=-1, keepdims=True)
        h_scr[...] = (x * lax.rsqrt(ms + EPS) * g_ref[...]).astype(BF16)

    z = jnp.dot(h_scr[...], w_ref[...], preferred_element_type=F32)

    def rope(zc):
        lane = lax.broadcasted_iota(jnp.int32, zc.shape, 1)
        first = (lane & (QK - 1)) < (QK // 2)
        partner = jnp.where(first, pltpu.roll(zc, 128 - QK // 2, 1), pltpu.roll(zc, QK // 2, 1))
        return zc * cos_ref[...] + partner * sin_ref[...]

    @pl.when(j == 0)
    def _():
        for c in range(AW // 128):
            sl = slice(c * 128, (c + 1) * 128)
            q_ref[:, sl] = (rope(z[:, sl]) * Q_SCALE).astype(BF16)

    @pl.when(j == 1)
    def _():
        for c in range(AW // 128):
            sl = slice(c * 128, (c + 1) * 128)
            kc = rope(z[:, sl])
            k_ref[:, sl] = kc
            kb_ref[:, sl] = kc.astype(BF16)

    @pl.when(j == 2)
    def _():
        v_ref[...] = z
        vb_ref[...] = z.astype(BF16)

    @pl.when(j == 3)
    def _():
        u_ref[...] = z


def _inproj(x, g, w_bf, cos, sin, tm):
    m = x.shape[0]
    row = lambda i, j: (i, 0)
    outs = [jax.ShapeDtypeStruct((m, AW), BF16), jax.ShapeDtypeStruct((m, AW), F32),
            jax.ShapeDtypeStruct((m, AW), BF16), jax.ShapeDtypeStruct((m, AW), F32),
            jax.ShapeDtypeStruct((m, AW), BF16), jax.ShapeDtypeStruct((m, SSM_W), F32)]
    return pl.pallas_call(
        _inproj_body,
        grid=(m // tm, 4),
        in_specs=[pl.BlockSpec((tm, D_MODEL), row),
                  pl.BlockSpec((1, D_MODEL), lambda i, j: (0, 0)),
                  pl.BlockSpec((D_MODEL, AW), lambda i, j: (0, j)),
                  pl.BlockSpec((tm, 128), row),
                  pl.BlockSpec((tm, 128), row)],
        out_specs=[pl.BlockSpec((tm, AW), row) for _ in outs],
        out_shape=outs,
        scratch_shapes=[pltpu.VMEM((tm, D_MODEL), BF16)],
        compiler_params=_cp("parallel", "arbitrary"),
        name="inproj",
    )(x, g, w_bf, cos, sin)


def _lam(lam4_ref):
    l4 = lam4_ref[...]
    a = jnp.exp(jnp.sum(l4[0:1, :] * l4[1:2, :], axis=-1, keepdims=True))
    b = jnp.exp(jnp.sum(l4[2:3, :] * l4[3:4, :], axis=-1, keepdims=True))
    return a - b + LAM_INIT


def _subnorm(o, gsub):
    o = o * lax.rsqrt(jnp.mean(o * o, axis=-1, keepdims=True) + EPS)
    return o * gsub * (1.0 - LAM_INIT)


def _attn_body(lam4_ref, qt_ref, k_ref, vt_ref, gsub_ref, o_ref, acc_scr, *, tq, tk):
    qi = pl.program_id(2)
    unroll = tq // tk
    qt = qt_ref[...]
    row = lax.broadcasted_iota(jnp.int32, qt.shape, 0)
    zero = jnp.zeros_like(qt)
    qc = (jnp.where(row < QK, qt, zero), jnp.where(row >= QK, qt, zero))
    acc_scr[...] = jnp.zeros(acc_scr.shape, F32)
    rel = (lax.broadcasted_iota(jnp.int32, (tk, tq), 1) - lax.broadcasted_iota(jnp.int32, (tk, tq), 0))

    def group(gi, carry, masked):
        ms, ls = list(carry[0]), list(carry[1])
        blocks = [gi * unroll + u for u in range(unroll)]
        kbs = [k_ref[pl.ds(pl.multiple_of(b * tk, tk), tk), :] for b in blocks]
        sts = [[jnp.dot(kb, qc[c], preferred_element_type=F32) for c in range(2)] for kb in kbs]
        for u, b in enumerate(blocks):
            vtb = vt_ref[b]
            for c in range(2):
                st = sts[u][c]
                if masked:
                    st = jnp.where(rel >= u * tk, st, NEG)
                m_new = jnp.maximum(ms[c], jnp.max(st, axis=0, keepdims=True))
                alpha = jnp.exp2(ms[c] - m_new)
                p = jnp.exp2(st - m_new)
                ls[c] = alpha * ls[c] + jnp.sum(p, axis=0, keepdims=True)
                acc_scr[c] = alpha * acc_scr[c] + jnp.dot(vtb, p.astype(BF16), preferred_element_type=F32)
                ms[c] = m_new
        return tuple(ms), tuple(ls)

    m0 = jnp.full((1, tq), NEG, F32)
    l0 = jnp.zeros((1, tq), F32)
    carry = lax.fori_loop(0, qi, lambda g, c: group(g, c, False), ((m0, m0), (l0, l0)))
    _, ls = group(qi, carry, True)

    o = acc_scr[0] / ls[0] - _lam(lam4_ref) * (acc_scr[1] / ls[1])
    o = o * lax.rsqrt(jnp.mean(o * o, axis=0, keepdims=True) + EPS)
    o_ref[...] = (o * gsub_ref[...] * (1.0 - LAM_INIT)).T


def _attn_prompt(lam4, qb, kb, vb, gsub, nb, t, tq, tk):
    qt = qb.reshape(nb, t, AW).transpose(0, 2, 1)
    kb = kb.reshape(nb, t, AW)
    vt = vb.reshape(nb, t // tk, tk, AW).transpose(0, 1, 3, 2)
    gcol = jnp.broadcast_to(gsub.reshape(VD, 1), (VD, tq))
    out = pl.pallas_call(
        functools.partial(_attn_body, tq=tq, tk=tk),
        grid=(nb, HEADS, t // tq),
        in_specs=[pl.BlockSpec((4, QK), lambda b, h, i: (0, 0)),
                  pl.BlockSpec((None, VD, tq), lambda b, h, i: (b, h, i)),
                  pl.BlockSpec((None, t, VD), lambda b, h, i: (b, 0, h)),
                  pl.BlockSpec((None, t // tk, VD, tk), lambda b, h, i: (b, 0, h, 0)),
                  pl.BlockSpec((VD, tq), lambda b, h, i: (0, 0))],
        out_specs=pl.BlockSpec((None, tq, VD), lambda b, h, i: (b, i, h)),
        out_shape=jax.ShapeDtypeStruct((nb, t, AW), F32),
        scratch_shapes=[pltpu.VMEM((2, VD, tq), F32)],
        compiler_params=_cp("parallel", "parallel", "arbitrary"),
        name="attn_prompt",
    )(lam4, qt, kb, vt, gcol)
    return out.reshape(nb * t, AW)


def _sattn_body(pt_ref, lam4_ref, q_ref, *refs, gp):
    k_refs, v_refs = refs[:gp], refs[gp:2 * gp]
    kn_ref, vn_ref, gsub_ref, o_ref, m_scr, l_scr, acc_scr = refs[2 * gp:]
    j = pl.program_id(1)
    q = q_ref[...]
    first = lax.broadcasted_iota(jnp.int32, (HEADS, VD), 1) < QK
    qc = (jnp.where(first, q, 0.0), jnp.where(first, 0.0, q))

    @pl.when(j == 0)
    def _():
        m_scr[...] = jnp.full(m_scr.shape, NEG, F32)
        l_scr[...] = jnp.zeros(l_scr.shape, F32)
        acc_scr[...] = jnp.zeros(acc_scr.shape, F32)

    def update(kp, vp):
        for c in range(2):
            s = jnp.sum(kp * qc[c][None], axis=-1, keepdims=True)
            m_prev = m_scr[c]
            m_new = jnp.maximum(m_prev, jnp.max(s, axis=0))
            alpha = jnp.exp2(m_prev - m_new)
            p = jnp.exp2(s - m_new[None])
            l_scr[c] = alpha * l_scr[c] + jnp.sum(p, axis=0)
            acc_scr[c] = alpha * acc_scr[c] + jnp.sum(p * vp, axis=0)
            m_scr[c] = m_new

    kc = 32
    for g in range(gp):
        for i in range(PAGE // kc):
            update(k_refs[g][i * kc:(i + 1) * kc], v_refs[g][i * kc:(i + 1) * kc])

    @pl.when(j == pl.num_programs(1) - 1)
    def _():
        update(kn_ref[...], vn_ref[...])
        o = acc_scr[0] / l_scr[0] - _lam(lam4_ref) * (acc_scr[1] / l_scr[1])
        o_ref[...] = _subnorm(o, gsub_ref[...])


def _attn_sample(page_table, lam4, q, cache_k, cache_v, kn, vn, gsub, gp):
    bs, n_pages = page_table.shape
    pt = page_table.reshape(-1).astype(jnp.int32)

    def page(g):
        return pl.BlockSpec((None, None, PAGE, HEADS, VD),
                            lambda b, j, pt_ref: (0, pt_ref[b * n_pages + j * gp + g], 0, 0, 0))

    per_b = lambda b, j, p: (b, 0, 0, 0)
    grid_spec = pltpu.PrefetchScalarGridSpec(
        num_scalar_prefetch=1,
        grid=(bs, n_pages // gp),
        in_specs=[pl.BlockSpec((4, QK), lambda b, j, p: (0, 0)),
                  pl.BlockSpec((None, HEADS, VD), lambda b, j, p: (b, 0, 0))]
                 + [page(g) for g in range(gp)] + [page(g) for g in range(gp)]
                 + [pl.BlockSpec((None, 1, HEADS, VD), per_b),
                    pl.BlockSpec((None, 1, HEADS, VD), per_b),
                    pl.BlockSpec((1, VD), lambda b, j, p: (0, 0))],
        out_specs=pl.BlockSpec((None, HEADS, VD), lambda b, j, p: (b, 0, 0)),
        scratch_shapes=[pltpu.VMEM((2, HEADS, VD), F32), pltpu.VMEM((2, HEADS, VD), F32),
                        pltpu.VMEM((2, HEADS, VD), F32)],
    )
    out = pl.pallas_call(
        functools.partial(_sattn_body, gp=gp),
        grid_spec=grid_spec,
        out_shape=jax.ShapeDtypeStruct((bs, HEADS, VD), F32),
        compiler_params=_cp("parallel", "arbitrary"),
        name="attn_sample",
    )(pt, lam4, q, *([cache_k] * gp), *([cache_v] * gp), kn, vn, gsub)
    return out.reshape(bs, AW)


def _cmul(ar, ai, br, bi):
    return ar * br - ai * bi, ar * bi + ai * br


def _s5_prep_body(are_ref, aim_ref, ldt_ref, arer_ref, aimr_ref, ldtr_ref, bre_ref, bim_ref,
                  pow_ref, bbr_ref, bbi_ref):
    def disc(are, aim, ldt):
        dt = jnp.exp(ldt)
        mag = jnp.exp(are * dt)
        lr = mag * jnp.cos(aim * dt)
        li = mag * jnp.sin(aim * dt)
        return lr, li

    lr, li = disc(are_ref[...], aim_ref[...], ldt_ref[...])
    pr, pi = lr, li
    for n in range(8):
        pow_ref[2 * n] = pr
        pow_ref[2 * n + 1] = pi
        pr, pi = _cmul(pr, pi, lr, li)

    are, aim = arer_ref[...], aimr_ref[...]
    lrr, lir = disc(are, aim, ldtr_ref[...])
    den = are * are + aim * aim
    zr = lrr - 1.0
    fr = (zr * are + lir * aim) / den
    fi = (lir * are - zr * aim) / den
    br, bi = bre_ref[...], bim_ref[...]
    bbr_ref[...] = fr * br - fi * bi
    bbi_ref[...] = fr * bi + fi * br


def _s5_prep(a_re, a_im, log_dt, b_re, b_im):
    rep = lambda a: jnp.repeat(a, SSM_CH, axis=1)
    ldt = jnp.broadcast_to(log_dt[:, None], (SSM_G, SSM_P))
    pw, bbr, bbi = pl.pallas_call(
        _s5_prep_body,
        out_shape=[jax.ShapeDtypeStruct((16, SSM_G, SSM_P), F32),
                   jax.ShapeDtypeStruct((SSM_G, SSM_P * SSM_CH), F32),
                   jax.ShapeDtypeStruct((SSM_G, SSM_P * SSM_CH), F32)],
        name="s5_prep",
    )(a_re, a_im, ldt, rep(a_re), rep(a_im), rep(ldt),
      b_re.reshape(SSM_G, -1), b_im.reshape(SSM_G, -1))
    return pw, bbr.reshape(SSM_G, SSM_P, SSM_CH), bbi.reshape(SSM_G, SSM_P, SSM_CH)


def _s5_tables(pw, bbr, bbi, c_re, c_im, d):
    eye = jnp.eye(GB, dtype=F32)
    def blk_in(bb):
        x = bb.reshape(NB, GB, SSM_P, SSM_CH)
        return jnp.einsum('ngpc,gh->ngchp', x, eye).reshape(NB, UL, SL)
    bblk = jnp.concatenate([blk_in(bbr), blk_in(bbi)], axis=2).astype(BF16)
    def blk_out(cc):
        x = cc.reshape(NB, GB, SSM_CH, SSM_P)
        return jnp.einsum('ngcp,gh->ngphc', x, eye).reshape(NB, SL, UL)
    cblk = jnp.concatenate([blk_out(c_re), blk_out(-c_im)], axis=1).astype(BF16)
    lanes = lambda a: a.reshape(NB, 1, SL)
    t_idx = jnp.arange(8)[None, :, None]
    tabs = []
    for n in (1, 2, 4):
        keep = (t_idx >= n).astype(F32)
        tabs += [lanes(pw[2 * (n - 1)]) * keep, lanes(pw[2 * (n - 1) + 1]) * keep]
    pr = jnp.stack([pw[2 * n] for n in range(8)], axis=0).reshape(8, NB, SL).transpose(1, 0, 2)
    pi = jnp.stack([pw[2 * n + 1] for n in range(8)], axis=0).reshape(8, NB, SL).transpose(1, 0, 2)
    tabs += [pr, pi]
    tab = jnp.stack(tabs, axis=1)
    lam1 = jnp.stack([lanes(pw[0]), lanes(pw[1])], axis=1).reshape(NB, 2, SL)
    dvec = d.reshape(NB, 1, UL)
    return bblk, cblk, tab, lam1, dvec


def _s5_body(u_ref, bblk_ref, cblk_ref, d_ref, tab_ref, h0r_ref, h0i_ref,
             y_ref, hr_ref, hi_ref, st_scr, cr_scr, ci_scr, *, chunk):
    k = pl.program_id(2)

    @pl.when(k == 0)
    def _():
        cr_scr[...] = h0r_ref[...]
        ci_scr[...] = h0i_ref[...]

    u = u_ref[...]
    st_scr[...] = jnp.dot(u.astype(BF16), bblk_ref[...], preferred_element_type=F32)

    def tile(r, carry):
        cr, ci = carry
        rows = pl.ds(pl.multiple_of(r * 8, 8), 8)
        xr = st_scr[rows, 0:SL]
        xi = st_scr[rows, SL:2 * SL]
        for lvl, sh in enumerate((1, 2, 4)):
            ar, ai = tab_ref[2 * lvl], tab_ref[2 * lvl + 1]
            sr, si = pltpu.roll(xr, sh, 0), pltpu.roll(xi, sh, 0)
            xr, xi = xr + (ar * sr - ai * si), xi + (ar * si + ai * sr)
        pr, pi = tab_ref[6], tab_ref[7]
        hr = xr + (pr * cr - pi * ci)
        hi = xi + (pr * ci + pi * cr)
        st_scr[rows, 0:SL] = hr
        st_scr[rows, SL:2 * SL] = hi
        return hr[7:8, :], hi[7:8, :]

    cr, ci = lax.fori_loop(0, chunk // 8, tile, (cr_scr[...], ci_scr[...]), unroll=2)
    cr_scr[...] = cr
    ci_scr[...] = ci
    y_ref[...] = (jnp.dot(st_scr[...].astype(BF16), cblk_ref[...], preferred_element_type=F32)
                  + d_ref[...] * u)

    @pl.when(k == pl.num_programs(2) - 1)
    def _():
        hr_ref[...] = cr
        hi_ref[...] = ci


def _s5_prompt(u, bblk, cblk, dvec, tab, h0r, h0i, nb, t, chunk):
    nk = t // chunk
    st = jax.ShapeDtypeStruct((nb, 1, SSM_G * SSM_P), F32)
    y, hr, hi = pl.pallas_call(
        functools.partial(_s5_body, chunk=chunk),
        grid=(nb, NB, nk),
        in_specs=[pl.BlockSpec((chunk, UL), lambda b, n, k: (b * nk + k, n)),
                  pl.BlockSpec((None, UL, 2 * SL), lambda b, n, k: (n, 0, 0)),
                  pl.BlockSpec((None, 2 * SL, UL), lambda b, n, k: (n, 0, 0)),
                  pl.BlockSpec((None, 1, UL), lambda b, n, k: (n, 0, 0)),
                  pl.BlockSpec((None, 8, 8, SL), lambda b, n, k: (n, 0, 0, 0)),
                  pl.BlockSpec((None, 1, SL), lambda b, n, k: (b, 0, n)),
                  pl.BlockSpec((None, 1, SL), lambda b, n, k: (b, 0, n))],
        out_specs=[pl.BlockSpec((chunk, UL), lambda b, n, k: (b * nk + k, n)),
                   pl.BlockSpec((None, 1, SL), lambda b, n, k: (b, 0, n)),
                   pl.BlockSpec((None, 1, SL), lambda b, n, k: (b, 0, n))],
        out_shape=[jax.ShapeDtypeStruct((nb * t, SSM_W), F32), st, st],
        scratch_shapes=[pltpu.VMEM((chunk, 2 * SL), F32), pltpu.VMEM((1, SL), F32),
                        pltpu.VMEM((1, SL), F32)],
        compiler_params=_cp("parallel", "parallel", "arbitrary"),
        name="s5_prompt",
    )(u, bblk, cblk, dvec, tab, h0r.reshape(nb, 1, -1), h0i.reshape(nb, 1, -1))
    return y, hr.reshape(nb, SSM_G, SSM_P), hi.reshape(nb, SSM_G, SSM_P)


def _s5_step_body(u_ref, bblk_ref, cblk_ref, d_ref, lam_ref, h0r_ref, h0i_ref, y_ref, hr_ref, hi_ref):
    u = u_ref[...]
    bu = jnp.dot(u.astype(BF16), bblk_ref[...], preferred_element_type=F32)
    lr, li = lam_ref[0:1, :], lam_ref[1:2, :]
    h0r, h0i = h0r_ref[...], h0i_ref[...]
    hr = bu[:, 0:SL] + (lr * h0r - li * h0i)
    hi = bu[:, SL:2 * SL] + (lr * h0i + li * h0r)
    hr_ref[...] = hr
    hi_ref[...] = hi
    hcat = jnp.concatenate([hr, hi], axis=1).astype(BF16)
    y_ref[...] = jnp.dot(hcat, cblk_ref[...], preferred_element_type=F32) + d_ref[...] * u


def _s5_sample(u, bblk, cblk, dvec, lam1, h0r, h0i):
    rows = u.shape[0]
    st = jax.ShapeDtypeStruct((rows, SSM_G * SSM_P), F32)
    return pl.pallas_call(
        _s5_step_body,
        grid=(NB,),
        in_specs=[pl.BlockSpec((rows, UL), lambda n: (0, n)),
                  pl.BlockSpec((None, UL, 2 * SL), lambda n: (n, 0, 0)),
                  pl.BlockSpec((None, 2 * SL, UL), lambda n: (n, 0, 0)),
                  pl.BlockSpec((None, 1, UL), lambda n: (n, 0, 0)),
                  pl.BlockSpec((None, 2, SL), lambda n: (n, 0, 0)),
                  pl.BlockSpec((rows, SL), lambda n: (0, n)),
                  pl.BlockSpec((rows, SL), lambda n: (0, n))],
        out_specs=[pl.BlockSpec((rows, UL), lambda n: (0, n)),
                   pl.BlockSpec((rows, SL), lambda n: (0, n)),
                   pl.BlockSpec((rows, SL), lambda n: (0, n))],
        out_shape=[jax.ShapeDtypeStruct((rows, SSM_W), F32), st, st],
        compiler_params=_cp("parallel"),
        name="s5_sample",
    )(u, bblk, cblk, dvec, lam1, h0r, h0i)


def _outproj_body(x_ref, a_ref, y_ref, wg_ref, bg_ref, wo_ref, g_ref, x2_ref, h2_ref, *, transposed):
    y = jax.nn.gelu(y_ref[...])
    gate = jnp.dot(y.astype(BF16), wg_ref[...], preferred_element_type=F32) + bg_ref[...]
    y = y * jax.nn.sigmoid(gate)
    mix = (jnp.dot(a_ref[...].astype(BF16), wo_ref[0:AW, :], preferred_element_type=F32)
           + jnp.dot(y.astype(BF16), wo_ref[AW:, :], preferred_element_type=F32))
    x2 = x_ref[...] + mix
    x2_ref[...] = x2
    ms = jnp.mean(x2 * x2, axis=-1, keepdims=True)
    h2 = x2 * lax.rsqrt(ms + EPS) * g_ref[...]
    h2_ref[...] = (h2.T if transposed else h2).astype(BF16)


def _outproj(x, attn, y, wg_bf, bg, wo_bf, g, tm, transposed):
    m = x.shape[0]
    row = lambda i: (i, 0)
    fix = lambda i: (0, 0)
    if transposed:
        h2_spec, h2_shape = pl.BlockSpec((D_MODEL, tm), lambda i: (0, i)), (D_MODEL, m)
    else:
        h2_spec, h2_shape = pl.BlockSpec((tm, D_MODEL), row), (m, D_MODEL)
    return pl.pallas_call(
        functools.partial(_outproj_body, transposed=transposed),
        grid=(m // tm,),
        in_specs=[pl.BlockSpec((tm, D_MODEL), row), pl.BlockSpec((tm, AW), row),
                  pl.BlockSpec((tm, SSM_W), row), pl.BlockSpec((SSM_W, SSM_W), fix),
                  pl.BlockSpec((1, SSM_W), fix), pl.BlockSpec((D_MODEL, D_MODEL), fix),
                  pl.BlockSpec((1, D_MODEL), fix)],
        out_specs=[pl.BlockSpec((tm, D_MODEL), row), h2_spec],
        out_shape=[jax.ShapeDtypeStruct((m, D_MODEL), F32), jax.ShapeDtypeStruct(h2_shape, BF16)],
        compiler_params=_cp("parallel"),
        name="outproj",
    )(x, attn, y, wg_bf, bg, wo_bf, g)


def _topk_body(h_ref, wq_ref, keys_ref, cnt_ref, w1_ref, rank_ref, e2_ref, q_scr, cand_scr, *, tb):
    q_scr[...] = jnp.dot(wq_ref[...], h_ref[...], preferred_element_type=F32).astype(BF16)
    ninf = -jnp.inf

    def head(h, carry):
        for lc in range(tb // 128):
            ls = slice(lc * 128, (lc + 1) * 128)
            sc = []
            for c in range(2):
                hc = 2 * h + c
                qrows = pl.ds(pl.multiple_of(hc * P_HALF, P_HALF), P_HALF)
                sc.append(jnp.dot(keys_ref[hc], q_scr[qrows, ls],
                                  preferred_element_type=F32))
            v1, v2 = [], []
            w = sc[0]
            for _ in range(P_TOPK):
                mx = jnp.max(w, axis=0, keepdims=True)
                v1.append(mx)
                w = jnp.where(w == mx, ninf, w)
            w = sc[1]
            rank = jnp.full(w.shape, float(P_TOPK), F32)
            for r in range(P_TOPK):
                mx = jnp.max(w, axis=0, keepdims=True)
                v2.append(mx)
                hit = w == mx
                rank = jnp.where(hit, float(r), rank)
                w = jnp.where(hit, ninf, w)
            cand_scr[...] = jnp.full(cand_scr.shape, ninf, F32)
            for n, (i, j) in enumerate(STAIR):
                cand_scr[n:n + 1, :] = v1[i] + v2[j]
            w = cand_scr[...]
            top = v1[0] + v2[0]
            z = jnp.zeros_like(top)
            tau = top
            for _ in range(P_TOPK):
                tau = jnp.max(w, axis=0, keepdims=True)
                z = z + jnp.exp(tau - top)
                w = jnp.where(w == tau, ninf, w)
            cnt = jnp.zeros(sc[0].shape, F32)
            for stride in (8, 4, 2, 1):
                probe = None
                for base in range(0, P_TOPK, 2 * stride):
                    row = v2[base + stride - 1]
                    probe = row if probe is None else jnp.where(cnt >= float(base), row, probe)
                cnt = cnt + jnp.where(sc[0] + probe >= tau, float(stride), 0.0)
            cnt = cnt + jnp.where(sc[0] + v2[P_TOPK - 1] >= tau, 1.0, 0.0)
            rows = pl.ds(pl.multiple_of(h * P_NK, P_NK), P_NK)
            cnt_ref[rows, ls] = cnt
            w1_ref[rows, ls] = jnp.exp(sc[0] - v1[0]) / z
            rank_ref[rows, ls] = rank
            e2_ref[rows, ls] = jnp.exp(sc[1] - v2[0])
        return carry

    lax.fori_loop(0, P_HEADS, head, 0)


def _peer_topk(h2t, wqt_bf, keys_bf, tb):
    m = h2t.shape[1]
    f32 = jax.ShapeDtypeStruct((P_HEADS * P_NK, m), F32)
    col = lambda i: (0, i)
    return pl.pallas_call(
        functools.partial(_topk_body, tb=tb),
        grid=(m // tb,),
        in_specs=[pl.BlockSpec((D_MODEL, tb), col),
                  pl.BlockSpec((P_HEADS * 2 * P_HALF, D_MODEL), lambda i: (0, 0)),
                  pl.BlockSpec((P_HEADS * 2, P_NK, P_HALF), lambda i: (0, 0, 0))],
        out_specs=[pl.BlockSpec((P_HEADS * P_NK, tb), col)] * 4,
        out_shape=[f32, f32, f32, f32],
        scratch_shapes=[pltpu.VMEM((P_HEADS * 2 * P_HALF, tb), BF16), pltpu.VMEM((STAIR_ROWS, 128), F32)],
        compiler_params=_cp("parallel"),
        name="peer_topk",
    )(h2t, wqt_bf, keys_bf)


def _peer_body(h_ref, u_ref, vt_ref, cnt_ref, w1_ref, rank_in, e2_in, o_ref,
               a_scr, wm_scr, rank_ref, e2_ref, *, tb, na):
    step = pl.program_id(1)

    @pl.when(step == 0)
    def _():
        o_ref[...] = jnp.zeros(o_ref.shape, F32)
        rank_ref[...] = rank_in[...].astype(BF16)
        e2_ref[...] = e2_in[...].astype(BF16)

    a_scr[...] = jnp.dot(u_ref[...], h_ref[...], preferred_element_type=F32)

    def row_tile(ref, al, h, ls):
        packed = jnp.broadcast_to(ref[h, al:al + 1, ls], (16, 128)).astype(BF16)
        return jnp.tile(packed, (P_NK // 16, 1))

    for al in range(na):
        arows = slice(al * P_NK, (al + 1) * P_NK)
        for lc in range(tb // 128):
            ls = slice(lc * 128, (lc + 1) * 128)
            g = jnp.zeros((P_NK, 128), BF16)
            for h in range(P_HEADS):
                rows = slice(h * P_NK, (h + 1) * P_NK)
                sel = rank_ref[rows, ls] < row_tile(cnt_ref, al, h, ls)
                g = g + jnp.where(sel, e2_ref[rows, ls] * row_tile(w1_ref, al, h, ls), jnp.zeros((), BF16))
            act = jax.nn.gelu(a_scr[arows, ls])
            wm_scr[arows, ls] = (act * g.astype(F32)).astype(BF16)
    o_ref[...] += jnp.dot(vt_ref[...], wm_scr[...], preferred_element_type=F32)


def _peer_dense(h2t, u_bf, vt_bf, cnt, w1, rank, e2, tb, na):
    m = h2t.shape[1]
    col = lambda i, k: (0, i)
    return pl.pallas_call(
        functools.partial(_peer_body, tb=tb, na=na),
        grid=(m // tb, P_NK // na),
        in_specs=[pl.BlockSpec((D_MODEL, tb), col),
                  pl.BlockSpec((na * P_NK, D_MODEL), lambda i, k: (k, 0)),
                  pl.BlockSpec((None, D_MODEL, na * P_NK), lambda i, k: (k, 0, 0)),
                  pl.BlockSpec((P_HEADS, na, tb), lambda i, k: (0, k, i)),
                  pl.BlockSpec((P_HEADS, na, tb), lambda i, k: (0, k, i)),
                  pl.BlockSpec((P_HEADS * P_NK, tb), col),
                  pl.BlockSpec((P_HEADS * P_NK, tb), col)],
        out_specs=pl.BlockSpec((D_MODEL, tb), col),
        out_shape=jax.ShapeDtypeStruct((D_MODEL, m), F32),
        scratch_shapes=[pltpu.VMEM((na * P_NK, tb), F32), pltpu.VMEM((na * P_NK, tb), BF16),
                        pltpu.VMEM((P_HEADS * P_NK, tb), BF16), pltpu.VMEM((P_HEADS * P_NK, tb), BF16)],
        compiler_params=_cp("parallel", "arbitrary"),
        name="peer_dense",
    )(h2t, u_bf, vt_bf, cnt, w1, rank, e2)


def _final_body(x_ref, pt_ref, g_ref, y_ref):
    tm = x_ref.shape[0]
    x = x_ref[...] + pt_ref[...].T[:tm]
    ms = jnp.mean(x * x, axis=-1, keepdims=True)
    y_ref[...] = x * lax.rsqrt(ms + EPS) * g_ref[...]


def _final(x2, pt, g, tm):
    m = x2.shape[0]
    row = lambda i: (i, 0)
    return pl.pallas_call(
        _final_body,
        grid=(m // tm,),
        in_specs=[pl.BlockSpec((tm, D_MODEL), row),
                  pl.BlockSpec((D_MODEL, max(tm, 128)), lambda i: (0, i)),
                  pl.BlockSpec((1, D_MODEL), lambda i: (0, 0))],
        out_specs=pl.BlockSpec((tm, D_MODEL), row),
        out_shape=jax.ShapeDtypeStruct((m, D_MODEL), F32),
        compiler_params=_cp("parallel"),
        name="final_norm",
    )(x2, pt, g)


def _rope_tables(pos):
    half = QK // 2
    inv = 1.0 / (ROPE_THETA ** (jnp.arange(half, dtype=F32) * 2.0 / QK))
    ang = pos.astype(F32)[:, None] * inv[None, :]
    c, s = jnp.cos(ang), jnp.sin(ang)
    cos = jnp.concatenate([c, c, c, c], axis=1)
    sin = jnp.concatenate([-s, s, -s, s], axis=1)
    return cos, sin


def _peer(h2t, wqt_bf, keys_bf, u_bf, vt_bf, tb, na):
    cnt, w1, rank, e2 = _peer_topk(h2t, wqt_bf, keys_bf, tb)
    per_head = lambda a: a.reshape(P_HEADS, P_NK, -1)
    return _peer_dense(h2t, u_bf, vt_bf, per_head(cnt), per_head(w1), rank, e2, tb, na)


def kernel(x_prompt, x_sample, cache_k, cache_v, state_ssm_re, state_ssm_im, page_table, g_mix, w_in, lambda_q1, lambda_k1, lambda_q2, lambda_k2, g_sub, ssm_a_re, ssm_a_im, ssm_log_dt, ssm_b_re, ssm_b_im, ssm_c_re, ssm_c_im, ssm_d, w_glu, b_glu, w_out, g_ffn, peer_w_q, peer_sub_keys, peer_u, peer_v, g_final):
    bp, tp, _ = x_prompt.shape
    bs, ts, _ = x_sample.shape
    assert ts == 1 and g_mix.shape[0] == 1
    n_pages = page_table.shape[1]
    past_len = n_pages * PAGE
    mp = bp * tp
    srows = 16

    w_in_bf = w_in[0].astype(BF16)
    wg_bf = w_glu[0].astype(BF16)
    wo_bf = w_out[0].astype(BF16)
    wqt_bf = peer_w_q[0].T.astype(BF16)
    keys_bf = peer_sub_keys[0].reshape(P_HEADS * 2, P_NK, P_HALF).astype(BF16)
    u_bf = peer_u[0].astype(BF16)
    vt_bf = peer_v[0].reshape(P_NK // PEER_NA, PEER_NA * P_NK, D_MODEL).transpose(0, 2, 1).astype(BF16)
    gmix = g_mix[0].reshape(1, -1)
    gffn = g_ffn[0].reshape(1, -1)
    gfin = g_final.reshape(1, -1)
    gsub = g_sub[0].reshape(1, -1)
    bg = b_glu[0].reshape(1, -1)
    lam4 = jnp.stack([lambda_q1[0], lambda_k1[0], lambda_q2[0], lambda_k2[0]], axis=0)
    pw, bbr, bbi = _s5_prep(ssm_a_re[0], ssm_a_im[0], ssm_log_dt[0], ssm_b_re[0], ssm_b_im[0])
    bblk, cblk, tab, lam1, dvec = _s5_tables(pw, bbr, bbi, ssm_c_re[0], ssm_c_im[0], ssm_d[0])

    xp = x_prompt.reshape(mp, D_MODEL)
    cos_p, sin_p = _rope_tables(jnp.tile(jnp.arange(tp), bp))
    qb, kp, kpb, vp, vpb, up = _inproj(xp, gmix, w_in_bf, cos_p, sin_p, 512)
    attn_p = _attn_prompt(lam4, qb, kpb, vpb, gsub, bp, tp, 512, 128)
    zeros = jnp.zeros((bp, SSM_G * SSM_P), F32)
    yp, rp, ip = _s5_prompt(up, bblk, cblk, dvec, tab, zeros, zeros, bp, tp, 1024)
    x2p, h2pt = _outproj(xp, attn_p, yp, wg_bf, bg, wo_bf, gffn, 256, True)
    peer_p = _peer(h2pt, wqt_bf, keys_bf, u_bf, vt_bf, 512, PEER_NA)
    y_prompt = _final(x2p, peer_p, gfin, 512).reshape(bp, tp, D_MODEL)

    xs = jnp.pad(x_sample.reshape(bs, D_MODEL), ((0, srows - bs), (0, 0)))
    cos_s, sin_s = _rope_tables(jnp.full((srows,), past_len))
    qs, ks, _, vs, _, us = _inproj(xs, gmix, w_in_bf, cos_s, sin_s, srows)
    heads = lambda a: a[:bs].astype(F32).reshape(bs, 1, HEADS, VD)
    attn_s = _attn_sample(page_table, lam4, heads(qs)[:, 0], cache_k, cache_v, heads(ks), heads(vs),
                          gsub, 8 if n_pages % 8 == 0 else 1)
    pad_s = lambda a: jnp.pad(a.reshape(bs, -1), ((0, srows - bs), (0, 0)))
    ys, rs, is_ = _s5_sample(us, bblk, cblk, dvec, lam1, pad_s(state_ssm_re[0]), pad_s(state_ssm_im[0]))
    x2s, h2s = _outproj(xs, pad_s(attn_s), ys, wg_bf, bg, wo_bf, gffn, srows, False)
    h2s_rep = jnp.tile(h2s[:bs], (128 // bs, 1))
    peer_s = _peer(h2s_rep.T, wqt_bf, keys_bf, u_bf, vt_bf, 128, PEER_NA)
    y_sample = _final(x2s, peer_s, gfin, srows)[:bs].reshape(bs, ts, D_MODEL)

    return (y_prompt, y_sample,
            kp.reshape(1, bp, tp, HEADS, 2 * QK), vp.reshape(1, bp, tp, HEADS, VD),
            rp[None], ip[None],
            ks[:bs].reshape(1, bs, ts, HEADS, 2 * QK), vs[:bs].reshape(1, bs, ts, HEADS, VD),
            rs[:bs].reshape(1, bs, SSM_G, SSM_P), is_[:bs].reshape(1, bs, SSM_G, SSM_P))
```

```python
import functools
import math

import jax
import jax.numpy as jnp
from jax import lax
from jax.experimental import pallas as pl
from jax.experimental.pallas import tpu as pltpu

F32 = jnp.float32
BF16 = jnp.bfloat16

D_MODEL = 2048
PAGE = 128
HEADS = 8
QK = 64
VD = 128
AW = HEADS * VD
SSM_W = D_MODEL - AW
SSM_CH = 16
SSM_G = SSM_W // SSM_CH
SSM_P = 64
GB = 16
NB = SSM_G // GB
SL = GB * SSM_P
UL = GB * SSM_CH
P_HEADS = 8
P_TOPK = 16
P_NK = 128
P_HALF = 128
P_E = P_NK * P_NK
EPS = 1e-6
NEG = -1e30
ROPE_THETA = 10000.0
LAM_INIT = 0.8 - 0.6 * math.exp(-0.3 * 0)
VMEM_LIMIT = 56 * 1024 * 1024
Q_SCALE = QK ** -0.5 * math.log2(math.e)
PEER_NA = 8

STAIR = [(i, j) for i in range(P_TOPK) for j in range(P_TOPK) if (i + 1) * (j + 1) <= P_TOPK]
STAIR_ROWS = -(-len(STAIR) // 8) * 8


def _cp(*sem):
    return pltpu.CompilerParams(dimension_semantics=sem, vmem_limit_bytes=VMEM_LIMIT)


def _inproj_body(x_ref, g_ref, w_ref, cos_ref, sin_ref,
                 q_ref, k_ref, kb_ref, v_ref, vb_ref, u_ref, h_scr):
    j = pl.program_id(1)

    @pl.when(j == 0)
    def _():
        x = x_ref[...]
        ms = jnp.mean(x * x, axis=-1, keepdims=True)
        h_scr[...] = (x * lax.rsqrt(ms + EPS) * g_ref[...]).astype(BF16)

    z = jnp.dot(h_scr[...], w_ref[...], preferred_element_type=F32)

    def rope(zc):
        lane = lax.broadcasted_iota(jnp.int32, zc.shape, 1)
        first = (lane & (QK - 1)) < (QK // 2)
        partner = jnp.where(first, pltpu.roll(zc, 128 - QK // 2, 1), pltpu.roll(zc, QK // 2, 1))
        return zc * cos_ref[...] + partner * sin_ref[...]

    @pl.when(j == 0)
    def _():
        for c in range(AW // 128):
            sl = slice(c * 128, (c + 1) * 128)
            q_ref[:, sl] = (rope(z[:, sl]) * Q_SCALE).astype(BF16)

    @pl.when(j == 1)
    def _():
        for c in range(AW // 128):
            sl = slice(c * 128, (c + 1) * 128)
            kc = rope(z[:, sl])
            k_ref[:, sl] = kc
            kb_ref[:, sl] = kc.astype(BF16)

    @pl.when(j == 2)
    def _():
        v_ref[...] = z
        vb_ref[...] = z.astype(BF16)

    @pl.when(j == 3)
    def _():
        u_ref[...] = z


def _inproj(x, g, w_bf, cos, sin, tm):
    m = x.shape[0]
    row = lambda i, j: (i, 0)
    outs = [jax.ShapeDtypeStruct((m, AW), BF16), jax.ShapeDtypeStruct((m, AW), F32),
            jax.ShapeDtypeStruct((m, AW), BF16), jax.ShapeDtypeStruct((m, AW), F32),
            jax.ShapeDtypeStruct((m, AW), BF16), jax.ShapeDtypeStruct((m, SSM_W), F32)]
    return pl.pallas_call(
        _inproj_body,
        grid=(m // tm, 4),
        in_specs=[pl.BlockSpec((tm, D_MODEL), row),
                  pl.BlockSpec((1, D_MODEL), lambda i, j: (0, 0)),
                  pl.BlockSpec((D_MODEL, AW), lambda i, j: (0, j)),
                  pl.BlockSpec((tm, 128), row),
                  pl.BlockSpec((tm, 128), row)],
        out_specs=[pl.BlockSpec((tm, AW), row) for _ in outs],
        out_shape=outs,
        scratch_shapes=[pltpu.VMEM((tm, D_MODEL), BF16)],
        compiler_params=_cp("parallel", "arbitrary"),
        name="inproj",
    )(x, g, w_bf, cos, sin)


def _lam(lam4_ref):
    l4 = lam4_ref[...]
    a = jnp.exp(jnp.sum(l4[0:1, :] * l4[1:2, :], axis=-1, keepdims=True))
    b = jnp.exp(jnp.sum(l4[2:3, :] * l4[3:4, :], axis=-1, keepdims=True))
    return a - b + LAM_INIT


def _subnorm(o, gsub):
    o = o * lax.rsqrt(jnp.mean(o * o, axis=-1, keepdims=True) + EPS)
    return o * gsub * (1.0 - LAM_INIT)


def _attn_body(lam4_ref, qt_ref, k_ref, vt_ref, gsub_ref, o_ref, acc_scr, *, tq, tk):
    qi = pl.program_id(2)
    unroll = tq // tk
    qt = qt_ref[...]
    row = lax.broadcasted_iota(jnp.int32, qt.shape, 0)
    zero = jnp.zeros_like(qt)
    qc = (jnp.where(row < QK, qt, zero), jnp.where(row >= QK, qt, zero))
    acc_scr[...] = jnp.zeros(acc_scr.shape, F32)
    rel = (lax.broadcasted_iota(jnp.int32, (tk, tq), 1) - lax.broadcasted_iota(jnp.int32, (tk, tq), 0))

    def group(gi, carry, masked):
        ms, ls = list(carry[0]), list(carry[1])
        blocks = [gi * unroll + u for u in range(unroll)]
        kbs = [k_ref[pl.ds(pl.multiple_of(b * tk, tk), tk), :] for b in blocks]
        offs = [u * tk if masked else 0 for u in range(unroll)]
        sts = [[jnp.dot(kb, qc[c][:, off:], preferred_element_type=F32) for c in range(2)]
               for kb, off in zip(kbs, offs)]
        for u, b in enumerate(blocks):
            vtb = vt_ref[b]
            off = offs[u]
            for c in range(2):
                st = sts[u][c]
                if masked:
                    st = jnp.where(rel[:, off:] >= u * tk, st, NEG)
                m_old, l_old = ms[c][:, off:], ls[c][:, off:]
                m_new = jnp.maximum(m_old, jnp.max(st, axis=0, keepdims=True))
                alpha = jnp.exp2(m_old - m_new)
                p = jnp.exp2(st - m_new)
                l_new = alpha * l_old + jnp.sum(p, axis=0, keepdims=True)
                acc_scr[c, :, off:] = (alpha * acc_scr[c, :, off:]
                                       + jnp.dot(vtb, p.astype(BF16), preferred_element_type=F32))
                ms[c] = m_new if off == 0 else jnp.concatenate([ms[c][:, :off], m_new], axis=1)
                ls[c] = l_new if off == 0 else jnp.concatenate([ls[c][:, :off], l_new], axis=1)
        return tuple(ms), tuple(ls)

    m0 = jnp.full((1, tq), NEG, F32)
    l0 = jnp.zeros((1, tq), F32)
    carry = lax.fori_loop(0, qi, lambda g, c: group(g, c, False), ((m0, m0), (l0, l0)))
    _, ls = group(qi, carry, True)

    o = acc_scr[0] / ls[0] - _lam(lam4_ref) * (acc_scr[1] / ls[1])
    o = o * lax.rsqrt(jnp.mean(o * o, axis=0, keepdims=True) + EPS)
    o_ref[...] = (o * gsub_ref[...] * (1.0 - LAM_INIT)).T


def _attn_prompt(lam4, qb, kb, vb, gsub, nb, t, tq, tk):
    qt = qb.reshape(nb, t, AW).transpose(0, 2, 1)
    kb = kb.reshape(nb, t, AW)
    vt = vb.reshape(nb, t // tk, tk, AW).transpose(0, 1, 3, 2)
    gcol = jnp.broadcast_to(gsub.reshape(VD, 1), (VD, tq))
    out = pl.pallas_call(
        functools.partial(_attn_body, tq=tq, tk=tk),
        grid=(nb, HEADS, t // tq),
        in_specs=[pl.BlockSpec((4, QK), lambda b, h, i: (0, 0)),
                  pl.BlockSpec((None, VD, tq), lambda b, h, i: (b, h, i)),
                  pl.BlockSpec((None, t, VD), lambda b, h, i: (b, 0, h)),
                  pl.BlockSpec((None, t // tk, VD, tk), lambda b, h, i: (b, 0, h, 0)),
                  pl.BlockSpec((VD, tq), lambda b, h, i: (0, 0))],
        out_specs=pl.BlockSpec((None, tq, VD), lambda b, h, i: (b, i, h)),
        out_shape=jax.ShapeDtypeStruct((nb, t, AW), F32),
        scratch_shapes=[pltpu.VMEM((2, VD, tq), F32)],
        compiler_params=_cp("parallel", "parallel", "arbitrary"),
        name="attn_prompt",
    )(lam4, qt, kb, vt, gcol)
    return out.reshape(nb * t, AW)


def _sattn_body(pt_ref, lam4_ref, q_ref, *refs, gp):
    k_refs, v_refs = refs[:gp], refs[gp:2 * gp]
    kn_ref, vn_ref, gsub_ref, o_ref, m_scr, l_scr, acc_scr = refs[2 * gp:]
    j = pl.program_id(1)
    q = q_ref[...]
    first = lax.broadcasted_iota(jnp.int32, (HEADS, VD), 1) < QK
    qc = (jnp.where(first, q, 0.0), jnp.where(first, 0.0, q))

    @pl.when(j == 0)
    def _():
        m_scr[...] = jnp.full(m_scr.shape, NEG, F32)
        l_scr[...] = jnp.zeros(l_scr.shape, F32)
        acc_scr[...] = jnp.zeros(acc_scr.shape, F32)

    def update(kp, vp):
        for c in range(2):
            s = jnp.sum(kp * qc[c][None], axis=-1, keepdims=True)
            m_prev = m_scr[c]
            m_new = jnp.maximum(m_prev, jnp.max(s, axis=0))
            alpha = jnp.exp2(m_prev - m_new)
            p = jnp.exp2(s - m_new[None])
            l_scr[c] = alpha * l_scr[c] + jnp.sum(p, axis=0)
            acc_scr[c] = alpha * acc_scr[c] + jnp.sum(p * vp, axis=0)
            m_scr[c] = m_new

    kc = 32
    for g in range(gp):
        for i in range(PAGE // kc):
            update(k_refs[g][i * kc:(i + 1) * kc], v_refs[g][i * kc:(i + 1) * kc])

    @pl.when(j == pl.num_programs(1) - 1)
    def _():
        update(kn_ref[...], vn_ref[...])
        o = acc_scr[0] / l_scr[0] - _lam(lam4_ref) * (acc_scr[1] / l_scr[1])
        o_ref[...] = _subnorm(o, gsub_ref[...])


def _attn_sample(page_table, lam4, q, cache_k, cache_v, kn, vn, gsub, gp):
    bs, n_pages = page_table.shape
    pt = page_table.reshape(-1).astype(jnp.int32)

    def page(g):
        return pl.BlockSpec((None, None, PAGE, HEADS, VD),
                            lambda b, j, pt_ref: (0, pt_ref[b * n_pages + j * gp + g], 0, 0, 0))

    per_b = lambda b, j, p: (b, 0, 0, 0)
    grid_spec = pltpu.PrefetchScalarGridSpec(
        num_scalar_prefetch=1,
        grid=(bs, n_pages // gp),
        in_specs=[pl.BlockSpec((4, QK), lambda b, j, p: (0, 0)),
                  pl.BlockSpec((None, HEADS, VD), lambda b, j, p: (b, 0, 0))]
                 + [page(g) for g in range(gp)] + [page(g) for g in range(gp)]
                 + [pl.BlockSpec((None, 1, HEADS, VD), per_b),
                    pl.BlockSpec((None, 1, HEADS, VD), per_b),
                    pl.BlockSpec((1, VD), lambda b, j, p: (0, 0))],
        out_specs=pl.BlockSpec((None, HEADS, VD), lambda b, j, p: (b, 0, 0)),
        scratch_shapes=[pltpu.VMEM((2, HEADS, VD), F32), pltpu.VMEM((2, HEADS, VD), F32),
                        pltpu.VMEM((2, HEADS, VD), F32)],
    )
    out = pl.pallas_call(
        functools.partial(_sattn_body, gp=gp),
        grid_spec=grid_spec,
        out_shape=jax.ShapeDtypeStruct((bs, HEADS, VD), F32),
        compiler_params=_cp("parallel", "arbitrary"),
        name="attn_sample",
    )(pt, lam4, q, *([cache_k] * gp), *([cache_v] * gp), kn, vn, gsub)
    return out.reshape(bs, AW)


def _cmul(ar, ai, br, bi):
    return ar * br - ai * bi, ar * bi + ai * br


def _s5_prep_body(are_ref, aim_ref, ldt_ref, arer_ref, aimr_ref, ldtr_ref, bre_ref, bim_ref,
                  pow_ref, bbr_ref, bbi_ref):
    def disc(are, aim, ldt):
        dt = jnp.exp(ldt)
        mag = jnp.exp(are * dt)
        lr = mag * jnp.cos(aim * dt)
        li = mag * jnp.sin(aim * dt)
        return lr, li

    lr, li = disc(are_ref[...], aim_ref[...], ldt_ref[...])
    pr, pi = lr, li
    for n in range(8):
        pow_ref[2 * n] = pr
        pow_ref[2 * n + 1] = pi
        pr, pi = _cmul(pr, pi, lr, li)

    are, aim = arer_ref[...], aimr_ref[...]
    lrr, lir = disc(are, aim, ldtr_ref[...])
    den = are * are + aim * aim
    zr = lrr - 1.0
    fr = (zr * are + lir * aim) / den
    fi = (lir * are - zr * aim) / den
    br, bi = bre_ref[...], bim_ref[...]
    bbr_ref[...] = fr * br - fi * bi
    bbi_ref[...] = fr * bi + fi * br


def _s5_prep(a_re, a_im, log_dt, b_re, b_im):
    rep = lambda a: jnp.repeat(a, SSM_CH, axis=1)
    ldt = jnp.broadcast_to(log_dt[:, None], (SSM_G, SSM_P))
    pw, bbr, bbi = pl.pallas_call(
        _s5_prep_body,
        out_shape=[jax.ShapeDtypeStruct((16, SSM_G, SSM_P), F32),
                   jax.ShapeDtypeStruct((SSM_G, SSM_P * SSM_CH), F32),
                   jax.ShapeDtypeStruct((SSM_G, SSM_P * SSM_CH), F32)],
        name="s5_prep",
    )(a_re, a_im, ldt, rep(a_re), rep(a_im), rep(ldt),
      b_re.reshape(SSM_G, -1), b_im.reshape(SSM_G, -1))
    return pw, bbr.reshape(SSM_G, SSM_P, SSM_CH), bbi.reshape(SSM_G, SSM_P, SSM_CH)


def _s5_tables(pw, bbr, bbi, c_re, c_im, d):
    eye = jnp.eye(GB, dtype=F32)
    def blk_in(bb):
        x = bb.reshape(NB, GB, SSM_P, SSM_CH)
        return jnp.einsum('ngpc,gh->ngchp', x, eye).reshape(NB, UL, SL)
    bblk = jnp.concatenate([blk_in(bbr), blk_in(bbi)], axis=2).astype(BF16)
    def blk_out(cc):
        x = cc.reshape(NB, GB, SSM_CH, SSM_P)
        return jnp.einsum('ngcp,gh->ngphc', x, eye).reshape(NB, SL, UL)
    cblk = jnp.concatenate([blk_out(c_re), blk_out(-c_im)], axis=1).astype(BF16)
    lanes = lambda a: a.reshape(NB, 1, SL)
    t_idx = jnp.arange(8)[None, :, None]
    tabs = []
    for n in (1, 2, 4):
        keep = (t_idx >= n).astype(F32)
        tabs += [lanes(pw[2 * (n - 1)]) * keep, lanes(pw[2 * (n - 1) + 1]) * keep]
    pr = jnp.stack([pw[2 * n] for n in range(8)], axis=0).reshape(8, NB, SL).transpose(1, 0, 2)
    pi = jnp.stack([pw[2 * n + 1] for n in range(8)], axis=0).reshape(8, NB, SL).transpose(1, 0, 2)
    tabs += [pr, pi]
    tab = jnp.stack(tabs, axis=1)
    lam1 = jnp.stack([lanes(pw[0]), lanes(pw[1])], axis=1).reshape(NB, 2, SL)
    dvec = d.reshape(NB, 1, UL)
    return bblk, cblk, tab, lam1, dvec


def _s5_body(u_ref, bblk_ref, cblk_ref, d_ref, tab_ref, h0r_ref, h0i_ref,
             y_ref, hr_ref, hi_ref, st_scr, cr_scr, ci_scr, *, chunk):
    k = pl.program_id(2)

    @pl.when(k == 0)
    def _():
        cr_scr[...] = h0r_ref[...]
        ci_scr[...] = h0i_ref[...]

    u = u_ref[...]
    st_scr[...] = jnp.dot(u.astype(BF16), bblk_ref[...], preferred_element_type=F32)

    def tile(r, carry):
        cr, ci = carry
        rows = pl.ds(pl.multiple_of(r * 8, 8), 8)
        xr = st_scr[rows, 0:SL]
        xi = st_scr[rows, SL:2 * SL]
        for lvl, sh in enumerate((1, 2, 4)):
            ar, ai = tab_ref[2 * lvl], tab_ref[2 * lvl + 1]
            sr, si = pltpu.roll(xr, sh, 0), pltpu.roll(xi, sh, 0)
            xr, xi = xr + (ar * sr - ai * si), xi + (ar * si + ai * sr)
        pr, pi = tab_ref[6], tab_ref[7]
        hr = xr + (pr * cr - pi * ci)
        hi = xi + (pr * ci + pi * cr)
        st_scr[rows, 0:SL] = hr
        st_scr[rows, SL:2 * SL] = hi
        return hr[7:8, :], hi[7:8, :]

    cr, ci = lax.fori_loop(0, chunk // 8, tile, (cr_scr[...], ci_scr[...]), unroll=2)
    cr_scr[...] = cr
    ci_scr[...] = ci
    y_ref[...] = (jnp.dot(st_scr[...].astype(BF16), cblk_ref[...], preferred_element_type=F32)
                  + d_ref[...] * u)

    @pl.when(k == pl.num_programs(2) - 1)
    def _():
        hr_ref[...] = cr
        hi_ref[...] = ci


def _s5_prompt(u, bblk, cblk, dvec, tab, h0r, h0i, nb, t, chunk):
    nk = t // chunk
    st = jax.ShapeDtypeStruct((nb, 1, SSM_G * SSM_P), F32)
    y, hr, hi = pl.pallas_call(
        functools.partial(_s5_body, chunk=chunk),
        grid=(nb, NB, nk),
        in_specs=[pl.BlockSpec((chunk, UL), lambda b, n, k: (b * nk + k, n)),
                  pl.BlockSpec((None, UL, 2 * SL), lambda b, n, k: (n, 0, 0)),
                  pl.BlockSpec((None, 2 * SL, UL), lambda b, n, k: (n, 0, 0)),
                  pl.BlockSpec((None, 1, UL), lambda b, n, k: (n, 0, 0)),
                  pl.BlockSpec((None, 8, 8, SL), lambda b, n, k: (n, 0, 0, 0)),
                  pl.BlockSpec((None, 1, SL), lambda b, n, k: (b, 0, n)),
                  pl.BlockSpec((None, 1, SL), lambda b, n, k: (b, 0, n))],
        out_specs=[pl.BlockSpec((chunk, UL), lambda b, n, k: (b * nk + k, n)),
                   pl.BlockSpec((None, 1, SL), lambda b, n, k: (b, 0, n)),
                   pl.BlockSpec((None, 1, SL), lambda b, n, k: (b, 0, n))],
        out_shape=[jax.ShapeDtypeStruct((nb * t, SSM_W), F32), st, st],
        scratch_shapes=[pltpu.VMEM((chunk, 2 * SL), F32), pltpu.VMEM((1, SL), F32),
                        pltpu.VMEM((1, SL), F32)],
        compiler_params=_cp("parallel", "parallel", "arbitrary"),
        name="s5_prompt",
    )(u, bblk, cblk, dvec, tab, h0r.reshape(nb, 1, -1), h0i.reshape(nb, 1, -1))
    return y, hr.reshape(nb, SSM_G, SSM_P), hi.reshape(nb, SSM_G, SSM_P)


def _s5_step_body(u_ref, bblk_ref, cblk_ref, d_ref, lam_ref, h0r_ref, h0i_ref, y_ref, hr_ref, hi_ref):
    u = u_ref[...]
    bu = jnp.dot(u.astype(BF16), bblk_ref[...], preferred_element_type=F32)
    lr, li = lam_ref[0:1, :], lam_ref[1:2, :]
    h0r, h0i = h0r_ref[...], h0i_ref[...]
    hr = bu[:, 0:SL] + (lr * h0r - li * h0i)
    hi = bu[:, SL:2 * SL] + (lr * h0i + li * h0r)
    hr_ref[...] = hr
    hi_ref[...] = hi
    hcat = jnp.concatenate([hr, hi], axis=1).astype(BF16)
    y_ref[...] = jnp.dot(hcat, cblk_ref[...], preferred_element_type=F32) + d_ref[...] * u


def _s5_sample(u, bblk, cblk, dvec, lam1, h0r, h0i):
    rows = u.shape[0]
    st = jax.ShapeDtypeStruct((rows, SSM_G * SSM_P), F32)
    return pl.pallas_call(
        _s5_step_body,
        grid=(NB,),
        in_specs=[pl.BlockSpec((rows, UL), lambda n: (0, n)),
                  pl.BlockSpec((None, UL, 2 * SL), lambda n: (n, 0, 0)),
                  pl.BlockSpec((None, 2 * SL, UL), lambda n: (n, 0, 0)),
                  pl.BlockSpec((None, 1, UL), lambda n: (n, 0, 0)),
                  pl.BlockSpec((None, 2, SL), lambda n: (n, 0, 0)),
                  pl.BlockSpec((rows, SL), lambda n: (0, n)),
                  pl.BlockSpec((rows, SL), lambda n: (0, n))],
        out_specs=[pl.BlockSpec((rows, UL), lambda n: (0, n)),
                   pl.BlockSpec((rows, SL), lambda n: (0, n)),
                   pl.BlockSpec((rows, SL), lambda n: (0, n))],
        out_shape=[jax.ShapeDtypeStruct((rows, SSM_W), F32), st, st],
        compiler_params=_cp("parallel"),
        name="s5_sample",
    )(u, bblk, cblk, dvec, lam1, h0r, h0i)


def _outproj_body(x_ref, a_ref, y_ref, wg_ref, bg_ref, wo_ref, g_ref, x2_ref, h2_ref, *, transposed):
    y = jax.nn.gelu(y_ref[...])
    gate = jnp.dot(y.astype(BF16), wg_ref[...], preferred_element_type=F32) + bg_ref[...]
    y = y * jax.nn.sigmoid(gate)
    mix = (jnp.dot(a_ref[...].astype(BF16), wo_ref[0:AW, :], preferred_element_type=F32)
           + jnp.dot(y.astype(BF16), wo_ref[AW:, :], preferred_element_type=F32))
    x2 = x_ref[...] + mix
    x2_ref[...] = x2
    ms = jnp.mean(x2 * x2, axis=-1, keepdims=True)
    h2 = x2 * lax.rsqrt(ms + EPS) * g_ref[...]
    h2_ref[...] = (h2.T if transposed else h2).astype(BF16)


def _outproj(x, attn, y, wg_bf, bg, wo_bf, g, tm, transposed):
    m = x.shape[0]
    row = lambda i: (i, 0)
    fix = lambda i: (0, 0)
    if transposed:
        h2_spec, h2_shape = pl.BlockSpec((D_MODEL, tm), lambda i: (0, i)), (D_MODEL, m)
    else:
        h2_spec, h2_shape = pl.BlockSpec((tm, D_MODEL), row), (m, D_MODEL)
    return pl.pallas_call(
        functools.partial(_outproj_body, transposed=transposed),
        grid=(m // tm,),
        in_specs=[pl.BlockSpec((tm, D_MODEL), row), pl.BlockSpec((tm, AW), row),
                  pl.BlockSpec((tm, SSM_W), row), pl.BlockSpec((SSM_W, SSM_W), fix),
                  pl.BlockSpec((1, SSM_W), fix), pl.BlockSpec((D_MODEL, D_MODEL), fix),
                  pl.BlockSpec((1, D_MODEL), fix)],
        out_specs=[pl.BlockSpec((tm, D_MODEL), row), h2_spec],
        out_shape=[jax.ShapeDtypeStruct((m, D_MODEL), F32), jax.ShapeDtypeStruct(h2_shape, BF16)],
        compiler_params=_cp("parallel"),
        name="outproj",
    )(x, attn, y, wg_bf, bg, wo_bf, g)


def _topk_body(h_ref, wq_ref, keys_ref, cnt_ref, w1_ref, rank_ref, e2_ref, q_scr, cand_scr, *, tb):
    q_scr[...] = jnp.dot(wq_ref[...], h_ref[...], preferred_element_type=F32).astype(BF16)
    ninf = -jnp.inf

    def head(h, carry):
        for lc in range(tb // 128):
            ls = slice(lc * 128, (lc + 1) * 128)
            sc = []
            for c in range(2):
                hc = 2 * h + c
                qrows = pl.ds(pl.multiple_of(hc * P_HALF, P_HALF), P_HALF)
                sc.append(jnp.dot(keys_ref[hc], q_scr[qrows, ls],
                                  preferred_element_type=F32))
            v1, v2 = [], []
            w = sc[0]
            for _ in range(P_TOPK):
                mx = jnp.max(w, axis=0, keepdims=True)
                v1.append(mx)
                w = jnp.where(w == mx, ninf, w)
            w = sc[1]
            rank = jnp.full(w.shape, float(P_TOPK), F32)
            for r in range(P_TOPK):
                mx = jnp.max(w, axis=0, keepdims=True)
                v2.append(mx)
                hit = w == mx
                rank = jnp.where(hit, float(r), rank)
                w = jnp.where(hit, ninf, w)
            cand_scr[...] = jnp.full(cand_scr.shape, ninf, F32)
            for n, (i, j) in enumerate(STAIR):
                cand_scr[n:n + 1, :] = v1[i] + v2[j]
            w = cand_scr[...]
            top = v1[0] + v2[0]
            z = jnp.zeros_like(top)
            tau = top
            for _ in range(P_TOPK):
                tau = jnp.max(w, axis=0, keepdims=True)
                z = z + jnp.exp(tau - top)
                w = jnp.where(w == tau, ninf, w)
            cnt = jnp.zeros(sc[0].shape, F32)
            for stride in (8, 4, 2, 1):
                probe = None
                for base in range(0, P_TOPK, 2 * stride):
                    row = v2[base + stride - 1]
                    probe = row if probe is None else jnp.where(cnt >= float(base), row, probe)
                cnt = cnt + jnp.where(sc[0] + probe >= tau, float(stride), 0.0)
            cnt = cnt + jnp.where(sc[0] + v2[P_TOPK - 1] >= tau, 1.0, 0.0)
            rows = pl.ds(pl.multiple_of(h * P_NK, P_NK), P_NK)
            cnt_ref[rows, ls] = cnt
            w1_ref[rows, ls] = jnp.exp(sc[0] - v1[0]) / z
            rank_ref[rows, ls] = rank
            e2_ref[rows, ls] = jnp.exp(sc[1] - v2[0])
        return carry

    lax.fori_loop(0, P_HEADS, head, 0)


def _peer_topk(h2t, wqt_bf, keys_bf, tb):
    m = h2t.shape[1]
    f32 = jax.ShapeDtypeStruct((P_HEADS * P_NK, m), F32)
    col = lambda i: (0, i)
    return pl.pallas_call(
        functools.partial(_topk_body, tb=tb),
        grid=(m // tb,),
        in_specs=[pl.BlockSpec((D_MODEL, tb), col),
                  pl.BlockSpec((P_HEADS * 2 * P_HALF, D_MODEL), lambda i: (0, 0)),
                  pl.BlockSpec((P_HEADS * 2, P_NK, P_HALF), lambda i: (0, 0, 0))],
        out_specs=[pl.BlockSpec((P_HEADS * P_NK, tb), col)] * 4,
        out_shape=[f32, f32, f32, f32],
        scratch_shapes=[pltpu.VMEM((P_HEADS * 2 * P_HALF, tb), BF16), pltpu.VMEM((STAIR_ROWS, 128), F32)],
        compiler_params=_cp("parallel"),
        name="peer_topk",
    )(h2t, wqt_bf, keys_bf)


def _peer_body(h_ref, u_ref, vt_ref, cnt_ref, w1_ref, rank_in, e2_in, o_ref,
               a_scr, wm_scr, rank_ref, e2_ref, *, tb, na):
    step = pl.program_id(1)

    @pl.when(step == 0)
    def _():
        o_ref[...] = jnp.zeros(o_ref.shape, F32)
        rank_ref[...] = rank_in[...].astype(BF16)
        e2_ref[...] = e2_in[...].astype(BF16)

    a_scr[...] = jnp.dot(u_ref[...], h_ref[...], preferred_element_type=F32)

    def row_tile(ref, al, h, ls):
        packed = jnp.broadcast_to(ref[h, al:al + 1, ls], (16, 128)).astype(BF16)
        return jnp.tile(packed, (P_NK // 16, 1))

    for al in range(na):
        arows = slice(al * P_NK, (al + 1) * P_NK)
        for lc in range(tb // 128):
            ls = slice(lc * 128, (lc + 1) * 128)
            g = jnp.zeros((P_NK, 128), BF16)
            for h in range(P_HEADS):
                rows = slice(h * P_NK, (h + 1) * P_NK)
                sel = rank_ref[rows, ls] < row_tile(cnt_ref, al, h, ls)
                g = g + jnp.where(sel, e2_ref[rows, ls] * row_tile(w1_ref, al, h, ls), jnp.zeros((), BF16))
            act = jax.nn.gelu(a_scr[arows, ls])
            wm_scr[arows, ls] = (act * g.astype(F32)).astype(BF16)
    o_ref[...] += jnp.dot(vt_ref[...], wm_scr[...], preferred_element_type=F32)


def _peer_dense(h2t, u_bf, vt_bf, cnt, w1, rank, e2, tb, na):
    m = h2t.shape[1]
    col = lambda i, k: (0, i)
    return pl.pallas_call(
        functools.partial(_peer_body, tb=tb, na=na),
        grid=(m // tb, P_NK // na),
        in_specs=[pl.BlockSpec((D_MODEL, tb), col),
                  pl.BlockSpec((na * P_NK, D_MODEL), lambda i, k: (k, 0)),
                  pl.BlockSpec((None, D_MODEL, na * P_NK), lambda i, k: (k, 0, 0)),
                  pl.BlockSpec((P_HEADS, na, tb), lambda i, k: (0, k, i)),
                  pl.BlockSpec((P_HEADS, na, tb), lambda i, k: (0, k, i)),
                  pl.BlockSpec((P_HEADS * P_NK, tb), col),
                  pl.BlockSpec((P_HEADS * P_NK, tb), col)],
        out_specs=pl.BlockSpec((D_MODEL, tb), col),
        out_shape=jax.ShapeDtypeStruct((D_MODEL, m), F32),
        scratch_shapes=[pltpu.VMEM((na * P_NK, tb), F32), pltpu.VMEM((na * P_NK, tb), BF16),
                        pltpu.VMEM((P_HEADS * P_NK, tb), BF16), pltpu.VMEM((P_HEADS * P_NK, tb), BF16)],
        compiler_params=_cp("parallel", "arbitrary"),
        name="peer_dense",
    )(h2t, u_bf, vt_bf, cnt, w1, rank, e2)


def _final_body(x_ref, pt_ref, g_ref, y_ref):
    tm = x_ref.shape[0]
    x = x_ref[...] + pt_ref[...].T[:tm]
    ms = jnp.mean(x * x, axis=-1, keepdims=True)
    y_ref[...] = x * lax.rsqrt(ms + EPS) * g_ref[...]


def _final(x2, pt, g, tm):
    m = x2.shape[0]
    row = lambda i: (i, 0)
    return pl.pallas_call(
        _final_body,
        grid=(m // tm,),
        in_specs=[pl.BlockSpec((tm, D_MODEL), row),
                  pl.BlockSpec((D_MODEL, max(tm, 128)), lambda i: (0, i)),
                  pl.BlockSpec((1, D_MODEL), lambda i: (0, 0))],
        out_specs=pl.BlockSpec((tm, D_MODEL), row),
        out_shape=jax.ShapeDtypeStruct((m, D_MODEL), F32),
        compiler_params=_cp("parallel"),
        name="final_norm",
    )(x2, pt, g)


def _rope_tables(pos):
    half = QK // 2
    inv = 1.0 / (ROPE_THETA ** (jnp.arange(half, dtype=F32) * 2.0 / QK))
    ang = pos.astype(F32)[:, None] * inv[None, :]
    c, s = jnp.cos(ang), jnp.sin(ang)
    cos = jnp.concatenate([c, c, c, c], axis=1)
    sin = jnp.concatenate([-s, s, -s, s], axis=1)
    return cos, sin


def _peer(h2t, wqt_bf, keys_bf, u_bf, vt_bf, tb, na):
    cnt, w1, rank, e2 = _peer_topk(h2t, wqt_bf, keys_bf, tb)
    per_head = lambda a: a.reshape(P_HEADS, P_NK, -1)
    return _peer_dense(h2t, u_bf, vt_bf, per_head(cnt), per_head(w1), rank, e2, tb, na)


def kernel(x_prompt, x_sample, cache_k, cache_v, state_ssm_re, state_ssm_im, page_table, g_mix, w_in, lambda_q1, lambda_k1, lambda_q2, lambda_k2, g_sub, ssm_a_re, ssm_a_im, ssm_log_dt, ssm_b_re, ssm_b_im, ssm_c_re, ssm_c_im, ssm_d, w_glu, b_glu, w_out, g_ffn, peer_w_q, peer_sub_keys, peer_u, peer_v, g_final):
    bp, tp, _ = x_prompt.shape
    bs, ts, _ = x_sample.shape
    assert ts == 1 and g_mix.shape[0] == 1
    n_pages = page_table.shape[1]
    past_len = n_pages * PAGE
    mp = bp * tp
    srows = 16

    w_in_bf = w_in[0].astype(BF16)
    wg_bf = w_glu[0].astype(BF16)
    wo_bf = w_out[0].astype(BF16)
    wqt_bf = peer_w_q[0].T.astype(BF16)
    keys_bf = peer_sub_keys[0].reshape(P_HEADS * 2, P_NK, P_HALF).astype(BF16)
    u_bf = peer_u[0].astype(BF16)
    vt_bf = peer_v[0].reshape(P_NK // PEER_NA, PEER_NA * P_NK, D_MODEL).transpose(0, 2, 1).astype(BF16)
    gmix = g_mix[0].reshape(1, -1)
    gffn = g_ffn[0].reshape(1, -1)
    gfin = g_final.reshape(1, -1)
    gsub = g_sub[0].reshape(1, -1)
    bg = b_glu[0].reshape(1, -1)
    lam4 = jnp.stack([lambda_q1[0], lambda_k1[0], lambda_q2[0], lambda_k2[0]], axis=0)
    pw, bbr, bbi = _s5_prep(ssm_a_re[0], ssm_a_im[0], ssm_log_dt[0], ssm_b_re[0], ssm_b_im[0])
    bblk, cblk, tab, lam1, dvec = _s5_tables(pw, bbr, bbi, ssm_c_re[0], ssm_c_im[0], ssm_d[0])

    xp = x_prompt.reshape(mp, D_MODEL)
    cos_p, sin_p = _rope_tables(jnp.tile(jnp.arange(tp), bp))
    qb, kp, kpb, vp, vpb, up = _inproj(xp, gmix, w_in_bf, cos_p, sin_p, 512)
    attn_p = _attn_prompt(lam4, qb, kpb, vpb, gsub, bp, tp, 512, 128)
    zeros = jnp.zeros((bp, SSM_G * SSM_P), F32)
    yp, rp, ip = _s5_prompt(up, bblk, cblk, dvec, tab, zeros, zeros, bp, tp, 1024)
    x2p, h2pt = _outproj(xp, attn_p, yp, wg_bf, bg, wo_bf, gffn, 256, True)
    peer_p = _peer(h2pt, wqt_bf, keys_bf, u_bf, vt_bf, 512, PEER_NA)
    y_prompt = _final(x2p, peer_p, gfin, 512).reshape(bp, tp, D_MODEL)

    xs = jnp.pad(x_sample.reshape(bs, D_MODEL), ((0, srows - bs), (0, 0)))
    cos_s, sin_s = _rope_tables(jnp.full((srows,), past_len))
    qs, ks, _, vs, _, us = _inproj(xs, gmix, w_in_bf, cos_s, sin_s, srows)
    heads = lambda a: a[:bs].astype(F32).reshape(bs, 1, HEADS, VD)
    attn_s = _attn_sample(page_table, lam4, heads(qs)[:, 0], cache_k, cache_v, heads(ks), heads(vs),
                          gsub, 8 if n_pages % 8 == 0 else 1)
    pad_s = lambda a: jnp.pad(a.reshape(bs, -1), ((0, srows - bs), (0, 0)))
    ys, rs, is_ = _s5_sample(us, bblk, cblk, dvec, lam1, pad_s(state_ssm_re[0]), pad_s(state_ssm_im[0]))
    x2s, h2s = _outproj(xs, pad_s(attn_s), ys, wg_bf, bg, wo_bf, gffn, srows, False)
    h2s_rep = jnp.tile(h2s[:bs], (128 // bs, 1))
    peer_s = _peer(h2s_rep.T, wqt_bf, keys_bf, u_bf, vt_bf, 128, PEER_NA)
    y_sample = _final(x2s, peer_s, gfin, srows)[:bs].reshape(bs, ts, D_MODEL)

    return (y_prompt, y_sample,
            kp.reshape(1, bp, tp, HEADS, 2 * QK), vp.reshape(1, bp, tp, HEADS, VD),
            rp[None], ip[None],
            ks[:bs].reshape(1, bs, ts, HEADS, 2 * QK), vs[:bs].reshape(1, bs, ts, HEADS, VD),
            rs[:bs].reshape(1, bs, SSM_G, SSM_P), is_[:bs].reshape(1, bs, SSM_G, SSM_P))
```
